```python
import math, functools
import jax, jax.numpy as jnp
from jax import lax
import numpy as np

D_MODEL = 2048
BATCH = 2
SEQ = 4096
DEPTH = 4
DEC_BATCH = 128
DEC_SEQ = 4
PAST_LEN = 8192
PAGE_SIZE = 128

MIX_W = D_MODEL
SSM_W = MIX_W // 2
SSM_CH = 16
SSM_GROUPS = SSM_W // SSM_CH
SSM_STATE = 64
V_HEAD_DIM = 128
N_HEADS = (MIX_W - SSM_W) // V_HEAD_DIM
QK_NOPE = 128
QK_ROPE = 64
QK_DIM = QK_NOPE + QK_ROPE
KV_RANK = D_MODEL // 8
ROPE_THETA = 10000.0
Q_BLOCK = 128
MOE_GROUPS = 4
EXPERTS_PER_GROUP = 4
N_EXPERTS = MOE_GROUPS * EXPERTS_PER_GROUP
TOP_K = 2
EXPERT_FF = D_MODEL // 4
IN_COLS = SSM_W + N_HEADS * QK_DIM + KV_RANK + QK_ROPE
EPS = 1e-6
SM_SCALE = QK_DIM ** -0.5
POOL_NUM = 5
POOL_DEN = 4

kernel_name = 'hybrid_s5_mla_hmoe_decode_step'


def rmsnorm(x, g):
    xf = x.astype(jnp.float32)
    y = xf * lax.rsqrt(jnp.mean(xf * xf, axis=-1, keepdims=True) + EPS)
    return (y * g.astype(jnp.float32)).astype(x.dtype)


def adaln(c, w, b):
    m = jax.nn.silu(c) @ w + b
    return jnp.split(m, 6, axis=-1)


def modulate(x, g, shift, scale):
    return rmsnorm(x, g) * (1.0 + scale[:, None, :]) + shift[:, None, :]


def rope_tables(pos):
    half = QK_ROPE // 2
    inv = ROPE_THETA ** (-jnp.arange(half, dtype=jnp.float32) * (2.0 / QK_ROPE))
    ang = pos.astype(jnp.float32)[:, None] * inv[None, :]
    return jnp.cos(ang), jnp.sin(ang)


def apply_rope(x, cos, sin):
    xf = x.astype(jnp.float32)
    x1, x2 = jnp.split(xf, 2, axis=-1)
    return jnp.concatenate([x1 * cos - x2 * sin, x2 * cos + x1 * sin], axis=-1).astype(x.dtype)


def project_in(h, w_in):
    z = h @ w_in
    u = z[..., :SSM_W]
    off = SSM_W + N_HEADS * QK_DIM
    q = z[..., SSM_W:off].reshape(z.shape[:-1] + (N_HEADS, QK_DIM))
    kv = z[..., off:off + KV_RANK]
    kr = z[..., off + KV_RANK:]
    return u, q, kv, kr


def _complex_affine_combine(e1, e2):
    a1r, a1i, b1r, b1i = e1
    a2r, a2i, b2r, b2i = e2
    ar = a2r * a1r - a2i * a1i
    ai = a2r * a1i + a2i * a1r
    br = a2r * b1r - a2i * b1i + b2r
    bi = a2r * b1i + a2i * b1r + b2i
    return (ar, ai, br, bi)


def ssm_glu(u, h0_re, h0_im, lam_re, lam_im, log_dt, b_re, b_im, c_re, c_im, d_skip, w_glu, b_glu):
    f32 = jnp.float32
    bsz, t, _ = u.shape
    uf = u.astype(f32).reshape(bsz, t, SSM_GROUPS, SSM_CH)
    lr = lam_re.astype(f32)
    li = lam_im.astype(f32)
    dt = jnp.exp(log_dt.astype(f32))[:, None]
    mag = jnp.exp(lr * dt)
    ab_re = mag * jnp.cos(li * dt)
    ab_im = mag * jnp.sin(li * dt)
    den = lr * lr + li * li
    f_re = ((ab_re - 1.0) * lr + ab_im * li) / den
    f_im = (ab_im * lr - (ab_re - 1.0) * li) / den
    br = b_re.astype(f32)
    bi = b_im.astype(f32)
    bb_re = f_re[..., None] * br - f_im[..., None] * bi
    bb_im = f_re[..., None] * bi + f_im[..., None] * br
    bu_re = jnp.einsum('gnp,btgp->btgn', bb_re, uf)
    bu_im = jnp.einsum('gnp,btgp->btgn', bb_im, uf)
    h0r = h0_re.astype(f32)
    h0i = h0_im.astype(f32)
    bu_re = bu_re.at[:, 0].add(ab_re * h0r - ab_im * h0i)
    bu_im = bu_im.at[:, 0].add(ab_re * h0i + ab_im * h0r)
    a_re = jnp.broadcast_to(ab_re, bu_re.shape)
    a_im = jnp.broadcast_to(ab_im, bu_im.shape)
    _, _, h_re, h_im = lax.associative_scan(_complex_affine_combine, (a_re, a_im, bu_re, bu_im), axis=1)
    y = (jnp.einsum('gpn,btgn->btgp', c_re.astype(f32), h_re)
         - jnp.einsum('gpn,btgn->btgp', c_im.astype(f32), h_im)
         + d_skip.astype(f32).reshape(SSM_GROUPS, SSM_CH) * uf)
    g = jax.nn.gelu(y.reshape(bsz, t, SSM_W))
    out = g * jax.nn.sigmoid(g @ w_glu.astype(f32) + b_glu.astype(f32))
    return out.astype(u.dtype), h_re[:, -1].astype(h0_re.dtype), h_im[:, -1].astype(h0_im.dtype)


def mla_prepare(q, kv, kr, pos, kv_norm_g, w_uk):
    cos, sin = rope_tables(pos)
    q_nope = q[..., :QK_NOPE]
    q_pe = apply_rope(q[..., QK_NOPE:], cos[:, None, :], sin[:, None, :]) * SM_SCALE
    k_pe = apply_rope(kr, cos, sin)
    c_kv = rmsnorm(kv, kv_norm_g)
    q_lat = jnp.einsum('bthd,rhd->bthr', q_nope, w_uk) * SM_SCALE
    return q_lat, q_pe, c_kv, k_pe


def attend_prompt(q_lat, q_pe, c_kv, k_pe):
    bsz, t = q_lat.shape[:2]
    nb = t // Q_BLOCK
    ql = jnp.moveaxis(q_lat.reshape(bsz, nb, Q_BLOCK, N_HEADS, KV_RANK), 1, 0)
    qp = jnp.moveaxis(q_pe.reshape(bsz, nb, Q_BLOCK, N_HEADS, QK_ROPE), 1, 0)
    kpos = jnp.arange(t)

    def block(args):
        i, qlb, qpb = args
        qpos = i * Q_BLOCK + jnp.arange(Q_BLOCK)
        s = (jnp.einsum('bqhr,bkr->bhqk', qlb, c_kv)
             + jnp.einsum('bqhe,bke->bhqk', qpb, k_pe)).astype(jnp.float32)
        s = jnp.where(kpos[None, :] <= qpos[:, None], s, -jnp.inf)
        p = jax.nn.softmax(s, axis=-1).astype(c_kv.dtype)
        return jnp.einsum('bhqk,bkr->bqhr', p, c_kv)

    o = lax.map(block, (jnp.arange(nb), ql, qp))
    return jnp.moveaxis(o, 0, 1).reshape(bsz, t, N_HEADS, KV_RANK)


def attend_sample(q_lat, q_pe, c_kv, k_pe, lat_past, pe_past):
    t = q_lat.shape[1]
    n_past = lat_past.shape[1]
    s_past = (jnp.einsum('bqhr,bkr->bhqk', q_lat, lat_past)
              + jnp.einsum('bqhe,bke->bhqk', q_pe, pe_past)).astype(jnp.float32)
    s_new = (jnp.einsum('bqhr,bkr->bhqk', q_lat, c_kv)
             + jnp.einsum('bqhe,bke->bhqk', q_pe, k_pe)).astype(jnp.float32)
    causal = jnp.arange(t)[None, :] <= jnp.arange(t)[:, None]
    s_new = jnp.where(causal, s_new, -jnp.inf)
    p = jax.nn.softmax(jnp.concatenate([s_past, s_new], axis=-1), axis=-1).astype(c_kv.dtype)
    return (jnp.einsum('bhqk,bkr->bqhr', p[..., :n_past], lat_past)
            + jnp.einsum('bhqk,bkr->bqhr', p[..., n_past:], c_kv))


def hier_moe(h, w_rg, b_rg, w_re, b_re, w_g, w_u, w_d):
    shp = h.shape
    t = h.reshape(-1, shp[-1])
    lg = (t @ w_rg + b_rg).astype(jnp.float32)
    pg = jax.nn.softmax(lg, axis=-1)
    gsel = jnp.argmax(lg, axis=-1)
    wg = jnp.take_along_axis(pg, gsel[:, None], axis=-1)[:, 0]
    le = (jnp.einsum('nd,dge->nge', t, w_re) + b_re).astype(jnp.float32)
    le_sel = jnp.take_along_axis(le, gsel[:, None, None], axis=1)[:, 0]
    top_v, top_i = lax.top_k(le_sel, TOP_K)
    we = jax.nn.softmax(top_v, axis=-1) * wg[:, None]
    eidx = gsel[:, None] * EXPERTS_PER_GROUP + top_i
    gates = jnp.sum(jax.nn.one_hot(eidx, N_EXPERTS, dtype=jnp.float32) * we[..., None], axis=1)
    a = jnp.einsum('nd,xdf->nxf', t, w_g)
    b = jnp.einsum('nd,xdf->nxf', t, w_u)
    act = jax.nn.silu(a) * b * gates[..., None].astype(t.dtype)
    return jnp.einsum('nxf,xfd->nd', act, w_d).reshape(shp)


def setup_inputs(seed: int = 0) -> dict:
    key = jax.random.key(seed)
    ks = iter(jax.random.split(key, 64))
    f32 = jnp.float32

    def nrm(shape, scale):
        return jax.random.normal(next(ks), shape, f32) * scale

    n_pages = PAST_LEN // PAGE_SIZE
    n_pool = (DEC_BATCH * n_pages * POOL_NUM) // POOL_DEN
    x_prompt = nrm((BATCH, SEQ, D_MODEL), 1.0)
    x_sample = nrm((DEC_BATCH, DEC_SEQ, D_MODEL), 1.0)
    cache_kv_latent = nrm((DEPTH, n_pool, PAGE_SIZE, KV_RANK), 1.0)
    cache_k_rope = nrm((DEPTH, n_pool, PAGE_SIZE, QK_ROPE), 1.0)
    state_ssm_re = nrm((DEPTH, DEC_BATCH, SSM_GROUPS, SSM_STATE), 0.5)
    state_ssm_im = nrm((DEPTH, DEC_BATCH, SSM_GROUPS, SSM_STATE), 0.5)
    perm = jax.random.permutation(next(ks), n_pool)
    page_table = perm[: DEC_BATCH * n_pages].reshape(DEC_BATCH, n_pages).astype(jnp.int32)
    c_prompt = nrm((BATCH, D_MODEL), 1.0)
    c_sample = nrm((DEC_BATCH, D_MODEL), 1.0)

    norm_mix_g = 1.0 + nrm((DEPTH, D_MODEL), 0.02)
    norm_ffn_g = 1.0 + nrm((DEPTH, D_MODEL), 0.02)
    w_ada = nrm((DEPTH, D_MODEL, 6 * D_MODEL), 0.5 * D_MODEL ** -0.5)
    b_ada = nrm((DEPTH, 6 * D_MODEL), 0.02)
    w_in = nrm((DEPTH, D_MODEL, IN_COLS), D_MODEL ** -0.5)
    ssm_lam_re = -0.5 + nrm((DEPTH, SSM_GROUPS, SSM_STATE), 0.01)
    ssm_lam_im = jnp.pi * jnp.arange(SSM_STATE, dtype=f32) + nrm((DEPTH, SSM_GROUPS, SSM_STATE), 0.01)
    ssm_log_dt = jax.random.uniform(next(ks), (DEPTH, SSM_GROUPS), f32, math.log(1e-3), math.log(1e-1))
    ssm_b_re = nrm((DEPTH, SSM_GROUPS, SSM_STATE, SSM_CH), (2.0 * SSM_CH) ** -0.5)
    ssm_b_im = nrm((DEPTH, SSM_GROUPS, SSM_STATE, SSM_CH), (2.0 * SSM_CH) ** -0.5)
    ssm_c_re = nrm((DEPTH, SSM_GROUPS, SSM_CH, SSM_STATE), (2.0 * SSM_STATE) ** -0.5)
    ssm_c_im = nrm((DEPTH, SSM_GROUPS, SSM_CH, SSM_STATE), (2.0 * SSM_STATE) ** -0.5)
    ssm_d = nrm((DEPTH, SSM_W), 1.0)
    w_glu = nrm((DEPTH, SSM_W, SSM_W), SSM_W ** -0.5)
    b_glu = nrm((DEPTH, SSM_W), 0.02)
    kv_norm_g = 1.0 + nrm((DEPTH, KV_RANK), 0.02)
    w_uk = nrm((DEPTH, KV_RANK, N_HEADS, QK_NOPE), KV_RANK ** -0.5)
    w_uv = nrm((DEPTH, KV_RANK, N_HEADS, V_HEAD_DIM), KV_RANK ** -0.5)
    g_ssm_out = 1.0 + nrm((DEPTH, SSM_W), 0.02)
    g_attn_out = 1.0 + nrm((DEPTH, N_HEADS * V_HEAD_DIM), 0.02)
    w_out = nrm((DEPTH, MIX_W, D_MODEL), MIX_W ** -0.5)
    w_route_group = nrm((DEPTH, D_MODEL, MOE_GROUPS), D_MODEL ** -0.5)
    b_route_group = nrm((DEPTH, MOE_GROUPS), 0.01)
    w_route_expert = nrm((DEPTH, D_MODEL, MOE_GROUPS, EXPERTS_PER_GROUP), D_MODEL ** -0.5)
    b_route_expert = nrm((DEPTH, MOE_GROUPS, EXPERTS_PER_GROUP), 0.01)
    w_exp_gate = nrm((DEPTH, N_EXPERTS, D_MODEL, EXPERT_FF), D_MODEL ** -0.5)
    w_exp_up = nrm((DEPTH, N_EXPERTS, D_MODEL, EXPERT_FF), D_MODEL ** -0.5)
    w_exp_down = nrm((DEPTH, N_EXPERTS, EXPERT_FF, D_MODEL), EXPERT_FF ** -0.5)
    final_norm_g = 1.0 + nrm((D_MODEL,), 0.02)
    return {
        'x_prompt': x_prompt, 'x_sample': x_sample,
        'cache_kv_latent': cache_kv_latent, 'cache_k_rope': cache_k_rope,
        'state_ssm_re': state_ssm_re, 'state_ssm_im': state_ssm_im,
        'page_table': page_table, 'c_prompt': c_prompt, 'c_sample': c_sample,
        'norm_mix_g': norm_mix_g, 'norm_ffn_g': norm_ffn_g, 'w_ada': w_ada, 'b_ada': b_ada,
        'w_in': w_in, 'ssm_lam_re': ssm_lam_re, 'ssm_lam_im': ssm_lam_im, 'ssm_log_dt': ssm_log_dt,
        'ssm_b_re': ssm_b_re, 'ssm_b_im': ssm_b_im, 'ssm_c_re': ssm_c_re, 'ssm_c_im': ssm_c_im,
        'ssm_d': ssm_d, 'w_glu': w_glu, 'b_glu': b_glu, 'kv_norm_g': kv_norm_g,
        'w_uk': w_uk, 'w_uv': w_uv, 'g_ssm_out': g_ssm_out, 'g_attn_out': g_attn_out, 'w_out': w_out,
        'w_route_group': w_route_group, 'b_route_group': b_route_group,
        'w_route_expert': w_route_expert, 'b_route_expert': b_route_expert,
        'w_exp_gate': w_exp_gate, 'w_exp_up': w_exp_up, 'w_exp_down': w_exp_down,
        'final_norm_g': final_norm_g,
    }


def reference(x_prompt, x_sample, cache_kv_latent, cache_k_rope, state_ssm_re, state_ssm_im,
              page_table, c_prompt, c_sample, norm_mix_g, norm_ffn_g, w_ada, b_ada, w_in,
              ssm_lam_re, ssm_lam_im, ssm_log_dt, ssm_b_re, ssm_b_im, ssm_c_re, ssm_c_im, ssm_d,
              w_glu, b_glu, kv_norm_g, w_uk, w_uv, g_ssm_out, g_attn_out, w_out,
              w_route_group, b_route_group, w_route_expert, b_route_expert,
              w_exp_gate, w_exp_up, w_exp_down, final_norm_g):
    bp, tp = x_prompt.shape[:2]
    bs, ts = x_sample.shape[:2]
    past_len = page_table.shape[1] * cache_kv_latent.shape[2]
    pos_p = jnp.arange(tp)
    pos_s = past_len + jnp.arange(ts)
    zero_state = jnp.zeros((bp, SSM_GROUPS, SSM_STATE), x_prompt.dtype)

    def layer(x, c, pos, h0_re, h0_im, attend, l):
        shift1, scale1, gate1, shift2, scale2, gate2 = adaln(c, w_ada[l], b_ada[l])
        h = modulate(x, norm_mix_g[l], shift1, scale1)
        u, q, kv, kr = project_in(h, w_in[l])
        y_ssm, s_re, s_im = ssm_glu(u, h0_re, h0_im, ssm_lam_re[l], ssm_lam_im[l], ssm_log_dt[l],
                                    ssm_b_re[l], ssm_b_im[l], ssm_c_re[l], ssm_c_im[l], ssm_d[l],
                                    w_glu[l], b_glu[l])
        q_lat, q_pe, c_kv, k_pe = mla_prepare(q, kv, kr, pos, kv_norm_g[l], w_uk[l])
        o_lat = attend(q_lat, q_pe, c_kv, k_pe)
        o = jnp.einsum('bthr,rhd->bthd', o_lat, w_uv[l]).reshape(x.shape[0], x.shape[1], N_HEADS * V_HEAD_DIM)
        merged = jnp.concatenate([rmsnorm(y_ssm, g_ssm_out[l]), rmsnorm(o, g_attn_out[l])], axis=-1) @ w_out[l]
        x = x + gate1[:, None, :] * merged
        h2 = modulate(x, norm_ffn_g[l], shift2, scale2)
        x = x + gate2[:, None, :] * hier_moe(h2, w_route_group[l], b_route_group[l], w_route_expert[l],
                                             b_route_expert[l], w_exp_gate[l], w_exp_up[l], w_exp_down[l])
        return x, c_kv, k_pe, s_re, s_im

    xp, xs = x_prompt, x_sample
    lat_p, pe_p, sre_p, sim_p = [], [], [], []
    lat_s, pe_s, sre_s, sim_s = [], [], [], []
    for l in range(DEPTH):
        xp, a1, a2, a3, a4 = layer(xp, c_prompt, pos_p, zero_state, zero_state, attend_prompt, l)
        lat_p.append(a1)
        pe_p.append(a2)
        sre_p.append(a3)
        sim_p.append(a4)
        lat_past = cache_kv_latent[l, page_table].reshape(bs, past_len, KV_RANK)
        pe_past = cache_k_rope[l, page_table].reshape(bs, past_len, QK_ROPE)
        attend_s = functools.partial(attend_sample, lat_past=lat_past, pe_past=pe_past)
        xs, b1, b2, b3, b4 = layer(xs, c_sample, pos_s, state_ssm_re[l], state_ssm_im[l], attend_s, l)
        lat_s.append(b1)
        pe_s.append(b2)
        sre_s.append(b3)
        sim_s.append(b4)

    y_prompt = rmsnorm(xp, final_norm_g)
    y_sample = rmsnorm(xs, final_norm_g)
    return (y_prompt, y_sample,
            jnp.stack(lat_p), jnp.stack(pe_p), jnp.stack(sre_p), jnp.stack(sim_p),
            jnp.stack(lat_s), jnp.stack(pe_s), jnp.stack(sre_s), jnp.stack(sim_s))
```

```python
import functools
import math

import numpy as np
import jax
import jax.numpy as jnp
from jax import lax
from jax.experimental import pallas as pl
from jax.experimental.pallas import tpu as pltpu

F32 = jnp.float32
BF16 = jnp.bfloat16
I32 = jnp.int32

SSM_CH = 16
SSM_STATE = 64
N_HEADS = 8
QK_NOPE = 128
QK_ROPE = 64
QK_DIM = QK_NOPE + QK_ROPE
V_HEAD_DIM = 128
ROPE_THETA = 10000.0
EPS = 1e-6
SM_SCALE = QK_DIM ** -0.5
MOE_GROUPS = 4
EXPERTS_PER_GROUP = 4
N_EXPERTS = MOE_GROUPS * EXPERTS_PER_GROUP
TOP_K = 2

LANES = 128
MXU_DIM = 256
VMEM_LIMIT = 56 * 1024 * 1024

GROUPS_PER_BLOCK = MXU_DIM // SSM_CH
SSM_CHUNK = 64
NEG_BIG = -1e30


def _cparams(n_axes):
    return pltpu.CompilerParams(dimension_semantics=("arbitrary",) * n_axes,
                                vmem_limit_bytes=VMEM_LIMIT)


def _rms(x, g):
    return x * lax.rsqrt(jnp.mean(x * x, axis=-1, keepdims=True) + EPS) * g


def _ada_body(c_ref, w_ref, b_ref, o_ref):
    c = c_ref[...]
    s = (c * jax.nn.sigmoid(c)).astype(BF16)
    o_ref[0] = jnp.dot(s, w_ref[0].astype(BF16), preferred_element_type=F32) + b_ref[0]


def _ada(c_all, w_ada, b_ada):
    depth, d, n6 = w_ada.shape
    r = c_all.shape[0]
    tn = 1024
    return pl.pallas_call(
        _ada_body,
        grid=(depth, n6 // tn),
        in_specs=[pl.BlockSpec((r, d), lambda l, j: (0, 0)),
                  pl.BlockSpec((1, d, tn), lambda l, j: (l, 0, j)),
                  pl.BlockSpec((1, 1, tn), lambda l, j: (l, 0, j))],
        out_specs=pl.BlockSpec((1, r, tn), lambda l, j: (l, 0, j)),
        out_shape=jax.ShapeDtypeStruct((depth, r, n6), F32),
        compiler_params=_cparams(2),
        name="ada_ln",
    )(c_all, w_ada, b_ada.reshape(depth, 1, n6))


def _rope_pairs(p, cos4, sin4, first_half):
    swapped = jnp.where(first_half, pltpu.roll(p, 96, 1), pltpu.roll(p, 32, 1))
    return p * cos4 + swapped * sin4


def _inproj_body(x_ref, g_ref, sh_ref, sc_ref, w_ref, wuk_ref, kvg_ref, cos_ref, sin_ref,
                 u_ref, q_ref, kcat_ref, ckv_ref, kpe_ref, *, mod_bcast, rope_bcast, ssm_w, kv_rank):
    x = x_ref[...]
    tm = x.shape[0]
    sh = sh_ref[0] if mod_bcast else sh_ref[...]
    sc = sc_ref[0] if mod_bcast else sc_ref[...]
    h = _rms(x, g_ref[...]) * (1.0 + sc) + sh
    z = jnp.dot(h.astype(BF16), w_ref[...], preferred_element_type=F32)
    u_ref[...] = z[:, :ssm_w]
    cos4 = cos_ref[0] if rope_bcast else cos_ref[...]
    sin4 = sin_ref[0] if rope_bcast else sin_ref[...]
    lane = lax.broadcasted_iota(I32, (tm, LANES), 1)
    first_half = (lane % QK_ROPE) < (QK_ROPE // 2)
    off_nope = ssm_w
    off_rope = off_nope + N_HEADS * QK_NOPE
    off_kv = off_rope + N_HEADS * QK_ROPE
    off_kr = off_kv + kv_rank
    for hp in range(N_HEADS // 2):
        pr = z[:, off_rope + hp * LANES: off_rope + (hp + 1) * LANES]
        rp = (_rope_pairs(pr, cos4, sin4, first_half) * SM_SCALE).astype(BF16)
        q_ref[2 * hp, :, kv_rank:kv_rank + QK_ROPE] = rp[:, :QK_ROPE]
        q_ref[2 * hp + 1, :, kv_rank:kv_rank + QK_ROPE] = rp[:, QK_ROPE:]
    for hd in range(N_HEADS):
        nope = z[:, off_nope + hd * QK_NOPE: off_nope + (hd + 1) * QK_NOPE].astype(BF16)
        ql = jnp.dot(nope, wuk_ref[hd], preferred_element_type=F32) * SM_SCALE
        q_ref[hd, :, :kv_rank] = ql.astype(BF16)
    ckv = _rms(z[:, off_kv:off_kr], kvg_ref[...])
    ckv_ref[...] = ckv
    kk = _rope_pairs(z[:, off_kr:off_kr + LANES], cos4, sin4, first_half)
    kpe_ref[...] = kk[:, :QK_ROPE]
    kcat_ref[:, :kv_rank] = ckv.astype(BF16)
    kcat_ref[:, kv_rank:kv_rank + QK_ROPE] = kk[:, :QK_ROPE].astype(BF16)


def _inproj(x, g, shift, scale, w, wuk_t, kvg, cos4, sin4, *, tm, mod_bcast, rope_bcast, rows_per_batch):
    r, d = x.shape
    ncols = w.shape[1]
    kv_rank = wuk_t.shape[2]
    ssm_w = d // 2
    qd = kv_rank + QK_ROPE
    tpb = rows_per_batch // tm if mod_bcast else 1
    if mod_bcast:
        mod_spec = pl.BlockSpec((1, 1, d), lambda i: (i // tpb, 0, 0))
    else:
        mod_spec = pl.BlockSpec((tm, d), lambda i: (0, 0))
    if rope_bcast:
        rope_spec = pl.BlockSpec((1, 1, LANES), lambda i: (i, 0, 0))
    else:
        rope_spec = pl.BlockSpec((tm, LANES), lambda i: (i % tpb, 0))
    body = functools.partial(_inproj_body, mod_bcast=mod_bcast, rope_bcast=rope_bcast,
                             ssm_w=ssm_w, kv_rank=kv_rank)
    return pl.pallas_call(
        body,
        grid=(r // tm,),
        in_specs=[pl.BlockSpec((tm, d), lambda i: (i, 0)),
                  pl.BlockSpec((1, d), lambda i: (0, 0)),
                  mod_spec, mod_spec,
                  pl.BlockSpec((d, ncols), lambda i: (0, 0)),
                  pl.BlockSpec(wuk_t.shape, lambda i: (0, 0, 0)),
                  pl.BlockSpec((1, kv_rank), lambda i: (0, 0)),
                  rope_spec, rope_spec],
        out_specs=[pl.BlockSpec((tm, ssm_w), lambda i: (i, 0)),
                   pl.BlockSpec((N_HEADS, tm, qd), lambda i: (0, i, 0)),
                   pl.BlockSpec((tm, qd), lambda i: (i, 0)),
                   pl.BlockSpec((tm, kv_rank), lambda i: (i, 0)),
                   pl.BlockSpec((tm, QK_ROPE), lambda i: (i, 0))],
        out_shape=[jax.ShapeDtypeStruct((r, ssm_w), F32),
                   jax.ShapeDtypeStruct((N_HEADS, r, qd), BF16),
                   jax.ShapeDtypeStruct((r, qd), BF16),
                   jax.ShapeDtypeStruct((r, kv_rank), F32),
                   jax.ShapeDtypeStruct((r, QK_ROPE), F32)],
        compiler_params=_cparams(1),
        name="in_proj_mla_prep",
    )(x, g, shift, scale, w, wuk_t, kvg, cos4, sin4)


def _glu_norm(y, wglu_ref, bglu_ref, gout_ref):
    g = jax.nn.gelu(y)
    gate = jnp.dot(g.astype(BF16), wglu_ref[...], preferred_element_type=F32) + bglu_ref[...]
    return _rms(g * jax.nn.sigmoid(gate), gout_ref[...])


def _ssm_prompt_body(u_ref, bbd_ref, cbd_ref, tri_ref, pre_re_ref, pre_im_ref, post_re_ref, post_im_ref,
                     a_re_ref, a_im_ref, d_ref, wglu_ref, bglu_ref, gout_ref,
                     y_ref, sre_ref, sim_ref, bu_scr, h_scr, hprev_scr, yacc_scr, *, n_blocks, chunk):
    tc = pl.program_id(1)
    ts = u_ref.shape[1]
    sl = GROUPS_PER_BLOCK * SSM_STATE

    @pl.when(tc == 0)
    def _():
        hprev_scr[...] = jnp.zeros_like(hprev_scr)

    u = u_ref[0]
    ub = u.astype(BF16)
    tri = tri_ref[...]
    for j in range(n_blocks):
        cols = slice(j * MXU_DIM, (j + 1) * MXU_DIM)
        lanes = slice(j * sl, (j + 1) * sl)
        bu_scr[...] = jnp.dot(ub[:, cols], bbd_ref[j], preferred_element_type=F32)
        a_re = a_re_ref[:, lanes]
        a_im = a_im_ref[:, lanes]

        def sub(s, carry):
            hr, hi = carry
            r0 = pl.multiple_of(s * chunk, chunk)
            b_re = bu_scr[pl.ds(r0, chunk), 0:sl]
            b_im = bu_scr[pl.ds(r0, chunk), sl:2 * sl]
            p_re = pre_re_ref[:, lanes]
            p_im = pre_im_ref[:, lanes]
            x_re = (p_re * b_re - p_im * b_im).astype(BF16)
            x_im = (p_re * b_im + p_im * b_re).astype(BF16)
            z_re = jnp.dot(tri, x_re, preferred_element_type=F32) + (a_re * hr - a_im * hi)
            z_im = jnp.dot(tri, x_im, preferred_element_type=F32) + (a_re * hi + a_im * hr)
            q_re = post_re_ref[:, lanes]
            q_im = post_im_ref[:, lanes]
            h_re = q_re * z_re - q_im * z_im
            h_im = q_re * z_im + q_im * z_re
            h_scr[pl.ds(r0, chunk), 0:sl] = h_re.astype(BF16)
            h_scr[pl.ds(r0, chunk), sl:2 * sl] = h_im.astype(BF16)
            return h_re[chunk - 1:chunk], h_im[chunk - 1:chunk]

        hr, hi = lax.fori_loop(0, ts // chunk, sub,
                               (hprev_scr[2 * j:2 * j + 1], hprev_scr[2 * j + 1:2 * j + 2]))
        hprev_scr[2 * j:2 * j + 1] = hr
        hprev_scr[2 * j + 1:2 * j + 2] = hi
        sre_ref[0, :, lanes] = hr
        sim_ref[0, :, lanes] = hi
        yacc_scr[:, cols] = (jnp.dot(h_scr[...], cbd_ref[j], preferred_element_type=F32)
                             + d_ref[:, cols] * u[:, cols])
    y_ref[0] = _glu_norm(yacc_scr[...], wglu_ref, bglu_ref, gout_ref).astype(BF16)


def _ssm_prompt(u, tabs, wglu, bglu, gout, *, ts):
    b, t, w = u.shape
    n_blocks = w // MXU_DIM
    sl2 = 2 * GROUPS_PER_BLOCK * SSM_STATE
    nst = (w // SSM_CH) * SSM_STATE
    chunk = SSM_CHUNK
    full2 = lambda shape: pl.BlockSpec(shape, lambda i, j: (0,) * len(shape))
    body = functools.partial(_ssm_prompt_body, n_blocks=n_blocks, chunk=chunk)
    return pl.pallas_call(
        body,
        grid=(b, t // ts),
        in_specs=[pl.BlockSpec((1, ts, w), lambda i, j: (i, j, 0)),
                  full2((n_blocks, MXU_DIM, sl2)), full2((n_blocks, sl2, MXU_DIM)),
                  full2((chunk, chunk)),
                  full2((chunk, nst)), full2((chunk, nst)), full2((chunk, nst)), full2((chunk, nst)),
                  full2((1, nst)), full2((1, nst)), full2((1, w)),
                  full2((w, w)), full2((1, w)), full2((1, w))],
        out_specs=[pl.BlockSpec((1, ts, w), lambda i, j: (i, j, 0)),
                   pl.BlockSpec((1, 1, nst), lambda i, j: (i, 0, 0)),
                   pl.BlockSpec((1, 1, nst), lambda i, j: (i, 0, 0))],
        out_shape=[jax.ShapeDtypeStruct((b, t, w), BF16),
                   jax.ShapeDtypeStruct((b, 1, nst), F32),
                   jax.ShapeDtypeStruct((b, 1, nst), F32)],
        scratch_shapes=[pltpu.VMEM((ts, sl2), F32), pltpu.VMEM((ts, sl2), BF16),
                        pltpu.VMEM((2 * n_blocks, sl2 // 2), F32), pltpu.VMEM((ts, w), F32)],
        compiler_params=_cparams(2),
        name="ssm_prompt",
    )(u, tabs["bbd"], tabs["cbd"], tabs["tri"], tabs["pre_re"], tabs["pre_im"], tabs["post_re"],
      tabs["post_im"], tabs["a_re"], tabs["a_im"], tabs["d"], wglu, bglu, gout)


def _ssm_sample_body(u_ref, bbd_ref, cbd_ref, a_re_ref, a_im_ref, d_ref, h0re_ref, h0im_ref,
                     wglu_ref, bglu_ref, gout_ref, y_ref, sre_ref, sim_ref,
                     bu_scr, h_scr, yacc_scr, *, n_blocks, n_steps):
    sl = GROUPS_PER_BLOCK * SSM_STATE
    bs = h0re_ref.shape[0]
    u = u_ref[...]
    ub = u.astype(BF16)
    for j in range(n_blocks):
        cols = slice(j * MXU_DIM, (j + 1) * MXU_DIM)
        lanes = slice(j * sl, (j + 1) * sl)
        bu_scr[...] = jnp.dot(ub[:, cols], bbd_ref[j], preferred_element_type=F32)
        a_re = a_re_ref[:, lanes]
        a_im = a_im_ref[:, lanes]
        hr = h0re_ref[:, lanes]
        hi = h0im_ref[:, lanes]
        for t in range(n_steps):
            rows = slice(t * bs, (t + 1) * bs)
            nr = (a_re * hr - a_im * hi) + bu_scr[rows, 0:sl]
            ni = (a_re * hi + a_im * hr) + bu_scr[rows, sl:2 * sl]
            hr, hi = nr, ni
            h_scr[rows, 0:sl] = hr.astype(BF16)
            h_scr[rows, sl:2 * sl] = hi.astype(BF16)
        sre_ref[:, lanes] = hr
        sim_ref[:, lanes] = hi
        yacc_scr[:, cols] = (jnp.dot(h_scr[...], cbd_ref[j], preferred_element_type=F32)
                             + d_ref[:, cols] * u[:, cols])
    y_ref[...] = _glu_norm(yacc_scr[...], wglu_ref, bglu_ref, gout_ref).astype(BF16)


def _ssm_sample(u, tabs, h0re, h0im, wglu, bglu, gout):
    r, w = u.shape
    bs, nst = h0re.shape
    n_blocks = w // MXU_DIM
    sl2 = 2 * GROUPS_PER_BLOCK * SSM_STATE
    body = functools.partial(_ssm_sample_body, n_blocks=n_blocks, n_steps=r // bs)
    return pl.pallas_call(
        body,
        out_shape=[jax.ShapeDtypeStruct((r, w), BF16),
                   jax.ShapeDtypeStruct((bs, nst), F32),
                   jax.ShapeDtypeStruct((bs, nst), F32)],
        scratch_shapes=[pltpu.VMEM((r, sl2), F32), pltpu.VMEM((r, sl2), BF16), pltpu.VMEM((r, w), F32)],
        compiler_params=pltpu.CompilerParams(vmem_limit_bytes=VMEM_LIMIT),
        name="ssm_sample",
    )(u, tabs["bbd"], tabs["cbd"], tabs["a_re"], tabs["a_im"], tabs["d"], h0re, h0im, wglu, bglu, gout)


def _ssm_tables(lam_re, lam_im, log_dt, b_re, b_im, c_re, c_im, d_skip, chunk):
    depth, g, n = lam_re.shape
    p = b_re.shape[-1]
    gb = GROUPS_PER_BLOCK
    nb = g // gb
    lr = lam_re.astype(F32)
    li = lam_im.astype(F32)
    dt = jnp.exp(log_dt.astype(F32))[..., None]
    mag = jnp.exp(lr * dt)
    ab_re = mag * jnp.cos(li * dt)
    ab_im = mag * jnp.sin(li * dt)
    den = lr * lr + li * li
    f_re = ((ab_re - 1.0) * lr + ab_im * li) / den
    f_im = (ab_im * lr - (ab_re - 1.0) * li) / den
    bb_re = f_re[..., None] * b_re - f_im[..., None] * b_im
    bb_im = f_re[..., None] * b_im + f_im[..., None] * b_re
    ks = np.arange(chunk)
    pw_re = jnp.ones((chunk, depth, g, n), F32)
    pw_im = jnp.zeros((chunk, depth, g, n), F32)
    sq_re, sq_im = ab_re, ab_im
    for bit in range(max(1, int(chunk - 1).bit_length())):
        sel = jnp.asarray(((ks >> bit) & 1).astype(np.float32))[:, None, None, None]
        m_re = sel * sq_re + (1.0 - sel)
        m_im = sel * sq_im
        pw_re, pw_im = pw_re * m_re - pw_im * m_im, pw_re * m_im + pw_im * m_re
        sq_re, sq_im = sq_re * sq_re - sq_im * sq_im, 2.0 * sq_re * sq_im
    nrm = pw_re * pw_re + pw_im * pw_im
    inv_re = pw_re / nrm
    inv_im = -pw_im / nrm
    flat = lambda a: jnp.moveaxis(a, 0, 1).reshape(depth, chunk, g * n)
    eye = jnp.eye(gb, dtype=F32)

    def bdiag_in(bb):
        bbj = bb.reshape(depth, nb, gb, n, p)
        return jnp.einsum("gh,djgnp->djgphn", eye, bbj).reshape(depth, nb, gb * p, gb * n)

    def bdiag_out(cc):
        ccj = cc.reshape(depth, nb, gb, p, n)
        return jnp.einsum("gh,djgpn->djgnhp", eye, ccj).reshape(depth, nb, gb * n, gb * p)

    bbd = jnp.concatenate([bdiag_in(bb_re), bdiag_in(bb_im)], axis=-1).astype(BF16)
    cbd = jnp.concatenate([bdiag_out(c_re.astype(F32)), bdiag_out(-c_im.astype(F32))], axis=-2).astype(BF16)
    tri = jnp.asarray(np.tril(np.ones((chunk, chunk), np.float32))).astype(BF16)
    return {
        "bbd": bbd, "cbd": cbd, "tri": tri,
        "pre_re": flat(inv_re), "pre_im": flat(inv_im),
        "post_re": flat(pw_re), "post_im": flat(pw_im),
        "a_re": ab_re.reshape(depth, 1, g * n), "a_im": ab_im.reshape(depth, 1, g * n),
        "d": d_skip.astype(F32).reshape(depth, 1, g * p),
    }


def _attn_prompt_body(qi_ref, ki_ref, last_ref, q_ref, k_ref, wuv_ref, g_ref, o_ref,
                      m_scr, l_scr, acc_scr, *, tq, tk, kv_rank):
    pidx = pl.program_id(1)
    qi = qi_ref[pidx]
    ki = ki_ref[pidx]
    rows = N_HEADS * tq

    @pl.when(ki == 0)
    def _():
        m_scr[...] = jnp.full_like(m_scr, NEG_BIG)
        l_scr[...] = jnp.zeros_like(l_scr)
        acc_scr[...] = jnp.zeros_like(acc_scr)

    q = q_ref[...].reshape(rows, q_ref.shape[2])
    k = k_ref[...]
    s = lax.dot_general(q, k, (((1,), (1,)), ((), ())), preferred_element_type=F32)
    qpos = qi * tq + (lax.broadcasted_iota(I32, (rows, tk), 0) & (tq - 1))
    kpos = ki * tk + lax.broadcasted_iota(I32, (rows, tk), 1)
    s = jnp.where(kpos <= qpos, s, NEG_BIG)
    m_prev = m_scr[...]
    m_new = jnp.maximum(m_prev, jnp.max(s, axis=-1, keepdims=True))
    alpha = jnp.exp(m_prev - m_new)
    p = jnp.exp(s - m_new)
    l_scr[...] = alpha * l_scr[...] + jnp.sum(p, axis=-1, keepdims=True)
    acc_scr[...] = alpha * acc_scr[...] + jnp.dot(p.astype(BF16), k[:, :kv_rank],
                                                  preferred_element_type=F32)
    m_scr[...] = m_new

    @pl.when(ki == last_ref[pidx])
    def _():
        o = acc_scr[...] / l_scr[...]
        outs = []
        for hd in range(N_HEADS):
            oh = o[hd * tq:(hd + 1) * tq].astype(BF16)
            outs.append(jnp.dot(oh, wuv_ref[hd], preferred_element_type=F32))
        o_ref[...] = _rms(jnp.concatenate(outs, axis=-1), g_ref[...]).astype(BF16)


def _attn_prompt(q_cat, k_cat, wuv_t, g_attn, *, batch, seq, tq, tk):
    qd = q_cat.shape[2]
    kv_rank = wuv_t.shape[1]
    nq = seq // tq
    qi_l, ki_l, last_l = [], [], []
    for qi in range(nq):
        last = (qi * tq + tq - 1) // tk
        for ki in range(last + 1):
            qi_l.append(qi)
            ki_l.append(ki)
            last_l.append(last)
    npairs = len(qi_l)
    qi_a = jnp.asarray(np.array(qi_l, np.int32))
    ki_a = jnp.asarray(np.array(ki_l, np.int32))
    last_a = jnp.asarray(np.array(last_l, np.int32))
    nqb = seq // tq
    nkb = seq // tk
    d_out = N_HEADS * V_HEAD_DIM
    grid_spec = pltpu.PrefetchScalarGridSpec(
        num_scalar_prefetch=3,
        grid=(batch, npairs),
        in_specs=[pl.BlockSpec((N_HEADS, tq, qd), lambda b, p, qi, ki, la: (0, b * nqb + qi[p], 0)),
                  pl.BlockSpec((tk, qd), lambda b, p, qi, ki, la: (b * nkb + ki[p], 0)),
                  pl.BlockSpec(wuv_t.shape, lambda b, p, qi, ki, la: (0, 0, 0)),
                  pl.BlockSpec((1, d_out), lambda b, p, qi, ki, la: (0, 0))],
        out_specs=pl.BlockSpec((tq, d_out), lambda b, p, qi, ki, la: (b * nqb + qi[p], 0)),
        scratch_shapes=[pltpu.VMEM((N_HEADS * tq, 1), F32), pltpu.VMEM((N_HEADS * tq, 1), F32),
                        pltpu.VMEM((N_HEADS * tq, kv_rank), F32)],
    )
    body = functools.partial(_attn_prompt_body, tq=tq, tk=tk, kv_rank=kv_rank)
    return pl.pallas_call(
        body,
        grid_spec=grid_spec,
        out_shape=jax.ShapeDtypeStruct((batch * seq, d_out), BF16),
        compiler_params=_cparams(2),
        name="attn_prompt",
    )(qi_a, ki_a, last_a, q_cat, k_cat, wuv_t, g_attn)


def _attn_sample_body(pt_ref, q_ref, knew_ref, wuv_ref, g_ref, lat_hbm, pe_hbm, o_ref,
                      lat_buf, pe_buf, sems, *, layer, n_pages, page, kv_rank, n_new):
    b = pl.program_id(0)
    nb = pl.num_programs(0)

    def page_copies(bb, slot, j):
        pg = pt_ref[bb * n_pages + j]
        return (pltpu.make_async_copy(lat_hbm.at[layer, pg], lat_buf.at[slot, pl.ds(j * page, page)],
                                      sems.at[slot, 0]),
                pltpu.make_async_copy(pe_hbm.at[layer, pg], pe_buf.at[slot, pl.ds(j * page, page)],
                                      sems.at[slot, 1]))

    def start_all(bb, slot):
        for j in range(n_pages):
            c0, c1 = page_copies(bb, slot, j)
            c0.start()
            c1.start()

    @pl.when(b == 0)
    def _():
        start_all(0, 0)

    slot = b % 2

    @pl.when(b + 1 < nb)
    def _():
        start_all(b + 1, 1 - slot)

    for j in range(n_pages):
        c0, c1 = page_copies(b, slot, j)
        c0.wait()
        c1.wait()

    q = q_ref[0]
    rows = q.shape[0]
    lat = lat_buf[slot].astype(BF16)
    pe = pe_buf[slot].astype(BF16)
    contract_last = (((1,), (1,)), ((), ()))
    s = (lax.dot_general(q[:, :kv_rank], lat, contract_last, preferred_element_type=F32)
         + lax.dot_general(q[:, kv_rank:], pe, contract_last, preferred_element_type=F32))
    kn = knew_ref[0]
    sn = lax.dot_general(q, kn, contract_last, preferred_element_type=F32)
    t_row = lax.broadcasted_iota(I32, (rows, n_new), 0) % n_new
    t_col = lax.broadcasted_iota(I32, (rows, n_new), 1)
    sn = jnp.where(t_col <= t_row, sn, NEG_BIG)
    m = jnp.maximum(jnp.max(s, axis=-1, keepdims=True), jnp.max(sn, axis=-1, keepdims=True))
    p = jnp.exp(s - m)
    pn = jnp.exp(sn - m)
    l = jnp.sum(p, axis=-1, keepdims=True) + jnp.sum(pn, axis=-1, keepdims=True)
    o = (jnp.dot(p.astype(BF16), lat, preferred_element_type=F32)
         + jnp.dot(pn.astype(BF16), kn[:, :kv_rank], preferred_element_type=F32)) / l
    res = jnp.dot(o.astype(BF16), wuv_ref[...], preferred_element_type=F32)
    col_head = lax.broadcasted_iota(I32, (n_new, N_HEADS * V_HEAD_DIM), 1) // V_HEAD_DIM
    out = jnp.zeros((n_new, N_HEADS * V_HEAD_DIM), F32)
    for hd in range(N_HEADS):
        out = jnp.where(col_head == hd, res[hd * n_new:(hd + 1) * n_new], out)
    o_ref[0] = _rms(out, g_ref[...]).astype(BF16)


def _attn_sample(page_table, q_s, k_new, wuv_flat, g_attn, cache_lat, cache_pe, *, layer):
    bs, rows, qd = q_s.shape
    n_new = k_new.shape[1]
    n_pages = page_table.shape[1]
    page = cache_lat.shape[2]
    kv_rank = cache_lat.shape[3]
    past = n_pages * page
    d_out = N_HEADS * V_HEAD_DIM
    grid_spec = pltpu.PrefetchScalarGridSpec(
        num_scalar_prefetch=1,
        grid=(bs,),
        in_specs=[pl.BlockSpec((1, rows, qd), lambda b, pt: (b, 0, 0)),
                  pl.BlockSpec((1, n_new, qd), lambda b, pt: (b, 0, 0)),
                  pl.BlockSpec(wuv_flat.shape, lambda b, pt: (0, 0)),
                  pl.BlockSpec((1, d_out), lambda b, pt: (0, 0)),
                  pl.BlockSpec(memory_space=pl.ANY),
                  pl.BlockSpec(memory_space=pl.ANY)],
        out_specs=pl.BlockSpec((1, n_new, d_out), lambda b, pt: (b, 0, 0)),
        scratch_shapes=[pltpu.VMEM((2, past, kv_rank), F32),
                        pltpu.VMEM((2, past, QK_ROPE), F32),
                        pltpu.SemaphoreType.DMA((2, 2))],
    )
    body = functools.partial(_attn_sample_body, layer=layer, n_pages=n_pages, page=page,
                             kv_rank=kv_rank, n_new=n_new)
    return pl.pallas_call(
        body,
        grid_spec=grid_spec,
        out_shape=jax.ShapeDtypeStruct((bs, n_new, d_out), BF16),
        compiler_params=_cparams(1),
        name="attn_sample",
    )(page_table.reshape(-1), q_s, k_new, wuv_flat, g_attn, cache_lat, cache_pe)


def _outproj_body(x_ref, ssm_ref, att_ref, w_ref, gate_ref, g2_ref, sh_ref, sc_ref, wr_ref, br_ref,
                  xo_ref, h2_ref, ridx_ref, rw_ref, *, mod_bcast):
    half = ssm_ref.shape[1]
    merged = (jnp.dot(ssm_ref[...], w_ref[:half], preferred_element_type=F32)
              + jnp.dot(att_ref[...], w_ref[half:], preferred_element_type=F32))
    gate = gate_ref[0] if mod_bcast else gate_ref[...]
    sh = sh_ref[0] if mod_bcast else sh_ref[...]
    sc = sc_ref[0] if mod_bcast else sc_ref[...]
    xn = x_ref[...] + gate * merged
    xo_ref[...] = xn
    h2 = _rms(xn, g2_ref[...]) * (1.0 + sc) + sh
    h2_ref[...] = h2
    logits = jnp.dot(h2.astype(BF16), wr_ref[...], preferred_element_type=F32) + br_ref[...]
    tm = logits.shape[0]
    lane = lax.broadcasted_iota(I32, (tm, LANES), 1)
    lanef = lane.astype(F32)
    lg = jnp.where(lane < MOE_GROUPS, logits, NEG_BIG)
    mg = jnp.max(lg, axis=-1, keepdims=True)
    gsel = jnp.min(jnp.where(lg == mg, lanef, float(LANES)), axis=-1, keepdims=True)
    wg = 1.0 / jnp.sum(jnp.exp(lg - mg), axis=-1, keepdims=True)
    lo = float(MOE_GROUPS) + float(EXPERTS_PER_GROUP) * gsel
    le = jnp.where(lanef >= lo, jnp.where(lanef < lo + float(EXPERTS_PER_GROUP), logits, NEG_BIG), NEG_BIG)
    v1 = jnp.max(le, axis=-1, keepdims=True)
    i1 = jnp.min(jnp.where(le == v1, lanef, float(LANES)), axis=-1, keepdims=True)
    le2 = jnp.where(lanef == i1, NEG_BIG, le)
    v2 = jnp.max(le2, axis=-1, keepdims=True)
    i2 = jnp.min(jnp.where(le2 == v2, lanef, float(LANES)), axis=-1, keepdims=True)
    e2 = jnp.exp(v2 - v1)
    w1 = wg / (1.0 + e2)
    w2 = wg * e2 / (1.0 + e2)
    ex1 = (i1 - float(MOE_GROUPS)).astype(I32)
    ex2 = (i2 - float(MOE_GROUPS)).astype(I32)
    ridx_ref[...] = jnp.where(lane == 0, ex1, jnp.where(lane == 1, ex2, 0))
    rw_ref[...] = jnp.where(lane == 0, w1, jnp.where(lane == 1, w2, 0.0))


def _outproj(x, ssm_n, att_n, w_out, gate1, g2, shift2, scale2, w_route, b_route, *, tm, mod_bcast,
             rows_per_batch):
    r, d = x.shape
    half = ssm_n.shape[1]
    tpb = rows_per_batch // tm if mod_bcast else 1
    if mod_bcast:
        mod_spec = pl.BlockSpec((1, 1, d), lambda i: (i // tpb, 0, 0))
    else:
        mod_spec = pl.BlockSpec((tm, d), lambda i: (0, 0))
    body = functools.partial(_outproj_body, mod_bcast=mod_bcast)
    return pl.pallas_call(
        body,
        grid=(r // tm,),
        in_specs=[pl.BlockSpec((tm, d), lambda i: (i, 0)),
                  pl.BlockSpec((tm, half), lambda i: (i, 0)),
                  pl.BlockSpec((tm, half), lambda i: (i, 0)),
                  pl.BlockSpec(w_out.shape, lambda i: (0, 0)),
                  mod_spec,
                  pl.BlockSpec((1, d), lambda i: (0, 0)),
                  mod_spec, mod_spec,
                  pl.BlockSpec((d, LANES), lambda i: (0, 0)),
                  pl.BlockSpec((1, LANES), lambda i: (0, 0))],
        out_specs=[pl.BlockSpec((tm, d), lambda i: (i, 0)),
                   pl.BlockSpec((tm, d), lambda i: (i, 0)),
                   pl.BlockSpec((tm, LANES), lambda i: (i, 0)),
                   pl.BlockSpec((tm, LANES), lambda i: (i, 0))],
        out_shape=[jax.ShapeDtypeStruct((r, d), F32),
                   jax.ShapeDtypeStruct((r, d), F32),
                   jax.ShapeDtypeStruct((r, LANES), I32),
                   jax.ShapeDtypeStruct((r, LANES), F32)],
        compiler_params=_cparams(1),
        name="out_proj_router",
    )(x, ssm_n, att_n, w_out, gate1, g2, shift2, scale2, w_route, b_route)


def _route_meta(e_pairs, tile):
    npairs = e_pairs.shape[0]
    n_tiles = -(-npairs // tile) + N_EXPERTS
    oh = (e_pairs[:, None] == jnp.arange(N_EXPERTS, dtype=I32)[None, :]).astype(I32)
    cs = jnp.cumsum(oh, axis=0)
    rank = jnp.sum(cs * oh, axis=1) - 1
    counts = cs[-1]
    padded = ((counts + tile - 1) // tile) * tile
    ends = jnp.cumsum(padded)
    starts = ends - padded
    pos = jnp.sum(oh * starts[None, :], axis=1) + rank
    n_used = ends[-1] // tile
    tile_start = jnp.arange(n_tiles, dtype=I32) * tile
    tile_e = jnp.sum((tile_start[:, None] >= ends[None, :]).astype(I32), axis=1)
    last_e = jnp.sum((jnp.maximum(n_used - 1, 0) * tile >= ends).astype(I32))
    tile_e = jnp.minimum(tile_e, last_e).astype(I32)
    return pos.astype(I32), tile_e, n_used.astype(I32).reshape(1), n_tiles


def _dispatch_body(pos_ref, h_ref, xs_in_ref, xs_ref, sem, *, tm):
    del xs_in_ref
    base = pl.program_id(0) * tm

    def issue(r, c):
        for k in range(TOP_K):
            p = pos_ref[TOP_K * (base + r) + k]
            pltpu.make_async_copy(h_ref.at[pl.ds(r, 1)], xs_ref.at[pl.ds(p, 1)], sem).start()
        return c

    lax.fori_loop(0, tm, issue, 0)

    def drain(r, c):
        for k in range(TOP_K):
            pltpu.make_async_copy(h_ref.at[pl.ds(0, 1)], xs_ref.at[pl.ds(0, 1)], sem).wait()
        return c

    lax.fori_loop(0, tm, drain, 0)


def _dispatch(pos, h2, xs, *, tm):
    r, d = h2.shape
    grid_spec = pltpu.PrefetchScalarGridSpec(
        num_scalar_prefetch=1,
        grid=(r // tm,),
        in_specs=[pl.BlockSpec((tm, d), lambda i, p: (i, 0)),
                  pl.BlockSpec(memory_space=pl.ANY)],
        out_specs=pl.BlockSpec(memory_space=pl.ANY),
        scratch_shapes=[pltpu.SemaphoreType.DMA(())],
    )
    return pl.pallas_call(
        functools.partial(_dispatch_body, tm=tm),
        grid_spec=grid_spec,
        out_shape=jax.ShapeDtypeStruct(xs.shape, xs.dtype),
        input_output_aliases={2: 0},
        compiler_params=_cparams(1),
        name="moe_dispatch",
    )(pos, h2, xs)


def _moe_gemm_body(te_ref, nu_ref, x_ref, wg_ref, wu_ref, wd_ref, y_ref, wg_s, wu_s, wd_s):
    i = pl.program_id(0)
    e = te_ref[i]
    prev = te_ref[jnp.maximum(i - 1, 0)]

    @pl.when(jnp.logical_or(i == 0, e != prev))
    def _():
        wg_s[...] = wg_ref[0].astype(BF16)
        wu_s[...] = wu_ref[0].astype(BF16)
        wd_s[...] = wd_ref[0].astype(BF16)

    @pl.when(i < nu_ref[0])
    def _():
        xb = x_ref[...].astype(BF16)
        a = jnp.dot(xb, wg_s[...], preferred_element_type=F32)
        b = jnp.dot(xb, wu_s[...], preferred_element_type=F32)
        act = (a * jax.nn.sigmoid(a) * b).astype(BF16)
        y_ref[...] = jnp.dot(act, wd_s[...], preferred_element_type=F32)

    @pl.when(i >= nu_ref[0])
    def _():
        y_ref[...] = jnp.zeros_like(y_ref)


def _moe_gemm(tile_e, n_used, xs, w_g, w_u, w_d, *, layer, tile):
    rows, d = xs.shape
    f = w_g.shape[-1]
    ne = w_g.shape[1]
    n_tiles = rows // tile
    grid_spec = pltpu.PrefetchScalarGridSpec(
        num_scalar_prefetch=2,
        grid=(n_tiles,),
        in_specs=[pl.BlockSpec((tile, d), lambda i, te, nu: (i, 0)),
                  pl.BlockSpec((1, d, f), lambda i, te, nu: (layer * ne + te[i], 0, 0)),
                  pl.BlockSpec((1, d, f), lambda i, te, nu: (layer * ne + te[i], 0, 0)),
                  pl.BlockSpec((1, f, d), lambda i, te, nu: (layer * ne + te[i], 0, 0))],
        out_specs=pl.BlockSpec((tile, d), lambda i, te, nu: (i, 0)),
        scratch_shapes=[pltpu.VMEM((d, f), BF16), pltpu.VMEM((d, f), BF16), pltpu.VMEM((f, d), BF16)],
    )
    depth = w_g.shape[0]
    return pl.pallas_call(
        _moe_gemm_body,
        grid_spec=grid_spec,
        out_shape=jax.ShapeDtypeStruct((rows, d), F32),
        compiler_params=_cparams(1),
        name="moe_gemm",
    )(tile_e, n_used, xs, w_g.reshape(depth * ne, d, f), w_u.reshape(depth * ne, d, f),
      w_d.reshape(depth * ne, f, d))


def _combine_body(pos_ref, x_ref, rw_ref, gate_ref, y_hbm, o_ref, ybuf, sem, *, tm, mod_bcast):
    base = pl.program_id(0) * tm

    def issue(r, c):
        for k in range(TOP_K):
            p = pos_ref[TOP_K * (base + r) + k]
            pltpu.make_async_copy(y_hbm.at[pl.ds(p, 1)], ybuf.at[k, pl.ds(r, 1)], sem).start()
        return c

    lax.fori_loop(0, tm, issue, 0)

    def drain(r, c):
        for k in range(TOP_K):
            pltpu.make_async_copy(y_hbm.at[pl.ds(0, 1)], ybuf.at[k, pl.ds(0, 1)], sem).wait()
        return c

    lax.fori_loop(0, tm, drain, 0)
    gate = gate_ref[0] if mod_bcast else gate_ref[...]
    rw = rw_ref[...]
    moe = rw[:, 0:1] * ybuf[0] + rw[:, 1:2] * ybuf[1]
    o_ref[...] = x_ref[...] + gate * moe


def _combine(pos, x, rw, gate2, y, *, tm, mod_bcast, rows_per_batch):
    r, d = x.shape
    tpb = rows_per_batch // tm if mod_bcast else 1
    if mod_bcast:
        mod_spec = pl.BlockSpec((1, 1, d), lambda i, p: (i // tpb, 0, 0))
    else:
        mod_spec = pl.BlockSpec((tm, d), lambda i, p: (0, 0))
    grid_spec = pltpu.PrefetchScalarGridSpec(
        num_scalar_prefetch=1,
        grid=(r // tm,),
        in_specs=[pl.BlockSpec((tm, d), lambda i, p: (i, 0)),
                  pl.BlockSpec((tm, LANES), lambda i, p: (i, 0)),
                  mod_spec,
                  pl.BlockSpec(memory_space=pl.ANY)],
        out_specs=pl.BlockSpec((tm, d), lambda i, p: (i, 0)),
        scratch_shapes=[pltpu.VMEM((TOP_K, tm, d), F32), pltpu.SemaphoreType.DMA(())],
    )
    return pl.pallas_call(
        functools.partial(_combine_body, tm=tm, mod_bcast=mod_bcast),
        grid_spec=grid_spec,
        out_shape=jax.ShapeDtypeStruct((r, d), F32),
        compiler_params=_cparams(1),
        name="moe_combine",
    )(pos, x, rw, gate2, y)


def _final_norm_body(x_ref, g_ref, o_ref):
    o_ref[...] = _rms(x_ref[...], g_ref[...])


def _final_norm(x, g, *, tm):
    r, d = x.shape
    return pl.pallas_call(
        _final_norm_body,
        grid=(r // tm,),
        in_specs=[pl.BlockSpec((tm, d), lambda i: (i, 0)), pl.BlockSpec((1, d), lambda i: (0, 0))],
        out_specs=pl.BlockSpec((tm, d), lambda i: (i, 0)),
        out_shape=jax.ShapeDtypeStruct((r, d), F32),
        compiler_params=_cparams(1),
        name="final_norm",
    )(x, g)


def _rope_tables(pos):
    half = QK_ROPE // 2
    inv = ROPE_THETA ** (-jnp.arange(half, dtype=F32) * (2.0 / QK_ROPE))
    ang = pos.astype(F32)[:, None] * inv[None, :]
    c, s = jnp.cos(ang), jnp.sin(ang)
    reps = LANES // half
    cos4 = jnp.tile(c, (1, reps))
    sin4 = jnp.tile(jnp.concatenate([-s, s], axis=-1), (1, reps // 2))
    return cos4, sin4


def _pick_tile(n, pref):
    t = min(pref, n)
    while n % t:
        t //= 2
    return t


def kernel(x_prompt, x_sample, cache_kv_latent, cache_k_rope, state_ssm_re, state_ssm_im, page_table,
           c_prompt, c_sample, norm_mix_g, norm_ffn_g, w_ada, b_ada, w_in, ssm_lam_re, ssm_lam_im,
           ssm_log_dt, ssm_b_re, ssm_b_im, ssm_c_re, ssm_c_im, ssm_d, w_glu, b_glu, kv_norm_g, w_uk,
           w_uv, g_ssm_out, g_attn_out, w_out, w_route_group, b_route_group, w_route_expert,
           b_route_expert, w_exp_gate, w_exp_up, w_exp_down, final_norm_g):
    bp, tp, d = x_prompt.shape
    bs, ts_, _ = x_sample.shape
    depth = w_in.shape[0]
    kv_rank = w_uk.shape[1]
    ssm_w = d // 2
    n_groups = ssm_w // SSM_CH
    nst = n_groups * SSM_STATE
    n_pages, page = page_table.shape[1], cache_kv_latent.shape[2]
    past_len = n_pages * page
    rp = bp * tp
    rs = bs * ts_

    q_cols = w_in[..., ssm_w:ssm_w + N_HEADS * QK_DIM].reshape(depth, d, N_HEADS, QK_DIM)
    off = ssm_w + N_HEADS * QK_DIM
    w_in_p = jnp.concatenate([
        w_in[..., :ssm_w],
        q_cols[..., :QK_NOPE].reshape(depth, d, N_HEADS * QK_NOPE),
        q_cols[..., QK_NOPE:].reshape(depth, d, N_HEADS * QK_ROPE),
        w_in[..., off:off + kv_rank],
        w_in[..., off + kv_rank:], w_in[..., off + kv_rank:]], axis=-1).astype(BF16)
    wuk_t = jnp.transpose(w_uk, (0, 2, 3, 1)).astype(BF16)
    wuv_t = jnp.transpose(w_uv, (0, 2, 1, 3)).astype(BF16)
    wuv_flat = w_uv.reshape(depth, kv_rank, N_HEADS * V_HEAD_DIM).astype(BF16)
    w_glu_b = w_glu.astype(BF16)
    w_out_b = w_out.astype(BF16)
    n_route = MOE_GROUPS + N_EXPERTS
    w_route = jnp.concatenate([w_route_group, w_route_expert.reshape(depth, d, N_EXPERTS),
                               jnp.zeros((depth, d, LANES - n_route), F32)], axis=-1).astype(BF16)
    b_route = jnp.concatenate([b_route_group, b_route_expert.reshape(depth, N_EXPERTS),
                               jnp.zeros((depth, LANES - n_route), F32)], axis=-1).reshape(depth, 1, LANES)
    tabs_all = _ssm_tables(ssm_lam_re, ssm_lam_im, ssm_log_dt, ssm_b_re, ssm_b_im, ssm_c_re, ssm_c_im,
                           ssm_d, SSM_CHUNK)
    cos_p, sin_p = _rope_tables(jnp.arange(tp))
    cos_s, sin_s = _rope_tables(past_len + jnp.arange(ts_))
    cos_s = cos_s.reshape(ts_, 1, LANES)
    sin_s = sin_s.reshape(ts_, 1, LANES)

    pad = (-(bs + bp)) % 8
    c_all = jnp.concatenate([c_sample, c_prompt, jnp.zeros((pad, d), F32)], axis=0)
    mod = _ada(c_all, w_ada, b_ada)

    tm_p = _pick_tile(tp, 256)
    ts_ssm = _pick_tile(tp, 512)
    tq = _pick_tile(tp, 256)
    tk = _pick_tile(tp, 512)
    moe_tile = 256
    tm_final = _pick_tile(rp, 512)

    xp = x_prompt.reshape(rp, d)
    xs = jnp.transpose(x_sample, (1, 0, 2)).reshape(rs, d)
    h0re = state_ssm_re.reshape(depth, bs, nst)
    h0im = state_ssm_im.reshape(depth, bs, nst)

    lat_p, pe_p, sre_p, sim_p = [], [], [], []
    lat_s, pe_s, sre_s, sim_s = [], [], [], []
    for l in range(depth):
        mod_s = [mod[l, :bs, i * d:(i + 1) * d] for i in range(6)]
        mod_p = [mod[l, bs:bs + bp, i * d:(i + 1) * d].reshape(bp, 1, d) for i in range(6)]
        tabs = {k: (v if k == "tri" else v[l]) for k, v in tabs_all.items()}
        g_mix = norm_mix_g[l].reshape(1, d)
        g_ffn = norm_ffn_g[l].reshape(1, d)
        kvg = kv_norm_g[l].reshape(1, kv_rank)
        bglu = b_glu[l].reshape(1, ssm_w)
        gso = g_ssm_out[l].reshape(1, ssm_w)
        gao = g_attn_out[l].reshape(1, N_HEADS * V_HEAD_DIM)

        u_p, q_p, kcat_p, ckv_p, kpe_p = _inproj(
            xp, g_mix, mod_p[0], mod_p[1], w_in_p[l], wuk_t[l], kvg, cos_p, sin_p,
            tm=tm_p, mod_bcast=True, rope_bcast=False, rows_per_batch=tp)
        ssm_p, s_re, s_im = _ssm_prompt(u_p.reshape(bp, tp, ssm_w), tabs, w_glu_b[l], bglu, gso, ts=ts_ssm)
        att_p = _attn_prompt(q_p, kcat_p, wuv_t[l], gao, batch=bp, seq=tp, tq=tq, tk=tk)
        xp, h2_p, ridx_p, rw_p = _outproj(
            xp, ssm_p.reshape(rp, ssm_w), att_p, w_out_b[l], mod_p[2], g_ffn, mod_p[3], mod_p[4],
            w_route[l], b_route[l], tm=tm_p, mod_bcast=True, rows_per_batch=tp)
        lat_p.append(ckv_p.reshape(bp, tp, kv_rank))
        pe_p.append(kpe_p.reshape(bp, tp, QK_ROPE))
        sre_p.append(s_re.reshape(bp, n_groups, SSM_STATE))
        sim_p.append(s_im.reshape(bp, n_groups, SSM_STATE))

        u_s, q_s, kcat_s, ckv_s, kpe_s = _inproj(
            xs, g_mix, mod_s[0], mod_s[1], w_in_p[l], wuk_t[l], kvg, cos_s, sin_s,
            tm=bs, mod_bcast=False, rope_bcast=True, rows_per_batch=bs)
        ssm_s, s_re, s_im = _ssm_sample(u_s, tabs, h0re[l], h0im[l], w_glu_b[l], bglu, gso)
        q_sb = jnp.transpose(q_s.reshape(N_HEADS, ts_, bs, kv_rank + QK_ROPE), (2, 0, 1, 3))
        q_sb = q_sb.reshape(bs, N_HEADS * ts_, kv_rank + QK_ROPE)
        k_new = jnp.transpose(kcat_s.reshape(ts_, bs, kv_rank + QK_ROPE), (1, 0, 2))
        att_s = _attn_sample(page_table, q_sb, k_new, wuv_flat[l], gao, cache_kv_latent, cache_k_rope,
                             layer=l)
        att_s = jnp.transpose(att_s, (1, 0, 2)).reshape(rs, N_HEADS * V_HEAD_DIM)
        xs, h2_s, ridx_s, rw_s = _outproj(
            xs, ssm_s, att_s, w_out_b[l], mod_s[2], g_ffn, mod_s[3], mod_s[4],
            w_route[l], b_route[l], tm=bs, mod_bcast=False, rows_per_batch=bs)
        lat_s.append(jnp.transpose(ckv_s.reshape(ts_, bs, kv_rank), (1, 0, 2)))
        pe_s.append(jnp.transpose(kpe_s.reshape(ts_, bs, QK_ROPE), (1, 0, 2)))
        sre_s.append(s_re.reshape(bs, n_groups, SSM_STATE))
        sim_s.append(s_im.reshape(bs, n_groups, SSM_STATE))

        e_pairs = jnp.concatenate([ridx_p[:, :TOP_K].reshape(-1), ridx_s[:, :TOP_K].reshape(-1)])
        pos, tile_e, n_used, n_tiles = _route_meta(e_pairs, moe_tile)
        pos_p, pos_s = pos[:TOP_K * rp], pos[TOP_K * rp:]
        xsrt = jnp.zeros((n_tiles * moe_tile, d), F32)
        xsrt = _dispatch(pos_p, h2_p, xsrt, tm=tm_p)
        xsrt = _dispatch(pos_s, h2_s, xsrt, tm=bs)
        y = _moe_gemm(tile_e, n_used, xsrt, w_exp_gate, w_exp_up, w_exp_down, layer=l, tile=moe_tile)
        xp = _combine(pos_p, xp, rw_p, mod_p[5], y, tm=tm_p, mod_bcast=True, rows_per_batch=tp)
        xs = _combine(pos_s, xs, rw_s, mod_s[5], y, tm=bs, mod_bcast=False, rows_per_batch=bs)

    fg = final_norm_g.reshape(1, d)
    y_prompt = _final_norm(xp, fg, tm=tm_final).reshape(bp, tp, d)
    y_sample = jnp.transpose(_final_norm(xs, fg, tm=bs).reshape(ts_, bs, d), (1, 0, 2))
    return (y_prompt, y_sample,
            jnp.stack(lat_p), jnp.stack(pe_p), jnp.stack(sre_p), jnp.stack(sim_p),
            jnp.stack(lat_s), jnp.stack(pe_s), jnp.stack(sre_s), jnp.stack(sim_s))
```

```python
import functools
import math

import numpy as np
import jax
import jax.numpy as jnp
from jax import lax
from jax.experimental import pallas as pl
from jax.experimental.pallas import tpu as pltpu

F32 = jnp.float32
BF16 = jnp.bfloat16
I32 = jnp.int32

SSM_CH = 16
SSM_STATE = 64
N_HEADS = 8
QK_NOPE = 128
QK_ROPE = 64
QK_DIM = QK_NOPE + QK_ROPE
V_HEAD_DIM = 128
ROPE_THETA = 10000.0
EPS = 1e-6
SM_SCALE = QK_DIM ** -0.5
MOE_GROUPS = 4
EXPERTS_PER_GROUP = 4
N_EXPERTS = MOE_GROUPS * EXPERTS_PER_GROUP
TOP_K = 2

LANES = 128
MXU_DIM = 256
VMEM_LIMIT = 56 * 1024 * 1024

GROUPS_PER_BLOCK = MXU_DIM // SSM_CH
SSM_CHUNK = 64
NEG_BIG = -1e30


def _cparams(n_axes):
    return pltpu.CompilerParams(dimension_semantics=("arbitrary",) * n_axes,
                                vmem_limit_bytes=VMEM_LIMIT)


def _rms(x, g):
    return x * lax.rsqrt(jnp.mean(x * x, axis=-1, keepdims=True) + EPS) * g


def _ada_body(c_ref, w_ref, b_ref, o_ref):
    c = c_ref[...]
    s = (c * jax.nn.sigmoid(c)).astype(BF16)
    o_ref[0] = jnp.dot(s, w_ref[0].astype(BF16), preferred_element_type=F32) + b_ref[0]


def _ada(c_all, w_ada, b_ada):
    depth, d, n6 = w_ada.shape
    r = c_all.shape[0]
    tn = 1024
    return pl.pallas_call(
        _ada_body,
        grid=(depth, n6 // tn),
        in_specs=[pl.BlockSpec((r, d), lambda l, j: (0, 0)),
                  pl.BlockSpec((1, d, tn), lambda l, j: (l, 0, j)),
                  pl.BlockSpec((1, 1, tn), lambda l, j: (l, 0, j))],
        out_specs=pl.BlockSpec((1, r, tn), lambda l, j: (l, 0, j)),
        out_shape=jax.ShapeDtypeStruct((depth, r, n6), F32),
        compiler_params=_cparams(2),
        name="ada_ln",
    )(c_all, w_ada, b_ada.reshape(depth, 1, n6))


def _rope_pairs(p, cos4, sin4, first_half):
    swapped = jnp.where(first_half, pltpu.roll(p, 96, 1), pltpu.roll(p, 32, 1))
    return p * cos4 + swapped * sin4


def _inproj_body(x_ref, g_ref, sh_ref, sc_ref, w_ref, wuk_ref, kvg_ref, cos_ref, sin_ref,
                 u_ref, q_ref, kcat_ref, ckv_ref, kpe_ref, *, mod_bcast, rope_bcast, ssm_w, kv_rank):
    x = x_ref[...]
    tm = x.shape[0]
    sh = sh_ref[0] if mod_bcast else sh_ref[...]
    sc = sc_ref[0] if mod_bcast else sc_ref[...]
    h = _rms(x, g_ref[...]) * (1.0 + sc) + sh
    z = jnp.dot(h.astype(BF16), w_ref[...], preferred_element_type=F32)
    u_ref[...] = z[:, :ssm_w]
    cos4 = cos_ref[0] if rope_bcast else cos_ref[...]
    sin4 = sin_ref[0] if rope_bcast else sin_ref[...]
    lane = lax.broadcasted_iota(I32, (tm, LANES), 1)
    first_half = (lane % QK_ROPE) < (QK_ROPE // 2)
    off_nope = ssm_w
    off_rope = off_nope + N_HEADS * QK_NOPE
    off_kv = off_rope + N_HEADS * QK_ROPE
    off_kr = off_kv + kv_rank
    for hp in range(N_HEADS // 2):
        pr = z[:, off_rope + hp * LANES: off_rope + (hp + 1) * LANES]
        rp = (_rope_pairs(pr, cos4, sin4, first_half) * SM_SCALE).astype(BF16)
        q_ref[2 * hp, :, kv_rank:kv_rank + QK_ROPE] = rp[:, :QK_ROPE]
        q_ref[2 * hp + 1, :, kv_rank:kv_rank + QK_ROPE] = rp[:, QK_ROPE:]
    for hd in range(N_HEADS):
        nope = z[:, off_nope + hd * QK_NOPE: off_nope + (hd + 1) * QK_NOPE].astype(BF16)
        ql = jnp.dot(nope, wuk_ref[hd], preferred_element_type=F32) * SM_SCALE
        q_ref[hd, :, :kv_rank] = ql.astype(BF16)
    ckv = _rms(z[:, off_kv:off_kr], kvg_ref[...])
    ckv_ref[...] = ckv
    kk = _rope_pairs(z[:, off_kr:off_kr + LANES], cos4, sin4, first_half)
    kpe_ref[...] = kk[:, :QK_ROPE]
    kcat_ref[:, :kv_rank] = ckv.astype(BF16)
    kcat_ref[:, kv_rank:kv_rank + QK_ROPE] = kk[:, :QK_ROPE].astype(BF16)


def _inproj(x, g, shift, scale, w, wuk_t, kvg, cos4, sin4, *, tm, mod_bcast, rope_bcast, rows_per_batch):
    r, d = x.shape
    ncols = w.shape[1]
    kv_rank = wuk_t.shape[2]
    ssm_w = d // 2
    qd = kv_rank + QK_ROPE
    tpb = rows_per_batch // tm if mod_bcast else 1
    if mod_bcast:
        mod_spec = pl.BlockSpec((1, 1, d), lambda i: (i // tpb, 0, 0))
    else:
        mod_spec = pl.BlockSpec((tm, d), lambda i: (0, 0))
    if rope_bcast:
        rope_spec = pl.BlockSpec((1, 1, LANES), lambda i: (i, 0, 0))
    else:
        rope_spec = pl.BlockSpec((tm, LANES), lambda i: (i % tpb, 0))
    body = functools.partial(_inproj_body, mod_bcast=mod_bcast, rope_bcast=rope_bcast,
                             ssm_w=ssm_w, kv_rank=kv_rank)
    return pl.pallas_call(
        body,
        grid=(r // tm,),
        in_specs=[pl.BlockSpec((tm, d), lambda i: (i, 0)),
                  pl.BlockSpec((1, d), lambda i: (0, 0)),
                  mod_spec, mod_spec,
                  pl.BlockSpec((d, ncols), lambda i: (0, 0)),
                  pl.BlockSpec(wuk_t.shape, lambda i: (0, 0, 0)),
                  pl.BlockSpec((1, kv_rank), lambda i: (0, 0)),
                  rope_spec, rope_spec],
        out_specs=[pl.BlockSpec((tm, ssm_w), lambda i: (i, 0)),
                   pl.BlockSpec((N_HEADS, tm, qd), lambda i: (0, i, 0)),
                   pl.BlockSpec((tm, qd), lambda i: (i, 0)),
                   pl.BlockSpec((tm, kv_rank), lambda i: (i, 0)),
                   pl.BlockSpec((tm, QK_ROPE), lambda i: (i, 0))],
        out_shape=[jax.ShapeDtypeStruct((r, ssm_w), F32),
                   jax.ShapeDtypeStruct((N_HEADS, r, qd), BF16),
                   jax.ShapeDtypeStruct((r, qd), BF16),
                   jax.ShapeDtypeStruct((r, kv_rank), F32),
                   jax.ShapeDtypeStruct((r, QK_ROPE), F32)],
        compiler_params=_cparams(1),
        name="in_proj_mla_prep",
    )(x, g, shift, scale, w, wuk_t, kvg, cos4, sin4)


def _glu_norm(y, wglu_ref, bglu_ref, gout_ref):
    g = jax.nn.gelu(y)
    gate = jnp.dot(g.astype(BF16), wglu_ref[...], preferred_element_type=F32) + bglu_ref[...]
    return _rms(g * jax.nn.sigmoid(gate), gout_ref[...])


def _ssm_prompt_body(u_ref, bbd_ref, cbd_ref, tri_ref, pre_re_ref, pre_im_ref, post_re_ref, post_im_ref,
                     a_re_ref, a_im_ref, d_ref, wglu_ref, bglu_ref, gout_ref,
                     y_ref, sre_ref, sim_ref, bu_scr, h_scr, hprev_scr, yacc_scr, *, n_blocks, chunk):
    tc = pl.program_id(1)
    ts = u_ref.shape[1]
    sl = GROUPS_PER_BLOCK * SSM_STATE

    @pl.when(tc == 0)
    def _():
        hprev_scr[...] = jnp.zeros_like(hprev_scr)

    u = u_ref[0]
    ub = u.astype(BF16)
    tri = tri_ref[...]
    for j in range(n_blocks):
        cols = slice(j * MXU_DIM, (j + 1) * MXU_DIM)
        lanes = slice(j * sl, (j + 1) * sl)
        bu_scr[...] = jnp.dot(ub[:, cols], bbd_ref[j], preferred_element_type=F32)
        a_re = a_re_ref[:, lanes]
        a_im = a_im_ref[:, lanes]

        def sub(s, carry):
            hr, hi = carry
            r0 = pl.multiple_of(s * chunk, chunk)
            b_re = bu_scr[pl.ds(r0, chunk), 0:sl]
            b_im = bu_scr[pl.ds(r0, chunk), sl:2 * sl]
            p_re = pre_re_ref[:, lanes]
            p_im = pre_im_ref[:, lanes]
            x_re = (p_re * b_re - p_im * b_im).astype(BF16)
            x_im = (p_re * b_im + p_im * b_re).astype(BF16)
            z_re = jnp.dot(tri, x_re, preferred_element_type=F32) + (a_re * hr - a_im * hi)
            z_im = jnp.dot(tri, x_im, preferred_element_type=F32) + (a_re * hi + a_im * hr)
            q_re = post_re_ref[:, lanes]
            q_im = post_im_ref[:, lanes]
            h_re = q_re * z_re - q_im * z_im
            h_im = q_re * z_im + q_im * z_re
            h_scr[pl.ds(r0, chunk), 0:sl] = h_re.astype(BF16)
            h_scr[pl.ds(r0, chunk), sl:2 * sl] = h_im.astype(BF16)
            return h_re[chunk - 1:chunk], h_im[chunk - 1:chunk]

        hr, hi = lax.fori_loop(0, ts // chunk, sub,
                               (hprev_scr[2 * j:2 * j + 1], hprev_scr[2 * j + 1:2 * j + 2]))
        hprev_scr[2 * j:2 * j + 1] = hr
        hprev_scr[2 * j + 1:2 * j + 2] = hi
        sre_ref[0, :, lanes] = hr
        sim_ref[0, :, lanes] = hi
        yacc_scr[:, cols] = (jnp.dot(h_scr[...], cbd_ref[j], preferred_element_type=F32)
                             + d_ref[:, cols] * u[:, cols])
    y_ref[0] = _glu_norm(yacc_scr[...], wglu_ref, bglu_ref, gout_ref).astype(BF16)


def _ssm_prompt(u, tabs, wglu, bglu, gout, *, ts):
    b, t, w = u.shape
    n_blocks = w // MXU_DIM
    sl2 = 2 * GROUPS_PER_BLOCK * SSM_STATE
    nst = (w // SSM_CH) * SSM_STATE
    chunk = SSM_CHUNK
    full2 = lambda shape: pl.BlockSpec(shape, lambda i, j: (0,) * len(shape))
    body = functools.partial(_ssm_prompt_body, n_blocks=n_blocks, chunk=chunk)
    return pl.pallas_call(
        body,
        grid=(b, t // ts),
        in_specs=[pl.BlockSpec((1, ts, w), lambda i, j: (i, j, 0)),
                  full2((n_blocks, MXU_DIM, sl2)), full2((n_blocks, sl2, MXU_DIM)),
                  full2((chunk, chunk)),
                  full2((chunk, nst)), full2((chunk, nst)), full2((chunk, nst)), full2((chunk, nst)),
                  full2((1, nst)), full2((1, nst)), full2((1, w)),
                  full2((w, w)), full2((1, w)), full2((1, w))],
        out_specs=[pl.BlockSpec((1, ts, w), lambda i, j: (i, j, 0)),
                   pl.BlockSpec((1, 1, nst), lambda i, j: (i, 0, 0)),
                   pl.BlockSpec((1, 1, nst), lambda i, j: (i, 0, 0))],
        out_shape=[jax.ShapeDtypeStruct((b, t, w), BF16),
                   jax.ShapeDtypeStruct((b, 1, nst), F32),
                   jax.ShapeDtypeStruct((b, 1, nst), F32)],
        scratch_shapes=[pltpu.VMEM((ts, sl2), F32), pltpu.VMEM((ts, sl2), BF16),
                        pltpu.VMEM((2 * n_blocks, sl2 // 2), F32), pltpu.VMEM((ts, w), F32)],
        compiler_params=_cparams(2),
        name="ssm_prompt",
    )(u, tabs["bbd"], tabs["cbd"], tabs["tri"], tabs["pre_re"], tabs["pre_im"], tabs["post_re"],
      tabs["post_im"], tabs["a_re"], tabs["a_im"], tabs["d"], wglu, bglu, gout)


def _ssm_sample_body(u_ref, bbd_ref, cbd_ref, a_re_ref, a_im_ref, d_ref, h0re_ref, h0im_ref,
                     wglu_ref, bglu_ref, gout_ref, y_ref, sre_ref, sim_ref,
                     bu_scr, h_scr, yacc_scr, *, n_blocks, n_steps):
    sl = GROUPS_PER_BLOCK * SSM_STATE
    bs = h0re_ref.shape[0]
    u = u_ref[...]
    ub = u.astype(BF16)
    for j in range(n_blocks):
        cols = slice(j * MXU_DIM, (j + 1) * MXU_DIM)
        lanes = slice(j * sl, (j + 1) * sl)
        bu_scr[...] = jnp.dot(ub[:, cols], bbd_ref[j], preferred_element_type=F32)
        a_re = a_re_ref[:, lanes]
        a_im = a_im_ref[:, lanes]
        hr = h0re_ref[:, lanes]
        hi = h0im_ref[:, lanes]
        for t in range(n_steps):
            rows = slice(t * bs, (t + 1) * bs)
            nr = (a_re * hr - a_im * hi) + bu_scr[rows, 0:sl]
            ni = (a_re * hi + a_im * hr) + bu_scr[rows, sl:2 * sl]
            hr, hi = nr, ni
            h_scr[rows, 0:sl] = hr.astype(BF16)
            h_scr[rows, sl:2 * sl] = hi.astype(BF16)
        sre_ref[:, lanes] = hr
        sim_ref[:, lanes] = hi
        yacc_scr[:, cols] = (jnp.dot(h_scr[...], cbd_ref[j], preferred_element_type=F32)
                             + d_ref[:, cols] * u[:, cols])
    y_ref[...] = _glu_norm(yacc_scr[...], wglu_ref, bglu_ref, gout_ref).astype(BF16)


def _ssm_sample(u, tabs, h0re, h0im, wglu, bglu, gout):
    r, w = u.shape
    bs, nst = h0re.shape
    n_blocks = w // MXU_DIM
    sl2 = 2 * GROUPS_PER_BLOCK * SSM_STATE
    body = functools.partial(_ssm_sample_body, n_blocks=n_blocks, n_steps=r // bs)
    return pl.pallas_call(
        body,
        out_shape=[jax.ShapeDtypeStruct((r, w), BF16),
                   jax.ShapeDtypeStruct((bs, nst), F32),
                   jax.ShapeDtypeStruct((bs, nst), F32)],
        scratch_shapes=[pltpu.VMEM((r, sl2), F32), pltpu.VMEM((r, sl2), BF16), pltpu.VMEM((r, w), F32)],
        compiler_params=pltpu.CompilerParams(vmem_limit_bytes=VMEM_LIMIT),
        name="ssm_sample",
    )(u, tabs["bbd"], tabs["cbd"], tabs["a_re"], tabs["a_im"], tabs["d"], h0re, h0im, wglu, bglu, gout)


def _ssm_tables(lam_re, lam_im, log_dt, b_re, b_im, c_re, c_im, d_skip, chunk):
    depth, g, n = lam_re.shape
    p = b_re.shape[-1]
    gb = GROUPS_PER_BLOCK
    nb = g // gb
    lr = lam_re.astype(F32)
    li = lam_im.astype(F32)
    dt = jnp.exp(log_dt.astype(F32))[..., None]
    mag = jnp.exp(lr * dt)
    ab_re = mag * jnp.cos(li * dt)
    ab_im = mag * jnp.sin(li * dt)
    den = lr * lr + li * li
    f_re = ((ab_re - 1.0) * lr + ab_im * li) / den
    f_im = (ab_im * lr - (ab_re - 1.0) * li) / den
    bb_re = f_re[..., None] * b_re - f_im[..., None] * b_im
    bb_im = f_re[..., None] * b_im + f_im[..., None] * b_re
    ks = np.arange(chunk)
    pw_re = jnp.ones((chunk, depth, g, n), F32)
    pw_im = jnp.zeros((chunk, depth, g, n), F32)
    sq_re, sq_im = ab_re, ab_im
    for bit in range(max(1, int(chunk - 1).bit_length())):
        sel = jnp.asarray(((ks >> bit) & 1).astype(np.float32))[:, None, None, None]
        m_re = sel * sq_re + (1.0 - sel)
        m_im = sel * sq_im
        pw_re, pw_im = pw_re * m_re - pw_im * m_im, pw_re * m_im + pw_im * m_re
        sq_re, sq_im = sq_re * sq_re - sq_im * sq_im, 2.0 * sq_re * sq_im
    nrm = pw_re * pw_re + pw_im * pw_im
    inv_re = pw_re / nrm
    inv_im = -pw_im / nrm
    flat = lambda a: jnp.moveaxis(a, 0, 1).reshape(depth, chunk, g * n)
    eye = jnp.eye(gb, dtype=F32)

    def bdiag_in(bb):
        bbj = bb.reshape(depth, nb, gb, n, p)
        return jnp.einsum("gh,djgnp->djgphn", eye, bbj).reshape(depth, nb, gb * p, gb * n)

    def bdiag_out(cc):
        ccj = cc.reshape(depth, nb, gb, p, n)
        return jnp.einsum("gh,djgpn->djgnhp", eye, ccj).reshape(depth, nb, gb * n, gb * p)

    bbd = jnp.concatenate([bdiag_in(bb_re), bdiag_in(bb_im)], axis=-1).astype(BF16)
    cbd = jnp.concatenate([bdiag_out(c_re.astype(F32)), bdiag_out(-c_im.astype(F32))], axis=-2).astype(BF16)
    tri = jnp.asarray(np.tril(np.ones((chunk, chunk), np.float32))).astype(BF16)
    return {
        "bbd": bbd, "cbd": cbd, "tri": tri,
        "pre_re": flat(inv_re), "pre_im": flat(inv_im),
        "post_re": flat(pw_re), "post_im": flat(pw_im),
        "a_re": ab_re.reshape(depth, 1, g * n), "a_im": ab_im.reshape(depth, 1, g * n),
        "d": d_skip.astype(F32).reshape(depth, 1, g * p),
    }


def _attn_prompt_body(qi_ref, ki_ref, last_ref, k_ref, qt_ref, vt_ref, wuvt_ref, g_ref, o_ref,
                      m_scr, l_scr, acc_scr, *, tq, tk, hg):
    pidx = pl.program_id(1)
    qi = qi_ref[pidx]
    ki = ki_ref[pidx]

    @pl.when(ki == 0)
    def _():
        m_scr[...] = jnp.full_like(m_scr, NEG_BIG)
        l_scr[...] = jnp.zeros_like(l_scr)
        acc_scr[...] = jnp.zeros_like(acc_scr)

    k = k_ref[...]
    vt = vt_ref[...]
    kpos = ki * tk + lax.broadcasted_iota(I32, (tk, hg * tq), 0)
    qpos = qi * tq + (lax.broadcasted_iota(I32, (tk, hg * tq), 1) & (tq - 1))
    visible = kpos <= qpos
    for g0 in range(0, N_HEADS, hg):
        cols = slice(g0 * tq, (g0 + hg) * tq)
        qt = jnp.concatenate([qt_ref[hd] for hd in range(g0, g0 + hg)], axis=1)
        st = jnp.dot(k, qt, preferred_element_type=F32)
        st = jnp.where(visible, st, NEG_BIG)
        m_prev = m_scr[:, cols]
        m_new = jnp.maximum(m_prev, jnp.max(st, axis=0, keepdims=True))
        alpha = jnp.exp(m_prev - m_new)
        p = jnp.exp(st - m_new)
        l_scr[:, cols] = alpha * l_scr[:, cols] + jnp.sum(p, axis=0, keepdims=True)
        acc_scr[:, cols] = alpha * acc_scr[:, cols] + jnp.dot(vt, p.astype(BF16),
                                                              preferred_element_type=F32)
        m_scr[:, cols] = m_new

    @pl.when(ki == last_ref[pidx])
    def _():
        outs = []
        for hd in range(N_HEADS):
            cols = slice(hd * tq, (hd + 1) * tq)
            ot = (acc_scr[:, cols] / l_scr[:, cols]).astype(BF16)
            outs.append(jnp.dot(wuvt_ref[hd], ot, preferred_element_type=F32))
        att = jnp.concatenate(outs, axis=0)
        ms = jnp.mean(att * att, axis=0, keepdims=True)
        o_ref[...] = (att * lax.rsqrt(ms + EPS) * g_ref[...]).astype(BF16)


def _attn_prompt(k_cat, q_t, v_t, wuv_tt, g_col, *, batch, seq, tq, tk, heads_per_group):
    qd = k_cat.shape[1]
    kv_rank = v_t.shape[0]
    nq = seq // tq
    qi_l, ki_l, last_l = [], [], []
    for qi in range(nq):
        last = (qi * tq + tq - 1) // tk
        for ki in range(last + 1):
            qi_l.append(qi)
            ki_l.append(ki)
            last_l.append(last)
    npairs = len(qi_l)
    qi_a = jnp.asarray(np.array(qi_l, np.int32))
    ki_a = jnp.asarray(np.array(ki_l, np.int32))
    last_a = jnp.asarray(np.array(last_l, np.int32))
    nqb = seq // tq
    nkb = seq // tk
    d_out = N_HEADS * V_HEAD_DIM
    grid_spec = pltpu.PrefetchScalarGridSpec(
        num_scalar_prefetch=3,
        grid=(batch, npairs),
        in_specs=[pl.BlockSpec((tk, qd), lambda b, p, qi, ki, la: (b * nkb + ki[p], 0)),
                  pl.BlockSpec((N_HEADS, qd, tq), lambda b, p, qi, ki, la: (0, 0, b * nqb + qi[p])),
                  pl.BlockSpec((kv_rank, tk), lambda b, p, qi, ki, la: (0, b * nkb + ki[p])),
                  pl.BlockSpec(wuv_tt.shape, lambda b, p, qi, ki, la: (0, 0, 0)),
                  pl.BlockSpec((d_out, 1), lambda b, p, qi, ki, la: (0, 0))],
        out_specs=pl.BlockSpec((d_out, tq), lambda b, p, qi, ki, la: (0, b * nqb + qi[p])),
        scratch_shapes=[pltpu.VMEM((1, N_HEADS * tq), F32), pltpu.VMEM((1, N_HEADS * tq), F32),
                        pltpu.VMEM((kv_rank, N_HEADS * tq), F32)],
    )
    body = functools.partial(_attn_prompt_body, tq=tq, tk=tk, hg=heads_per_group)
    return pl.pallas_call(
        body,
        grid_spec=grid_spec,
        out_shape=jax.ShapeDtypeStruct((d_out, batch * seq), BF16),
        compiler_params=_cparams(2),
        name="attn_prompt",
    )(qi_a, ki_a, last_a, k_cat, q_t, v_t, wuv_tt, g_col)


def _attn_sample_body(pt_ref, q_ref, knew_ref, wuv_ref, g_ref, lat_hbm, pet_hbm, o_ref,
                      lat_buf, pet_buf, sems, *, layer, n_pages, page, kv_rank, n_new, n_chunks):
    b = pl.program_id(0)
    nb = pl.num_programs(0)

    def page_copies(bb, slot, j):
        pg = pt_ref[bb * n_pages + j]
        return (pltpu.make_async_copy(lat_hbm.at[layer, pg], lat_buf.at[slot, pl.ds(j * page, page)],
                                      sems.at[slot, 0]),
                pltpu.make_async_copy(pet_hbm.at[layer, pg], pet_buf.at[slot, :, pl.ds(j * page, page)],
                                      sems.at[slot, 1]))

    def start_all(bb, slot):
        for j in range(n_pages):
            c0, c1 = page_copies(bb, slot, j)
            c0.start()
            c1.start()

    @pl.when(b == 0)
    def _():
        start_all(0, 0)

    slot = b % 2

    @pl.when(b + 1 < nb)
    def _():
        start_all(b + 1, 1 - slot)

    for j in range(n_pages):
        c0, c1 = page_copies(b, slot, j)
        c0.wait()
        c1.wait()

    q = q_ref[0]
    rows = q.shape[0]
    ql = q[:, :kv_rank]
    qp = q[:, kv_rank:]
    contract_last = (((1,), (1,)), ((), ()))
    ck = (n_pages * page) // n_chunks
    ms, ls, os_ = [], [], []
    for c in range(n_chunks):
        lat = lat_buf[slot, pl.ds(c * ck, ck), :].astype(BF16)
        pet = pet_buf[slot, :, pl.ds(c * ck, ck)].astype(BF16)
        s = (lax.dot_general(ql, lat, contract_last, preferred_element_type=F32)
             + jnp.dot(qp, pet, preferred_element_type=F32))
        m = jnp.max(s, axis=-1, keepdims=True)
        p = jnp.exp(s - m)
        ms.append(m)
        ls.append(jnp.sum(p, axis=-1, keepdims=True))
        os_.append(jnp.dot(p.astype(BF16), lat, preferred_element_type=F32))
    kn = knew_ref[0]
    sn = lax.dot_general(q, kn, contract_last, preferred_element_type=F32)
    t_row = lax.broadcasted_iota(I32, (rows, n_new), 0) % n_new
    t_col = lax.broadcasted_iota(I32, (rows, n_new), 1)
    sn = jnp.where(t_col <= t_row, sn, NEG_BIG)
    mn = jnp.max(sn, axis=-1, keepdims=True)
    pn = jnp.exp(sn - mn)
    ms.append(mn)
    ls.append(jnp.sum(pn, axis=-1, keepdims=True))
    os_.append(jnp.dot(pn.astype(BF16), kn[:, :kv_rank], preferred_element_type=F32))
    m_all = functools.reduce(jnp.maximum, ms)
    l = jnp.zeros_like(m_all)
    o = jnp.zeros_like(os_[0])
    for mi, li, oi in zip(ms, ls, os_):
        w = jnp.exp(mi - m_all)
        l = l + w * li
        o = o + w * oi
    o = o / l
    res = jnp.dot(o.astype(BF16), wuv_ref[...], preferred_element_type=F32)
    col_head = lax.broadcasted_iota(I32, (n_new, N_HEADS * V_HEAD_DIM), 1) // V_HEAD_DIM
    out = jnp.zeros((n_new, N_HEADS * V_HEAD_DIM), F32)
    for hd in range(N_HEADS):
        out = jnp.where(col_head == hd, res[hd * n_new:(hd + 1) * n_new], out)
    o_ref[0] = _rms(out, g_ref[...]).astype(BF16)


def _attn_sample(page_table, q_s, k_new, wuv_flat, g_attn, cache_lat, cache_pe_t, *, layer, n_chunks):
    bs, rows, qd = q_s.shape
    n_new = k_new.shape[1]
    n_pages = page_table.shape[1]
    page = cache_lat.shape[2]
    kv_rank = cache_lat.shape[3]
    past = n_pages * page
    d_out = N_HEADS * V_HEAD_DIM
    grid_spec = pltpu.PrefetchScalarGridSpec(
        num_scalar_prefetch=1,
        grid=(bs,),
        in_specs=[pl.BlockSpec((1, rows, qd), lambda b, pt: (b, 0, 0)),
                  pl.BlockSpec((1, n_new, qd), lambda b, pt: (b, 0, 0)),
                  pl.BlockSpec(wuv_flat.shape, lambda b, pt: (0, 0)),
                  pl.BlockSpec((1, d_out), lambda b, pt: (0, 0)),
                  pl.BlockSpec(memory_space=pl.ANY),
                  pl.BlockSpec(memory_space=pl.ANY)],
        out_specs=pl.BlockSpec((1, n_new, d_out), lambda b, pt: (b, 0, 0)),
        scratch_shapes=[pltpu.VMEM((2, past, kv_rank), F32),
                        pltpu.VMEM((2, QK_ROPE, past), F32),
                        pltpu.SemaphoreType.DMA((2, 2))],
    )
    body = functools.partial(_attn_sample_body, layer=layer, n_pages=n_pages, page=page,
                             kv_rank=kv_rank, n_new=n_new, n_chunks=n_chunks)
    return pl.pallas_call(
        body,
        grid_spec=grid_spec,
        out_shape=jax.ShapeDtypeStruct((bs, n_new, d_out), BF16),
        compiler_params=_cparams(1),
        name="attn_sample",
    )(page_table.reshape(-1), q_s, k_new, wuv_flat, g_attn, cache_lat, cache_pe_t)


def _outproj_body(x_ref, ssm_ref, att_ref, w_ref, gate_ref, g2_ref, sh_ref, sc_ref, wr_ref, br_ref,
                  xo_ref, h2_ref, ridx_ref, rw_ref, *, mod_bcast):
    half = ssm_ref.shape[1]
    merged = (jnp.dot(ssm_ref[...], w_ref[:half], preferred_element_type=F32)
              + jnp.dot(att_ref[...], w_ref[half:], preferred_element_type=F32))
    gate = gate_ref[0] if mod_bcast else gate_ref[...]
    sh = sh_ref[0] if mod_bcast else sh_ref[...]
    sc = sc_ref[0] if mod_bcast else sc_ref[...]
    xn = x_ref[...] + gate * merged
    xo_ref[...] = xn
    h2 = _rms(xn, g2_ref[...]) * (1.0 + sc) + sh
    h2_ref[...] = h2
    logits = jnp.dot(h2.astype(BF16), wr_ref[...], preferred_element_type=F32) + br_ref[...]
    tm = logits.shape[0]
    lane = lax.broadcasted_iota(I32, (tm, LANES), 1)
    lanef = lane.astype(F32)
    lg = jnp.where(lane < MOE_GROUPS, logits, NEG_BIG)
    mg = jnp.max(lg, axis=-1, keepdims=True)
    gsel = jnp.min(jnp.where(lg == mg, lanef, float(LANES)), axis=-1, keepdims=True)
    wg = 1.0 / jnp.sum(jnp.exp(lg - mg), axis=-1, keepdims=True)
    lo = float(MOE_GROUPS) + float(EXPERTS_PER_GROUP) * gsel
    le = jnp.where(lanef >= lo, jnp.where(lanef < lo + float(EXPERTS_PER_GROUP), logits, NEG_BIG), NEG_BIG)
    v1 = jnp.max(le, axis=-1, keepdims=True)
    i1 = jnp.min(jnp.where(le == v1, lanef, float(LANES)), axis=-1, keepdims=True)
    le2 = jnp.where(lanef == i1, NEG_BIG, le)
    v2 = jnp.max(le2, axis=-1, keepdims=True)
    i2 = jnp.min(jnp.where(le2 == v2, lanef, float(LANES)), axis=-1, keepdims=True)
    e2 = jnp.exp(v2 - v1)
    w1 = wg / (1.0 + e2)
    w2 = wg * e2 / (1.0 + e2)
    ex1 = (i1 - float(MOE_GROUPS)).astype(I32)
    ex2 = (i2 - float(MOE_GROUPS)).astype(I32)
    ridx_ref[...] = jnp.where(lane == 0, ex1, jnp.where(lane == 1, ex2, 0))
    rw_ref[...] = jnp.where(lane == 0, w1, jnp.where(lane == 1, w2, 0.0))


def _outproj(x, ssm_n, att_n, w_out, gate1, g2, shift2, scale2, w_route, b_route, *, tm, mod_bcast,
             rows_per_batch):
    r, d = x.shape
    half = ssm_n.shape[1]
    tpb = rows_per_batch // tm if mod_bcast else 1
    if mod_bcast:
        mod_spec = pl.BlockSpec((1, 1, d), lambda i: (i // tpb, 0, 0))
    else:
        mod_spec = pl.BlockSpec((tm, d), lambda i: (0, 0))
    body = functools.partial(_outproj_body, mod_bcast=mod_bcast)
    return pl.pallas_call(
        body,
        grid=(r // tm,),
        in_specs=[pl.BlockSpec((tm, d), lambda i: (i, 0)),
                  pl.BlockSpec((tm, half), lambda i: (i, 0)),
                  pl.BlockSpec((tm, half), lambda i: (i, 0)),
                  pl.BlockSpec(w_out.shape, lambda i: (0, 0)),
                  mod_spec,
                  pl.BlockSpec((1, d), lambda i: (0, 0)),
                  mod_spec, mod_spec,
                  pl.BlockSpec((d, LANES), lambda i: (0, 0)),
                  pl.BlockSpec((1, LANES), lambda i: (0, 0))],
        out_specs=[pl.BlockSpec((tm, d), lambda i: (i, 0)),
                   pl.BlockSpec((tm, d), lambda i: (i, 0)),
                   pl.BlockSpec((tm, LANES), lambda i: (i, 0)),
                   pl.BlockSpec((tm, LANES), lambda i: (i, 0))],
        out_shape=[jax.ShapeDtypeStruct((r, d), F32),
                   jax.ShapeDtypeStruct((r, d), F32),
                   jax.ShapeDtypeStruct((r, LANES), I32),
                   jax.ShapeDtypeStruct((r, LANES), F32)],
        compiler_params=_cparams(1),
        name="out_proj_router",
    )(x, ssm_n, att_n, w_out, gate1, g2, shift2, scale2, w_route, b_route)


def _route_meta(e_pairs, tile):
    npairs = e_pairs.shape[0]
    n_tiles = -(-npairs // tile) + N_EXPERTS
    oh = (e_pairs[:, None] == jnp.arange(N_EXPERTS, dtype=I32)[None, :]).astype(I32)
    cs = jnp.cumsum(oh, axis=0)
    rank = jnp.sum(cs * oh, axis=1) - 1
    counts = cs[-1]
    padded = ((counts + tile - 1) // tile) * tile
    ends = jnp.cumsum(padded)
    starts = ends - padded
    pos = jnp.sum(oh * starts[None, :], axis=1) + rank
    n_used = ends[-1] // tile
    tile_start = jnp.arange(n_tiles, dtype=I32) * tile
    tile_e = jnp.sum((tile_start[:, None] >= ends[None, :]).astype(I32), axis=1)
    last_e = jnp.sum((jnp.maximum(n_used - 1, 0) * tile >= ends).astype(I32))
    tile_e = jnp.minimum(tile_e, last_e).astype(I32)
    return pos.astype(I32), tile_e, n_used.astype(I32).reshape(1), n_tiles


def _dispatch_body(pos_ref, h_ref, xs_in_ref, xs_ref, sem, *, tm):
    del xs_in_ref
    base = pl.program_id(0) * tm

    def issue(r, c):
        for k in range(TOP_K):
            p = pos_ref[TOP_K * (base + r) + k]
            pltpu.make_async_copy(h_ref.at[pl.ds(r, 1)], xs_ref.at[pl.ds(p, 1)], sem).start()
        return c

    lax.fori_loop(0, tm, issue, 0, unroll=8)
    for k in range(TOP_K):
        pltpu.make_async_copy(h_ref, xs_ref.at[pl.ds(0, tm)], sem).wait()


def _dispatch(pos, h2, xs, *, tm):
    r, d = h2.shape
    grid_spec = pltpu.PrefetchScalarGridSpec(
        num_scalar_prefetch=1,
        grid=(r // tm,),
        in_specs=[pl.BlockSpec((tm, d), lambda i, p: (i, 0)),
                  pl.BlockSpec(memory_space=pl.ANY)],
        out_specs=pl.BlockSpec(memory_space=pl.ANY),
        scratch_shapes=[pltpu.SemaphoreType.DMA(())],
    )
    return pl.pallas_call(
        functools.partial(_dispatch_body, tm=tm),
        grid_spec=grid_spec,
        out_shape=jax.ShapeDtypeStruct(xs.shape, xs.dtype),
        input_output_aliases={2: 0},
        compiler_params=_cparams(1),
        name="moe_dispatch",
    )(pos, h2, xs)


def _moe_gemm_body(te_ref, nu_ref, x_ref, wg_ref, wu_ref, wd_ref, y_ref, wg_s, wu_s, wd_s):
    i = pl.program_id(0)
    e = te_ref[i]
    prev = te_ref[jnp.maximum(i - 1, 0)]

    @pl.when(jnp.logical_or(i == 0, e != prev))
    def _():
        wg_s[...] = wg_ref[0].astype(BF16)
        wu_s[...] = wu_ref[0].astype(BF16)
        wd_s[...] = wd_ref[0].astype(BF16)

    @pl.when(i < nu_ref[0])
    def _():
        xb = x_ref[...].astype(BF16)
        a = jnp.dot(xb, wg_s[...], preferred_element_type=F32)
        b = jnp.dot(xb, wu_s[...], preferred_element_type=F32)
        act = (a * jax.nn.sigmoid(a) * b).astype(BF16)
        y_ref[...] = jnp.dot(act, wd_s[...], preferred_element_type=F32)

    @pl.when(i >= nu_ref[0])
    def _():
        y_ref[...] = jnp.zeros_like(y_ref)


def _moe_gemm(tile_e, n_used, xs, w_g, w_u, w_d, *, layer, tile):
    rows, d = xs.shape
    f = w_g.shape[-1]
    ne = w_g.shape[1]
    n_tiles = rows // tile
    grid_spec = pltpu.PrefetchScalarGridSpec(
        num_scalar_prefetch=2,
        grid=(n_tiles,),
        in_specs=[pl.BlockSpec((tile, d), lambda i, te, nu: (i, 0)),
                  pl.BlockSpec((1, d, f), lambda i, te, nu: (layer * ne + te[i], 0, 0)),
                  pl.BlockSpec((1, d, f), lambda i, te, nu: (layer * ne + te[i], 0, 0)),
                  pl.BlockSpec((1, f, d), lambda i, te, nu: (layer * ne + te[i], 0, 0))],
        out_specs=pl.BlockSpec((tile, d), lambda i, te, nu: (i, 0)),
        scratch_shapes=[pltpu.VMEM((d, f), BF16), pltpu.VMEM((d, f), BF16), pltpu.VMEM((f, d), BF16)],
    )
    depth = w_g.shape[0]
    return pl.pallas_call(
        _moe_gemm_body,
        grid_spec=grid_spec,
        out_shape=jax.ShapeDtypeStruct((rows, d), F32),
        compiler_params=_cparams(1),
        name="moe_gemm",
    )(tile_e, n_used, xs, w_g.reshape(depth * ne, d, f), w_u.reshape(depth * ne, d, f),
      w_d.reshape(depth * ne, f, d))


def _combine_body(pos_ref, x_ref, rw_ref, gate_ref, y_hbm, o_ref, ybuf, sem, *, tm, mod_bcast):
    base = pl.program_id(0) * tm

    def issue(r, c):
        for k in range(TOP_K):
            p = pos_ref[TOP_K * (base + r) + k]
            pltpu.make_async_copy(y_hbm.at[pl.ds(p, 1)], ybuf.at[k, pl.ds(r, 1)], sem).start()
        return c

    lax.fori_loop(0, tm, issue, 0, unroll=8)
    for k in range(TOP_K):
        pltpu.make_async_copy(y_hbm.at[pl.ds(0, tm)], ybuf.at[k], sem).wait()
    gate = gate_ref[0] if mod_bcast else gate_ref[...]
    rw = rw_ref[...]
    moe = rw[:, 0:1] * ybuf[0] + rw[:, 1:2] * ybuf[1]
    o_ref[...] = x_ref[...] + gate * moe


def _combine(pos, x, rw, gate2, y, *, tm, mod_bcast, rows_per_batch):
    r, d = x.shape
    tpb = rows_per_batch // tm if mod_bcast else 1
    if mod_bcast:
        mod_spec = pl.BlockSpec((1, 1, d), lambda i, p: (i // tpb, 0, 0))
    else:
        mod_spec = pl.BlockSpec((tm, d), lambda i, p: (0, 0))
    grid_spec = pltpu.PrefetchScalarGridSpec(
        num_scalar_prefetch=1,
        grid=(r // tm,),
        in_specs=[pl.BlockSpec((tm, d), lambda i, p: (i, 0)),
                  pl.BlockSpec((tm, LANES), lambda i, p: (i, 0)),
                  mod_spec,
                  pl.BlockSpec(memory_space=pl.ANY)],
        out_specs=pl.BlockSpec((tm, d), lambda i, p: (i, 0)),
        scratch_shapes=[pltpu.VMEM((TOP_K, tm, d), F32), pltpu.SemaphoreType.DMA(())],
    )
    return pl.pallas_call(
        functools.partial(_combine_body, tm=tm, mod_bcast=mod_bcast),
        grid_spec=grid_spec,
        out_shape=jax.ShapeDtypeStruct((r, d), F32),
        compiler_params=_cparams(1),
        name="moe_combine",
    )(pos, x, rw, gate2, y)


def _final_norm_body(x_ref, g_ref, o_ref):
    o_ref[...] = _rms(x_ref[...], g_ref[...])


def _final_norm(x, g, *, tm):
    r, d = x.shape
    return pl.pallas_call(
        _final_norm_body,
        grid=(r // tm,),
        in_specs=[pl.BlockSpec((tm, d), lambda i: (i, 0)), pl.BlockSpec((1, d), lambda i: (0, 0))],
        out_specs=pl.BlockSpec((tm, d), lambda i: (i, 0)),
        out_shape=jax.ShapeDtypeStruct((r, d), F32),
        compiler_params=_cparams(1),
        name="final_norm",
    )(x, g)


def _rope_tables(pos):
    half = QK_ROPE // 2
    inv = ROPE_THETA ** (-jnp.arange(half, dtype=F32) * (2.0 / QK_ROPE))
    ang = pos.astype(F32)[:, None] * inv[None, :]
    c, s = jnp.cos(ang), jnp.sin(ang)
    reps = LANES // half
    cos4 = jnp.tile(c, (1, reps))
    sin4 = jnp.tile(jnp.concatenate([-s, s], axis=-1), (1, reps // 2))
    return cos4, sin4


def _pick_tile(n, pref):
    t = min(pref, n)
    while n % t:
        t //= 2
    return t


def kernel(x_prompt, x_sample, cache_kv_latent, cache_k_rope, state_ssm_re, state_ssm_im, page_table,
           c_prompt, c_sample, norm_mix_g, norm_ffn_g, w_ada, b_ada, w_in, ssm_lam_re, ssm_lam_im,
           ssm_log_dt, ssm_b_re, ssm_b_im, ssm_c_re, ssm_c_im, ssm_d, w_glu, b_glu, kv_norm_g, w_uk,
           w_uv, g_ssm_out, g_attn_out, w_out, w_route_group, b_route_group, w_route_expert,
           b_route_expert, w_exp_gate, w_exp_up, w_exp_down, final_norm_g):
    bp, tp, d = x_prompt.shape
    bs, ts_, _ = x_sample.shape
    depth = w_in.shape[0]
    kv_rank = w_uk.shape[1]
    ssm_w = d // 2
    n_groups = ssm_w // SSM_CH
    nst = n_groups * SSM_STATE
    n_pages, page = page_table.shape[1], cache_kv_latent.shape[2]
    past_len = n_pages * page
    rp = bp * tp
    rs = bs * ts_

    q_cols = w_in[..., ssm_w:ssm_w + N_HEADS * QK_DIM].reshape(depth, d, N_HEADS, QK_DIM)
    off = ssm_w + N_HEADS * QK_DIM
    w_in_p = jnp.concatenate([
        w_in[..., :ssm_w],
        q_cols[..., :QK_NOPE].reshape(depth, d, N_HEADS * QK_NOPE),
        q_cols[..., QK_NOPE:].reshape(depth, d, N_HEADS * QK_ROPE),
        w_in[..., off:off + kv_rank],
        w_in[..., off + kv_rank:], w_in[..., off + kv_rank:]], axis=-1).astype(BF16)
    wuk_t = jnp.transpose(w_uk, (0, 2, 3, 1)).astype(BF16)
    wuv_tt = jnp.transpose(w_uv, (0, 2, 3, 1)).astype(BF16)
    wuv_flat = w_uv.reshape(depth, kv_rank, N_HEADS * V_HEAD_DIM).astype(BF16)
    w_glu_b = w_glu.astype(BF16)
    w_out_b = w_out.astype(BF16)
    n_route = MOE_GROUPS + N_EXPERTS
    w_route = jnp.concatenate([w_route_group, w_route_expert.reshape(depth, d, N_EXPERTS),
                               jnp.zeros((depth, d, LANES - n_route), F32)], axis=-1).astype(BF16)
    b_route = jnp.concatenate([b_route_group, b_route_expert.reshape(depth, N_EXPERTS),
                               jnp.zeros((depth, LANES - n_route), F32)], axis=-1).reshape(depth, 1, LANES)
    tabs_all = _ssm_tables(ssm_lam_re, ssm_lam_im, ssm_log_dt, ssm_b_re, ssm_b_im, ssm_c_re, ssm_c_im,
                           ssm_d, SSM_CHUNK)
    cos_p, sin_p = _rope_tables(jnp.arange(tp))
    cos_s, sin_s = _rope_tables(past_len + jnp.arange(ts_))
    cos_s = cos_s.reshape(ts_, 1, LANES)
    sin_s = sin_s.reshape(ts_, 1, LANES)

    pad = (-(bs + bp)) % 8
    c_all = jnp.concatenate([c_sample, c_prompt, jnp.zeros((pad, d), F32)], axis=0)
    mod = _ada(c_all, w_ada, b_ada)

    tm_p = _pick_tile(tp, 256)
    ts_ssm = _pick_tile(tp, 512)
    tq = _pick_tile(tp, 512)
    tk = _pick_tile(tp, 512)
    moe_tile = 256
    tm_final = _pick_tile(rp, 512)

    xp = x_prompt.reshape(rp, d)
    xs = jnp.transpose(x_sample, (1, 0, 2)).reshape(rs, d)
    cache_pe_t = jnp.transpose(cache_k_rope, (0, 1, 3, 2))
    h0re = state_ssm_re.reshape(depth, bs, nst)
    h0im = state_ssm_im.reshape(depth, bs, nst)

    lat_p, pe_p, sre_p, sim_p = [], [], [], []
    lat_s, pe_s, sre_s, sim_s = [], [], [], []
    for l in range(depth):
        mod_s = [mod[l, :bs, i * d:(i + 1) * d] for i in range(6)]
        mod_p = [mod[l, bs:bs + bp, i * d:(i + 1) * d].reshape(bp, 1, d) for i in range(6)]
        tabs = {k: (v if k == "tri" else v[l]) for k, v in tabs_all.items()}
        g_mix = norm_mix_g[l].reshape(1, d)
        g_ffn = norm_ffn_g[l].reshape(1, d)
        kvg = kv_norm_g[l].reshape(1, kv_rank)
        bglu = b_glu[l].reshape(1, ssm_w)
        gso = g_ssm_out[l].reshape(1, ssm_w)
        gao = g_attn_out[l].reshape(1, N_HEADS * V_HEAD_DIM)

        u_p, q_p, kcat_p, ckv_p, kpe_p = _inproj(
            xp, g_mix, mod_p[0], mod_p[1], w_in_p[l], wuk_t[l], kvg, cos_p, sin_p,
            tm=tm_p, mod_bcast=True, rope_bcast=False, rows_per_batch=tp)
        ssm_p, s_re, s_im = _ssm_prompt(u_p.reshape(bp, tp, ssm_w), tabs, w_glu_b[l], bglu, gso, ts=ts_ssm)
        q_t = jnp.transpose(q_p, (0, 2, 1))
        v_t = jnp.transpose(kcat_p[:, :kv_rank])
        att_pt = _attn_prompt(kcat_p, q_t, v_t, wuv_tt[l], gao.reshape(-1, 1), batch=bp, seq=tp, tq=tq, tk=tk,
                              heads_per_group=N_HEADS)
        att_p = jnp.transpose(att_pt)
        xp, h2_p, ridx_p, rw_p = _outproj(
            xp, ssm_p.reshape(rp, ssm_w), att_p, w_out_b[l], mod_p[2], g_ffn, mod_p[3], mod_p[4],
            w_route[l], b_route[l], tm=tm_p, mod_bcast=True, rows_per_batch=tp)
        lat_p.append(ckv_p.reshape(bp, tp, kv_rank))
        pe_p.append(kpe_p.reshape(bp, tp, QK_ROPE))
        sre_p.append(s_re.reshape(bp, n_groups, SSM_STATE))
        sim_p.append(s_im.reshape(bp, n_groups, SSM_STATE))

        u_s, q_s, kcat_s, ckv_s, kpe_s = _inproj(
            xs, g_mix, mod_s[0], mod_s[1], w_in_p[l], wuk_t[l], kvg, cos_s, sin_s,
            tm=bs, mod_bcast=False, rope_bcast=True, rows_per_batch=bs)
        ssm_s, s_re, s_im = _ssm_sample(u_s, tabs, h0re[l], h0im[l], w_glu_b[l], bglu, gso)
        q_sb = jnp.transpose(q_s.reshape(N_HEADS, ts_, bs, kv_rank + QK_ROPE), (2, 0, 1, 3))
        q_sb = q_sb.reshape(bs, N_HEADS * ts_, kv_rank + QK_ROPE)
        k_new = jnp.transpose(kcat_s.reshape(ts_, bs, kv_rank + QK_ROPE), (1, 0, 2))
        att_s = _attn_sample(page_table, q_sb, k_new, wuv_flat[l], gao, cache_kv_latent, cache_pe_t,
                             layer=l, n_chunks=2)
        att_s = jnp.transpose(att_s, (1, 0, 2)).reshape(rs, N_HEADS * V_HEAD_DIM)
        xs, h2_s, ridx_s, rw_s = _outproj(
            xs, ssm_s, att_s, w_out_b[l], mod_s[2], g_ffn, mod_s[3], mod_s[4],
            w_route[l], b_route[l], tm=bs, mod_bcast=False, rows_per_batch=bs)
        lat_s.append(jnp.transpose(ckv_s.reshape(ts_, bs, kv_rank), (1, 0, 2)))
        pe_s.append(jnp.transpose(kpe_s.reshape(ts_, bs, QK_ROPE), (1, 0, 2)))
        sre_s.append(s_re.reshape(bs, n_groups, SSM_STATE))
        sim_s.append(s_im.reshape(bs, n_groups, SSM_STATE))

        e_pairs = jnp.concatenate([ridx_p[:, :TOP_K].reshape(-1), ridx_s[:, :TOP_K].reshape(-1)])
        pos, tile_e, n_used, n_tiles = _route_meta(e_pairs, moe_tile)
        pos_p, pos_s = pos[:TOP_K * rp], pos[TOP_K * rp:]
        xsrt = jnp.zeros((n_tiles * moe_tile, d), F32)
        xsrt = _dispatch(pos_p, h2_p, xsrt, tm=tm_p)
        xsrt = _dispatch(pos_s, h2_s, xsrt, tm=bs)
        y = _moe_gemm(tile_e, n_used, xsrt, w_exp_gate, w_exp_up, w_exp_down, layer=l, tile=moe_tile)
        xp = _combine(pos_p, xp, rw_p, mod_p[5], y, tm=tm_p, mod_bcast=True, rows_per_batch=tp)
        xs = _combine(pos_s, xs, rw_s, mod_s[5], y, tm=bs, mod_bcast=False, rows_per_batch=bs)

    fg = final_norm_g.reshape(1, d)
    y_prompt = _final_norm(xp, fg, tm=tm_final).reshape(bp, tp, d)
    y_sample = jnp.transpose(_final_norm(xs, fg, tm=bs).reshape(ts_, bs, d), (1, 0, 2))
    return (y_prompt, y_sample,
            jnp.stack(lat_p), jnp.stack(pe_p), jnp.stack(sre_p), jnp.stack(sim_p),
            jnp.stack(lat_s), jnp.stack(pe_s), jnp.stack(sre_s), jnp.stack(sim_s))
```

```python
import functools
import math

import numpy as np
import jax
import jax.numpy as jnp
from jax import lax
from jax.experimental import pallas as pl
from jax.experimental.pallas import tpu as pltpu

F32 = jnp.float32
BF16 = jnp.bfloat16
I32 = jnp.int32

SSM_CH = 16
SSM_STATE = 64
N_HEADS = 8
QK_NOPE = 128
QK_ROPE = 64
QK_DIM = QK_NOPE + QK_ROPE
V_HEAD_DIM = 128
ROPE_THETA = 10000.0
EPS = 1e-6
SM_SCALE = QK_DIM ** -0.5
MOE_GROUPS = 4
EXPERTS_PER_GROUP = 4
N_EXPERTS = MOE_GROUPS * EXPERTS_PER_GROUP
TOP_K = 2

LANES = 128
MXU_DIM = 256
VMEM_LIMIT = 56 * 1024 * 1024

GROUPS_PER_BLOCK = MXU_DIM // SSM_CH
SSM_CHUNK = 64
NEG_BIG = -1e30


def _cparams(n_axes):
    return pltpu.CompilerParams(dimension_semantics=("arbitrary",) * n_axes,
                                vmem_limit_bytes=VMEM_LIMIT)


def _rms(x, g):
    return x * lax.rsqrt(jnp.mean(x * x, axis=-1, keepdims=True) + EPS) * g


def _ada_body(c_ref, w_ref, b_ref, o_ref):
    c = c_ref[...]
    s = (c * jax.nn.sigmoid(c)).astype(BF16)
    o_ref[0] = jnp.dot(s, w_ref[0].astype(BF16), preferred_element_type=F32) + b_ref[0]


def _ada(c_all, w_ada, b_ada):
    depth, d, n6 = w_ada.shape
    r = c_all.shape[0]
    tn = 1024
    return pl.pallas_call(
        _ada_body,
        grid=(depth, n6 // tn),
        in_specs=[pl.BlockSpec((r, d), lambda l, j: (0, 0)),
                  pl.BlockSpec((1, d, tn), lambda l, j: (l, 0, j)),
                  pl.BlockSpec((1, 1, tn), lambda l, j: (l, 0, j))],
        out_specs=pl.BlockSpec((1, r, tn), lambda l, j: (l, 0, j)),
        out_shape=jax.ShapeDtypeStruct((depth, r, n6), F32),
        compiler_params=_cparams(2),
        name="ada_ln",
    )(c_all, w_ada, b_ada.reshape(depth, 1, n6))


def _rope_pairs(p, cos4, sin4, first_half):
    swapped = jnp.where(first_half, pltpu.roll(p, 96, 1), pltpu.roll(p, 32, 1))
    return p * cos4 + swapped * sin4


def _inproj_body(x_ref, g_ref, sh_ref, sc_ref, w_ref, wuk_ref, kvg_ref, cos_ref, sin_ref,
                 u_ref, q_ref, kcat_ref, ckv_ref, kpe_ref, *maybe_vt_ref,
                 mod_bcast, rope_bcast, ssm_w, kv_rank, transposed_q):
    x = x_ref[...]
    tm = x.shape[0]
    sh = sh_ref[0] if mod_bcast else sh_ref[...]
    sc = sc_ref[0] if mod_bcast else sc_ref[...]
    h = _rms(x, g_ref[...]) * (1.0 + sc) + sh
    z = jnp.dot(h.astype(BF16), w_ref[...], preferred_element_type=F32)
    u_ref[...] = z[:, :ssm_w]
    cos4 = cos_ref[0] if rope_bcast else cos_ref[...]
    sin4 = sin_ref[0] if rope_bcast else sin_ref[...]
    lane = lax.broadcasted_iota(I32, (tm, LANES), 1)
    first_half = (lane % QK_ROPE) < (QK_ROPE // 2)
    off_nope = ssm_w
    off_rope = off_nope + N_HEADS * QK_NOPE
    off_kv = off_rope + N_HEADS * QK_ROPE
    off_kr = off_kv + kv_rank
    for hp in range(N_HEADS // 2):
        pr = z[:, off_rope + hp * LANES: off_rope + (hp + 1) * LANES]
        rp = _rope_pairs(pr, cos4, sin4, first_half) * SM_SCALE
        if transposed_q:
            rpt = rp.T.astype(BF16)
            q_ref[2 * hp, kv_rank:kv_rank + QK_ROPE, :] = rpt[:QK_ROPE]
            q_ref[2 * hp + 1, kv_rank:kv_rank + QK_ROPE, :] = rpt[QK_ROPE:]
        else:
            rp = rp.astype(BF16)
            q_ref[2 * hp, :, kv_rank:kv_rank + QK_ROPE] = rp[:, :QK_ROPE]
            q_ref[2 * hp + 1, :, kv_rank:kv_rank + QK_ROPE] = rp[:, QK_ROPE:]
    for hd in range(N_HEADS):
        nope = z[:, off_nope + hd * QK_NOPE: off_nope + (hd + 1) * QK_NOPE].astype(BF16)
        ql = jnp.dot(nope, wuk_ref[hd], preferred_element_type=F32) * SM_SCALE
        if transposed_q:
            q_ref[hd, :kv_rank, :] = ql.T.astype(BF16)
        else:
            q_ref[hd, :, :kv_rank] = ql.astype(BF16)
    ckv = _rms(z[:, off_kv:off_kr], kvg_ref[...])
    ckv_ref[...] = ckv
    kk = _rope_pairs(z[:, off_kr:off_kr + LANES], cos4, sin4, first_half)
    kpe_ref[...] = kk[:, :QK_ROPE]
    kcat_ref[:, :kv_rank] = ckv.astype(BF16)
    kcat_ref[:, kv_rank:kv_rank + QK_ROPE] = kk[:, :QK_ROPE].astype(BF16)
    if transposed_q:
        maybe_vt_ref[0][...] = ckv.T.astype(BF16)


def _inproj(x, g, shift, scale, w, wuk_t, kvg, cos4, sin4, *, tm, mod_bcast, rope_bcast, rows_per_batch,
            transposed_q):
    r, d = x.shape
    ncols = w.shape[1]
    kv_rank = wuk_t.shape[2]
    ssm_w = d // 2
    qd = kv_rank + QK_ROPE
    tpb = rows_per_batch // tm if mod_bcast else 1
    if mod_bcast:
        mod_spec = pl.BlockSpec((1, 1, d), lambda i: (i // tpb, 0, 0))
    else:
        mod_spec = pl.BlockSpec((tm, d), lambda i: (0, 0))
    if rope_bcast:
        rope_spec = pl.BlockSpec((1, 1, LANES), lambda i: (i, 0, 0))
    else:
        rope_spec = pl.BlockSpec((tm, LANES), lambda i: (i % tpb, 0))
    if transposed_q:
        q_spec = pl.BlockSpec((N_HEADS, qd, tm), lambda i: (0, 0, i))
        q_shape = jax.ShapeDtypeStruct((N_HEADS, qd, r), BF16)
    else:
        q_spec = pl.BlockSpec((N_HEADS, tm, qd), lambda i: (0, i, 0))
        q_shape = jax.ShapeDtypeStruct((N_HEADS, r, qd), BF16)
    out_specs = [pl.BlockSpec((tm, ssm_w), lambda i: (i, 0)),
                 q_spec,
                 pl.BlockSpec((tm, qd), lambda i: (i, 0)),
                 pl.BlockSpec((tm, kv_rank), lambda i: (i, 0)),
                 pl.BlockSpec((tm, QK_ROPE), lambda i: (i, 0))]
    out_shape = [jax.ShapeDtypeStruct((r, ssm_w), F32),
                 q_shape,
                 jax.ShapeDtypeStruct((r, qd), BF16),
                 jax.ShapeDtypeStruct((r, kv_rank), F32),
                 jax.ShapeDtypeStruct((r, QK_ROPE), F32)]
    if transposed_q:
        out_specs.append(pl.BlockSpec((kv_rank, tm), lambda i: (0, i)))
        out_shape.append(jax.ShapeDtypeStruct((kv_rank, r), BF16))
    body = functools.partial(_inproj_body, mod_bcast=mod_bcast, rope_bcast=rope_bcast,
                             ssm_w=ssm_w, kv_rank=kv_rank, transposed_q=transposed_q)
    return pl.pallas_call(
        body,
        grid=(r // tm,),
        in_specs=[pl.BlockSpec((tm, d), lambda i: (i, 0)),
                  pl.BlockSpec((1, d), lambda i: (0, 0)),
                  mod_spec, mod_spec,
                  pl.BlockSpec((d, ncols), lambda i: (0, 0)),
                  pl.BlockSpec(wuk_t.shape, lambda i: (0, 0, 0)),
                  pl.BlockSpec((1, kv_rank), lambda i: (0, 0)),
                  rope_spec, rope_spec],
        out_specs=out_specs,
        out_shape=out_shape,
        compiler_params=_cparams(1),
        name="in_proj_mla_prep",
    )(x, g, shift, scale, w, wuk_t, kvg, cos4, sin4)


def _glu_norm(y, wglu_ref, bglu_ref, gout_ref):
    g = jax.nn.gelu(y)
    gate = jnp.dot(g.astype(BF16), wglu_ref[...], preferred_element_type=F32) + bglu_ref[...]
    return _rms(g * jax.nn.sigmoid(gate), gout_ref[...])


def _ssm_prompt_body(u_ref, bbd_ref, cbd_ref, tri_ref, pre_re_ref, pre_im_ref, post_re_ref, post_im_ref,
                     a_re_ref, a_im_ref, d_ref, wglu_ref, bglu_ref, gout_ref,
                     y_ref, sre_ref, sim_ref, bu_scr, h_scr, hprev_scr, yacc_scr, *, n_blocks, chunk):
    tc = pl.program_id(1)
    ts = u_ref.shape[1]
    sl = GROUPS_PER_BLOCK * SSM_STATE

    @pl.when(tc == 0)
    def _():
        hprev_scr[...] = jnp.zeros_like(hprev_scr)

    u = u_ref[0]
    ub = u.astype(BF16)
    tri = tri_ref[...]
    for j in range(n_blocks):
        cols = slice(j * MXU_DIM, (j + 1) * MXU_DIM)
        lanes = slice(j * sl, (j + 1) * sl)
        bu_scr[...] = jnp.dot(ub[:, cols], bbd_ref[j], preferred_element_type=F32)
        a_re = a_re_ref[:, lanes]
        a_im = a_im_ref[:, lanes]

        def sub(s, carry):
            hr, hi = carry
            r0 = s * chunk
            b_re = bu_scr[pl.ds(r0, chunk), 0:sl]
            b_im = bu_scr[pl.ds(r0, chunk), sl:2 * sl]
            p_re = pre_re_ref[:, lanes]
            p_im = pre_im_ref[:, lanes]
            x_re = (p_re * b_re - p_im * b_im).astype(BF16)
            x_im = (p_re * b_im + p_im * b_re).astype(BF16)
            z_re = jnp.dot(tri, x_re, preferred_element_type=F32) + (a_re * hr - a_im * hi)
            z_im = jnp.dot(tri, x_im, preferred_element_type=F32) + (a_re * hi + a_im * hr)
            q_re = post_re_ref[:, lanes]
            q_im = post_im_ref[:, lanes]
            h_re = q_re * z_re - q_im * z_im
            h_im = q_re * z_im + q_im * z_re
            h_scr[pl.ds(r0, chunk), 0:sl] = h_re.astype(BF16)
            h_scr[pl.ds(r0, chunk), sl:2 * sl] = h_im.astype(BF16)
            return h_re[chunk - 1:chunk], h_im[chunk - 1:chunk]

        carry = (hprev_scr[2 * j:2 * j + 1], hprev_scr[2 * j + 1:2 * j + 2])
        for s in range(ts // chunk):
            carry = sub(s, carry)
        hr, hi = carry
        hprev_scr[2 * j:2 * j + 1] = hr
        hprev_scr[2 * j + 1:2 * j + 2] = hi
        sre_ref[0, :, lanes] = hr
        sim_ref[0, :, lanes] = hi
        yacc_scr[:, cols] = (jnp.dot(h_scr[...], cbd_ref[j], preferred_element_type=F32)
                             + d_ref[:, cols] * u[:, cols])
    y_ref[0] = _glu_norm(yacc_scr[...], wglu_ref, bglu_ref, gout_ref).astype(BF16)


def _ssm_prompt(u, tabs, wglu, bglu, gout, *, ts):
    b, t, w = u.shape
    n_blocks = w // MXU_DIM
    sl2 = 2 * GROUPS_PER_BLOCK * SSM_STATE
    nst = (w // SSM_CH) * SSM_STATE
    chunk = SSM_CHUNK
    full2 = lambda shape: pl.BlockSpec(shape, lambda i, j: (0,) * len(shape))
    body = functools.partial(_ssm_prompt_body, n_blocks=n_blocks, chunk=chunk)
    return pl.pallas_call(
        body,
        grid=(b, t // ts),
        in_specs=[pl.BlockSpec((1, ts, w), lambda i, j: (i, j, 0)),
                  full2((n_blocks, MXU_DIM, sl2)), full2((n_blocks, sl2, MXU_DIM)),
                  full2((chunk, chunk)),
                  full2((chunk, nst)), full2((chunk, nst)), full2((chunk, nst)), full2((chunk, nst)),
                  full2((1, nst)), full2((1, nst)), full2((1, w)),
                  full2((w, w)), full2((1, w)), full2((1, w))],
        out_specs=[pl.BlockSpec((1, ts, w), lambda i, j: (i, j, 0)),
                   pl.BlockSpec((1, 1, nst), lambda i, j: (i, 0, 0)),
                   pl.BlockSpec((1, 1, nst), lambda i, j: (i, 0, 0))],
        out_shape=[jax.ShapeDtypeStruct((b, t, w), BF16),
                   jax.ShapeDtypeStruct((b, 1, nst), F32),
                   jax.ShapeDtypeStruct((b, 1, nst), F32)],
        scratch_shapes=[pltpu.VMEM((ts, sl2), F32), pltpu.VMEM((ts, sl2), BF16),
                        pltpu.VMEM((2 * n_blocks, sl2 // 2), F32), pltpu.VMEM((ts, w), F32)],
        compiler_params=_cparams(2),
        name="ssm_prompt",
    )(u, tabs["bbd"], tabs["cbd"], tabs["tri"], tabs["pre_re"], tabs["pre_im"], tabs["post_re"],
      tabs["post_im"], tabs["a_re"], tabs["a_im"], tabs["d"], wglu, bglu, gout)


def _ssm_sample_body(u_ref, bbd_ref, cbd_ref, a_re_ref, a_im_ref, d_ref, h0re_ref, h0im_ref,
                     wglu_ref, bglu_ref, gout_ref, y_ref, sre_ref, sim_ref,
                     bu_scr, h_scr, yacc_scr, *, n_blocks, n_steps):
    sl = GROUPS_PER_BLOCK * SSM_STATE
    bs = h0re_ref.shape[0]
    u = u_ref[...]
    ub = u.astype(BF16)
    for j in range(n_blocks):
        cols = slice(j * MXU_DIM, (j + 1) * MXU_DIM)
        lanes = slice(j * sl, (j + 1) * sl)
        bu_scr[...] = jnp.dot(ub[:, cols], bbd_ref[j], preferred_element_type=F32)
        a_re = a_re_ref[:, lanes]
        a_im = a_im_ref[:, lanes]
        hr = h0re_ref[:, lanes]
        hi = h0im_ref[:, lanes]
        for t in range(n_steps):
            rows = slice(t * bs, (t + 1) * bs)
            nr = (a_re * hr - a_im * hi) + bu_scr[rows, 0:sl]
            ni = (a_re * hi + a_im * hr) + bu_scr[rows, sl:2 * sl]
            hr, hi = nr, ni
            h_scr[rows, 0:sl] = hr.astype(BF16)
            h_scr[rows, sl:2 * sl] = hi.astype(BF16)
        sre_ref[:, lanes] = hr
        sim_ref[:, lanes] = hi
        yacc_scr[:, cols] = (jnp.dot(h_scr[...], cbd_ref[j], preferred_element_type=F32)
                             + d_ref[:, cols] * u[:, cols])
    y_ref[...] = _glu_norm(yacc_scr[...], wglu_ref, bglu_ref, gout_ref).astype(BF16)


def _ssm_sample(u, tabs, h0re, h0im, wglu, bglu, gout):
    r, w = u.shape
    bs, nst = h0re.shape
    n_blocks = w // MXU_DIM
    sl2 = 2 * GROUPS_PER_BLOCK * SSM_STATE
    body = functools.partial(_ssm_sample_body, n_blocks=n_blocks, n_steps=r // bs)
    return pl.pallas_call(
        body,
        out_shape=[jax.ShapeDtypeStruct((r, w), BF16),
                   jax.ShapeDtypeStruct((bs, nst), F32),
                   jax.ShapeDtypeStruct((bs, nst), F32)],
        scratch_shapes=[pltpu.VMEM((r, sl2), F32), pltpu.VMEM((r, sl2), BF16), pltpu.VMEM((r, w), F32)],
        compiler_params=pltpu.CompilerParams(vmem_limit_bytes=VMEM_LIMIT),
        name="ssm_sample",
    )(u, tabs["bbd"], tabs["cbd"], tabs["a_re"], tabs["a_im"], tabs["d"], h0re, h0im, wglu, bglu, gout)


def _ssm_tables(lam_re, lam_im, log_dt, b_re, b_im, c_re, c_im, d_skip, chunk):
    depth, g, n = lam_re.shape
    p = b_re.shape[-1]
    gb = GROUPS_PER_BLOCK
    nb = g // gb
    lr = lam_re.astype(F32)
    li = lam_im.astype(F32)
    dt = jnp.exp(log_dt.astype(F32))[..., None]
    mag = jnp.exp(lr * dt)
    ab_re = mag * jnp.cos(li * dt)
    ab_im = mag * jnp.sin(li * dt)
    den = lr * lr + li * li
    f_re = ((ab_re - 1.0) * lr + ab_im * li) / den
    f_im = (ab_im * lr - (ab_re - 1.0) * li) / den
    bb_re = f_re[..., None] * b_re - f_im[..., None] * b_im
    bb_im = f_re[..., None] * b_im + f_im[..., None] * b_re
    ks = np.arange(chunk)
    pw_re = jnp.ones((chunk, depth, g, n), F32)
    pw_im = jnp.zeros((chunk, depth, g, n), F32)
    sq_re, sq_im = ab_re, ab_im
    for bit in range(max(1, int(chunk - 1).bit_length())):
        sel = jnp.asarray(((ks >> bit) & 1).astype(np.float32))[:, None, None, None]
        m_re = sel * sq_re + (1.0 - sel)
        m_im = sel * sq_im
        pw_re, pw_im = pw_re * m_re - pw_im * m_im, pw_re * m_im + pw_im * m_re
        sq_re, sq_im = sq_re * sq_re - sq_im * sq_im, 2.0 * sq_re * sq_im
    nrm = pw_re * pw_re + pw_im * pw_im
    inv_re = pw_re / nrm
    inv_im = -pw_im / nrm
    flat = lambda a: jnp.moveaxis(a, 0, 1).reshape(depth, chunk, g * n)
    eye = jnp.eye(gb, dtype=F32)

    def bdiag_in(bb):
        bbj = bb.reshape(depth, nb, gb, n, p)
        return jnp.einsum("gh,djgnp->djgphn", eye, bbj).reshape(depth, nb, gb * p, gb * n)

    def bdiag_out(cc):
        ccj = cc.reshape(depth, nb, gb, p, n)
        return jnp.einsum("gh,djgpn->djgnhp", eye, ccj).reshape(depth, nb, gb * n, gb * p)

    bbd = jnp.concatenate([bdiag_in(bb_re), bdiag_in(bb_im)], axis=-1).astype(BF16)
    cbd = jnp.concatenate([bdiag_out(c_re.astype(F32)), bdiag_out(-c_im.astype(F32))], axis=-2).astype(BF16)
    tri = jnp.asarray(np.tril(np.ones((chunk, chunk), np.float32))).astype(BF16)
    return {
        "bbd": bbd, "cbd": cbd, "tri": tri,
        "pre_re": flat(inv_re), "pre_im": flat(inv_im),
        "post_re": flat(pw_re), "post_im": flat(pw_im),
        "a_re": ab_re.reshape(depth, 1, g * n), "a_im": ab_im.reshape(depth, 1, g * n),
        "d": d_skip.astype(F32).reshape(depth, 1, g * p),
    }


def _attn_prompt_body(qi_ref, ki_ref, last_ref, k_ref, qt_ref, vt_ref, wuvt_ref, g_ref, o_ref,
                      m_scr, l_scr, acc_scr, *, tq, tk, hg):
    pidx = pl.program_id(1)
    qi = qi_ref[pidx]
    ki = ki_ref[pidx]

    @pl.when(ki == 0)
    def _():
        m_scr[...] = jnp.full_like(m_scr, NEG_BIG)
        l_scr[...] = jnp.zeros_like(l_scr)
        acc_scr[...] = jnp.zeros_like(acc_scr)

    k = k_ref[...]
    vt = vt_ref[...]
    kpos = ki * tk + lax.broadcasted_iota(I32, (tk, hg * tq), 0)
    qpos = qi * tq + (lax.broadcasted_iota(I32, (tk, hg * tq), 1) & (tq - 1))
    visible = kpos <= qpos
    for g0 in range(0, N_HEADS, hg):
        cols = slice(g0 * tq, (g0 + hg) * tq)
        qt = jnp.concatenate([qt_ref[hd] for hd in range(g0, g0 + hg)], axis=1)
        st = jnp.dot(k, qt, preferred_element_type=F32)
        st = jnp.where(visible, st, NEG_BIG)
        m_prev = m_scr[:, cols]
        m_new = jnp.maximum(m_prev, jnp.max(st, axis=0, keepdims=True))
        alpha = jnp.exp(m_prev - m_new)
        p = jnp.exp(st - m_new)
        l_scr[:, cols] = alpha * l_scr[:, cols] + jnp.sum(p, axis=0, keepdims=True)
        acc_scr[:, cols] = alpha * acc_scr[:, cols] + jnp.dot(vt, p.astype(BF16),
                                                              preferred_element_type=F32)
        m_scr[:, cols] = m_new

    @pl.when(ki == last_ref[pidx])
    def _():
        outs = []
        for hd in range(N_HEADS):
            cols = slice(hd * tq, (hd + 1) * tq)
            ot = (acc_scr[:, cols] / l_scr[:, cols]).astype(BF16)
            outs.append(jnp.dot(wuvt_ref[hd], ot, preferred_element_type=F32))
        att = jnp.concatenate(outs, axis=0).T
        o_ref[...] = _rms(att, g_ref[...]).astype(BF16)


def _attn_prompt(k_cat, q_t, v_t, wuv_tt, g_attn, *, batch, seq, tq, tk, heads_per_group):
    qd = k_cat.shape[1]
    kv_rank = v_t.shape[0]
    nq = seq // tq
    qi_l, ki_l, last_l = [], [], []
    for qi in range(nq):
        last = (qi * tq + tq - 1) // tk
        for ki in range(last + 1):
            qi_l.append(qi)
            ki_l.append(ki)
            last_l.append(last)
    npairs = len(qi_l)
    qi_a = jnp.asarray(np.array(qi_l, np.int32))
    ki_a = jnp.asarray(np.array(ki_l, np.int32))
    last_a = jnp.asarray(np.array(last_l, np.int32))
    nqb = seq // tq
    nkb = seq // tk
    d_out = N_HEADS * V_HEAD_DIM
    grid_spec = pltpu.PrefetchScalarGridSpec(
        num_scalar_prefetch=3,
        grid=(batch, npairs),
        in_specs=[pl.BlockSpec((tk, qd), lambda b, p, qi, ki, la: (b * nkb + ki[p], 0)),
                  pl.BlockSpec((N_HEADS, qd, tq), lambda b, p, qi, ki, la: (0, 0, b * nqb + qi[p])),
                  pl.BlockSpec((kv_rank, tk), lambda b, p, qi, ki, la: (0, b * nkb + ki[p])),
                  pl.BlockSpec(wuv_tt.shape, lambda b, p, qi, ki, la: (0, 0, 0)),
                  pl.BlockSpec((1, d_out), lambda b, p, qi, ki, la: (0, 0))],
        out_specs=pl.BlockSpec((tq, d_out), lambda b, p, qi, ki, la: (b * nqb + qi[p], 0)),
        scratch_shapes=[pltpu.VMEM((1, N_HEADS * tq), F32), pltpu.VMEM((1, N_HEADS * tq), F32),
                        pltpu.VMEM((kv_rank, N_HEADS * tq), F32)],
    )
    body = functools.partial(_attn_prompt_body, tq=tq, tk=tk, hg=heads_per_group)
    return pl.pallas_call(
        body,
        grid_spec=grid_spec,
        out_shape=jax.ShapeDtypeStruct((batch * seq, d_out), BF16),
        compiler_params=_cparams(2),
        name="attn_prompt",
    )(qi_a, ki_a, last_a, k_cat, q_t, v_t, wuv_tt, g_attn)


def _attn_sample_body(pt_ref, q_ref, knew_ref, wuv_ref, g_ref, lat_hbm, pet_hbm, o_ref,
                      lat_buf, pet_buf, sems, *, layer, n_pages, page, kv_rank, n_new, n_chunks):
    b = pl.program_id(0)
    nb = pl.num_programs(0)

    def page_copies(bb, slot, j):
        pg = pt_ref[bb * n_pages + j]
        return (pltpu.make_async_copy(lat_hbm.at[layer, pg], lat_buf.at[slot, pl.ds(j * page, page)],
                                      sems.at[slot, 0]),
                pltpu.make_async_copy(pet_hbm.at[layer, pg], pet_buf.at[slot, :, pl.ds(j * page, page)],
                                      sems.at[slot, 1]))

    def start_all(bb, slot):
        for j in range(n_pages):
            c0, c1 = page_copies(bb, slot, j)
            c0.start()
            c1.start()

    @pl.when(b == 0)
    def _():
        start_all(0, 0)

    slot = b % 2

    @pl.when(b + 1 < nb)
    def _():
        start_all(b + 1, 1 - slot)

    for j in range(n_pages):
        c0, c1 = page_copies(b, slot, j)
        c0.wait()
        c1.wait()

    q = q_ref[0]
    rows = q.shape[0]
    ql = q[:, :kv_rank]
    qp = q[:, kv_rank:]
    contract_last = (((1,), (1,)), ((), ()))
    ck = (n_pages * page) // n_chunks
    ms, ls, os_ = [], [], []
    lats, scores = [], []
    for c in range(n_chunks):
        lat = lat_buf[slot, pl.ds(c * ck, ck), :].astype(BF16)
        pet = pet_buf[slot, :, pl.ds(c * ck, ck)].astype(BF16)
        lats.append(lat)
        scores.append(lax.dot_general(ql, lat, contract_last, preferred_element_type=F32)
                      + jnp.dot(qp, pet, preferred_element_type=F32))
    for lat, s in zip(lats, scores):
        m = jnp.max(s, axis=-1, keepdims=True)
        p = jnp.exp(s - m)
        ms.append(m)
        ls.append(jnp.sum(p, axis=-1, keepdims=True))
        os_.append(jnp.dot(p.astype(BF16), lat, preferred_element_type=F32))
    kn = knew_ref[0]
    sn = lax.dot_general(q, kn, contract_last, preferred_element_type=F32)
    t_row = lax.broadcasted_iota(I32, (rows, n_new), 0) % n_new
    t_col = lax.broadcasted_iota(I32, (rows, n_new), 1)
    sn = jnp.where(t_col <= t_row, sn, NEG_BIG)
    mn = jnp.max(sn, axis=-1, keepdims=True)
    pn = jnp.exp(sn - mn)
    ms.append(mn)
    ls.append(jnp.sum(pn, axis=-1, keepdims=True))
    os_.append(jnp.dot(pn.astype(BF16), kn[:, :kv_rank], preferred_element_type=F32))
    m_all = functools.reduce(jnp.maximum, ms)
    l = jnp.zeros_like(m_all)
    o = jnp.zeros_like(os_[0])
    for mi, li, oi in zip(ms, ls, os_):
        w = jnp.exp(mi - m_all)
        l = l + w * li
        o = o + w * oi
    o = o / l
    res = jnp.dot(o.astype(BF16), wuv_ref[...], preferred_element_type=F32)
    col_head = lax.broadcasted_iota(I32, (n_new, N_HEADS * V_HEAD_DIM), 1) // V_HEAD_DIM
    out = jnp.zeros((n_new, N_HEADS * V_HEAD_DIM), F32)
    for hd in range(N_HEADS):
        out = jnp.where(col_head == hd, res[hd * n_new:(hd + 1) * n_new], out)
    o_ref[0] = _rms(out, g_ref[...]).astype(BF16)


def _attn_sample(page_table, q_s, k_new, wuv_flat, g_attn, cache_lat, cache_pe_t, *, layer, n_chunks):
    bs, rows, qd = q_s.shape
    n_new = k_new.shape[1]
    n_pages = page_table.shape[1]
    page = cache_lat.shape[2]
    kv_rank = cache_lat.shape[3]
    past = n_pages * page
    d_out = N_HEADS * V_HEAD_DIM
    grid_spec = pltpu.PrefetchScalarGridSpec(
        num_scalar_prefetch=1,
        grid=(bs,),
        in_specs=[pl.BlockSpec((1, rows, qd), lambda b, pt: (b, 0, 0)),
                  pl.BlockSpec((1, n_new, qd), lambda b, pt: (b, 0, 0)),
                  pl.BlockSpec(wuv_flat.shape, lambda b, pt: (0, 0)),
                  pl.BlockSpec((1, d_out), lambda b, pt: (0, 0)),
                  pl.BlockSpec(memory_space=pl.ANY),
                  pl.BlockSpec(memory_space=pl.ANY)],
        out_specs=pl.BlockSpec((1, n_new, d_out), lambda b, pt: (b, 0, 0)),
        scratch_shapes=[pltpu.VMEM((2, past, kv_rank), F32),
                        pltpu.VMEM((2, QK_ROPE, past), F32),
                        pltpu.SemaphoreType.DMA((2, 2))],
    )
    body = functools.partial(_attn_sample_body, layer=layer, n_pages=n_pages, page=page,
                             kv_rank=kv_rank, n_new=n_new, n_chunks=n_chunks)
    return pl.pallas_call(
        body,
        grid_spec=grid_spec,
        out_shape=jax.ShapeDtypeStruct((bs, n_new, d_out), BF16),
        compiler_params=_cparams(1),
        name="attn_sample",
    )(page_table.reshape(-1), q_s, k_new, wuv_flat, g_attn, cache_lat, cache_pe_t)


def _outproj_body(x_ref, ssm_ref, att_ref, w_ref, gate_ref, g2_ref, sh_ref, sc_ref, wr_ref, br_ref,
                  xo_ref, h2_ref, ridx_ref, rw_ref, *, mod_bcast):
    half = ssm_ref.shape[1]
    merged = (jnp.dot(ssm_ref[...], w_ref[:half], preferred_element_type=F32)
              + jnp.dot(att_ref[...], w_ref[half:], preferred_element_type=F32))
    gate = gate_ref[0] if mod_bcast else gate_ref[...]
    sh = sh_ref[0] if mod_bcast else sh_ref[...]
    sc = sc_ref[0] if mod_bcast else sc_ref[...]
    xn = x_ref[...] + gate * merged
    xo_ref[...] = xn
    h2 = _rms(xn, g2_ref[...]) * (1.0 + sc) + sh
    h2_ref[...] = h2
    logits = jnp.dot(h2.astype(BF16), wr_ref[...], preferred_element_type=F32) + br_ref[...]
    tm = logits.shape[0]
    lane = lax.broadcasted_iota(I32, (tm, LANES), 1)
    lanef = lane.astype(F32)
    lg = jnp.where(lane < MOE_GROUPS, logits, NEG_BIG)
    mg = jnp.max(lg, axis=-1, keepdims=True)
    gsel = jnp.min(jnp.where(lg == mg, lanef, float(LANES)), axis=-1, keepdims=True)
    wg = 1.0 / jnp.sum(jnp.exp(lg - mg), axis=-1, keepdims=True)
    lo = float(MOE_GROUPS) + float(EXPERTS_PER_GROUP) * gsel
    le = jnp.where(lanef >= lo, jnp.where(lanef < lo + float(EXPERTS_PER_GROUP), logits, NEG_BIG), NEG_BIG)
    v1 = jnp.max(le, axis=-1, keepdims=True)
    i1 = jnp.min(jnp.where(le == v1, lanef, float(LANES)), axis=-1, keepdims=True)
    le2 = jnp.where(lanef == i1, NEG_BIG, le)
    v2 = jnp.max(le2, axis=-1, keepdims=True)
    i2 = jnp.min(jnp.where(le2 == v2, lanef, float(LANES)), axis=-1, keepdims=True)
    e2 = jnp.exp(v2 - v1)
    w1 = wg / (1.0 + e2)
    w2 = wg * e2 / (1.0 + e2)
    ex1 = (i1 - float(MOE_GROUPS)).astype(I32)
    ex2 = (i2 - float(MOE_GROUPS)).astype(I32)
    ridx_ref[...] = jnp.where(lane == 0, ex1, jnp.where(lane == 1, ex2, 0))
    rw_ref[...] = jnp.where(lane == 0, w1, jnp.where(lane == 1, w2, 0.0))


def _outproj(x, ssm_n, att_n, w_out, gate1, g2, shift2, scale2, w_route, b_route, *, tm, mod_bcast,
             rows_per_batch):
    r, d = x.shape
    half = ssm_n.shape[1]
    tpb = rows_per_batch // tm if mod_bcast else 1
    if mod_bcast:
        mod_spec = pl.BlockSpec((1, 1, d), lambda i: (i // tpb, 0, 0))
    else:
        mod_spec = pl.BlockSpec((tm, d), lambda i: (0, 0))
    body = functools.partial(_outproj_body, mod_bcast=mod_bcast)
    return pl.pallas_call(
        body,
        grid=(r // tm,),
        in_specs=[pl.BlockSpec((tm, d), lambda i: (i, 0)),
                  pl.BlockSpec((tm, half), lambda i: (i, 0)),
                  pl.BlockSpec((tm, half), lambda i: (i, 0)),
                  pl.BlockSpec(w_out.shape, lambda i: (0, 0)),
                  mod_spec,
                  pl.BlockSpec((1, d), lambda i: (0, 0)),
                  mod_spec, mod_spec,
                  pl.BlockSpec((d, LANES), lambda i: (0, 0)),
                  pl.BlockSpec((1, LANES), lambda i: (0, 0))],
        out_specs=[pl.BlockSpec((tm, d), lambda i: (i, 0)),
                   pl.BlockSpec((tm, d), lambda i: (i, 0)),
                   pl.BlockSpec((tm, LANES), lambda i: (i, 0)),
                   pl.BlockSpec((tm, LANES), lambda i: (i, 0))],
        out_shape=[jax.ShapeDtypeStruct((r, d), F32),
                   jax.ShapeDtypeStruct((r, d), F32),
                   jax.ShapeDtypeStruct((r, LANES), I32),
                   jax.ShapeDtypeStruct((r, LANES), F32)],
        compiler_params=_cparams(1),
        name="out_proj_router",
    )(x, ssm_n, att_n, w_out, gate1, g2, shift2, scale2, w_route, b_route)


def _route_meta(e_pairs, tile):
    npairs = e_pairs.shape[0]
    n_tiles = -(-npairs // tile) + N_EXPERTS
    oh = (e_pairs[:, None] == jnp.arange(N_EXPERTS, dtype=I32)[None, :]).astype(I32)
    cs = jnp.cumsum(oh, axis=0)
    rank = jnp.sum(cs * oh, axis=1) - 1
    counts = cs[-1]
    padded = ((counts + tile - 1) // tile) * tile
    ends = jnp.cumsum(padded)
    starts = ends - padded
    pos = jnp.sum(oh * starts[None, :], axis=1) + rank
    n_used = ends[-1] // tile
    tile_start = jnp.arange(n_tiles, dtype=I32) * tile
    tile_e = jnp.sum((tile_start[:, None] >= ends[None, :]).astype(I32), axis=1)
    last_e = jnp.sum((jnp.maximum(n_used - 1, 0) * tile >= ends).astype(I32))
    tile_e = jnp.minimum(tile_e, last_e).astype(I32)
    return pos.astype(I32), tile_e, n_used.astype(I32).reshape(1), n_tiles


def _dispatch_body(pos_ref, h_ref, xs_in_ref, xs_ref, sem, *, tm):
    del xs_in_ref
    base = pl.program_id(0) * tm

    def issue(r, c):
        for k in range(TOP_K):
            p = pos_ref[TOP_K * (base + r) + k]
            pltpu.make_async_copy(h_ref.at[pl.ds(r, 1)], xs_ref.at[pl.ds(p, 1)], sem).start()
        return c

    lax.fori_loop(0, tm, issue, 0, unroll=8)
    for k in range(TOP_K):
        pltpu.make_async_copy(h_ref, xs_ref.at[pl.ds(0, tm)], sem).wait()


def _dispatch(pos, h2, xs, *, tm):
    r, d = h2.shape
    grid_spec = pltpu.PrefetchScalarGridSpec(
        num_scalar_prefetch=1,
        grid=(r // tm,),
        in_specs=[pl.BlockSpec((tm, d), lambda i, p: (i, 0)),
                  pl.BlockSpec(memory_space=pl.ANY)],
        out_specs=pl.BlockSpec(memory_space=pl.ANY),
        scratch_shapes=[pltpu.SemaphoreType.DMA(())],
    )
    return pl.pallas_call(
        functools.partial(_dispatch_body, tm=tm),
        grid_spec=grid_spec,
        out_shape=jax.ShapeDtypeStruct(xs.shape, xs.dtype),
        input_output_aliases={2: 0},
        compiler_params=_cparams(1),
        name="moe_dispatch",
    )(pos, h2, xs)


def _moe_gemm_body(te_ref, nu_ref, x_ref, wg_ref, wu_ref, wd_ref, y_ref, wg_s, wu_s, wd_s):
    i = pl.program_id(0)
    e = te_ref[i]
    prev = te_ref[jnp.maximum(i - 1, 0)]

    @pl.when(jnp.logical_or(i == 0, e != prev))
    def _():
        wg_s[...] = wg_ref[0].astype(BF16)
        wu_s[...] = wu_ref[0].astype(BF16)
        wd_s[...] = wd_ref[0].astype(BF16)

    @pl.when(i < nu_ref[0])
    def _():
        xb = x_ref[...].astype(BF16)
        a = jnp.dot(xb, wg_s[...], preferred_element_type=F32)
        b = jnp.dot(xb, wu_s[...], preferred_element_type=F32)
        act = (a * jax.nn.sigmoid(a) * b).astype(BF16)
        y_ref[...] = jnp.dot(act, wd_s[...], preferred_element_type=F32)

    @pl.when(i >= nu_ref[0])
    def _():
        y_ref[...] = jnp.zeros_like(y_ref)


def _moe_gemm(tile_e, n_used, xs, w_g, w_u, w_d, *, layer, tile):
    rows, d = xs.shape
    f = w_g.shape[-1]
    ne = w_g.shape[1]
    n_tiles = rows // tile
    grid_spec = pltpu.PrefetchScalarGridSpec(
        num_scalar_prefetch=2,
        grid=(n_tiles,),
        in_specs=[pl.BlockSpec((tile, d), lambda i, te, nu: (jnp.minimum(i, nu[0] - 1), 0)),
                  pl.BlockSpec((1, d, f), lambda i, te, nu: (layer * ne + te[i], 0, 0)),
                  pl.BlockSpec((1, d, f), lambda i, te, nu: (layer * ne + te[i], 0, 0)),
                  pl.BlockSpec((1, f, d), lambda i, te, nu: (layer * ne + te[i], 0, 0))],
        out_specs=pl.BlockSpec((tile, d), lambda i, te, nu: (i, 0)),
        scratch_shapes=[pltpu.VMEM((d, f), BF16), pltpu.VMEM((d, f), BF16), pltpu.VMEM((f, d), BF16)],
    )
    depth = w_g.shape[0]
    return pl.pallas_call(
        _moe_gemm_body,
        grid_spec=grid_spec,
        out_shape=jax.ShapeDtypeStruct((rows, d), F32),
        compiler_params=_cparams(1),
        name="moe_gemm",
    )(tile_e, n_used, xs, w_g.reshape(depth * ne, d, f), w_u.reshape(depth * ne, d, f),
      w_d.reshape(depth * ne, f, d))


def _combine_body(pos_ref, x_ref, rw_ref, gate_ref, y_hbm, o_ref, ybuf, sem, *, tm, mod_bcast):
    base = pl.program_id(0) * tm

    def issue(r, c):
        for k in range(TOP_K):
            p = pos_ref[TOP_K * (base + r) + k]
            pltpu.make_async_copy(y_hbm.at[pl.ds(p, 1)], ybuf.at[k, pl.ds(r, 1)], sem).start()
        return c

    lax.fori_loop(0, tm, issue, 0, unroll=8)
    for k in range(TOP_K):
        pltpu.make_async_copy(y_hbm.at[pl.ds(0, tm)], ybuf.at[k], sem).wait()
    gate = gate_ref[0] if mod_bcast else gate_ref[...]
    rw = rw_ref[...]
    moe = rw[:, 0:1] * ybuf[0] + rw[:, 1:2] * ybuf[1]
    o_ref[...] = x_ref[...] + gate * moe


def _combine(pos, x, rw, gate2, y, *, tm, mod_bcast, rows_per_batch):
    r, d = x.shape
    tpb = rows_per_batch // tm if mod_bcast else 1
    if mod_bcast:
        mod_spec = pl.BlockSpec((1, 1, d), lambda i, p: (i // tpb, 0, 0))
    else:
        mod_spec = pl.BlockSpec((tm, d), lambda i, p: (0, 0))
    grid_spec = pltpu.PrefetchScalarGridSpec(
        num_scalar_prefetch=1,
        grid=(r // tm,),
        in_specs=[pl.BlockSpec((tm, d), lambda i, p: (i, 0)),
                  pl.BlockSpec((tm, LANES), lambda i, p: (i, 0)),
                  mod_spec,
                  pl.BlockSpec(memory_space=pl.ANY)],
        out_specs=pl.BlockSpec((tm, d), lambda i, p: (i, 0)),
        scratch_shapes=[pltpu.VMEM((TOP_K, tm, d), F32), pltpu.SemaphoreType.DMA(())],
    )
    return pl.pallas_call(
        functools.partial(_combine_body, tm=tm, mod_bcast=mod_bcast),
        grid_spec=grid_spec,
        out_shape=jax.ShapeDtypeStruct((r, d), F32),
        compiler_params=_cparams(1),
        name="moe_combine",
    )(pos, x, rw, gate2, y)


def _final_norm_body(x_ref, g_ref, o_ref):
    o_ref[...] = _rms(x_ref[...], g_ref[...])


def _final_norm(x, g, *, tm):
    r, d = x.shape
    return pl.pallas_call(
        _final_norm_body,
        grid=(r // tm,),
        in_specs=[pl.BlockSpec((tm, d), lambda i: (i, 0)), pl.BlockSpec((1, d), lambda i: (0, 0))],
        out_specs=pl.BlockSpec((tm, d), lambda i: (i, 0)),
        out_shape=jax.ShapeDtypeStruct((r, d), F32),
        compiler_params=_cparams(1),
        name="final_norm",
    )(x, g)


def _rope_tables(pos):
    half = QK_ROPE // 2
    inv = ROPE_THETA ** (-jnp.arange(half, dtype=F32) * (2.0 / QK_ROPE))
    ang = pos.astype(F32)[:, None] * inv[None, :]
    c, s = jnp.cos(ang), jnp.sin(ang)
    reps = LANES // half
    cos4 = jnp.tile(c, (1, reps))
    sin4 = jnp.tile(jnp.concatenate([-s, s], axis=-1), (1, reps // 2))
    return cos4, sin4


def _pick_tile(n, pref):
    t = min(pref, n)
    while n % t:
        t //= 2
    return t


def kernel(x_prompt, x_sample, cache_kv_latent, cache_k_rope, state_ssm_re, state_ssm_im, page_table,
           c_prompt, c_sample, norm_mix_g, norm_ffn_g, w_ada, b_ada, w_in, ssm_lam_re, ssm_lam_im,
           ssm_log_dt, ssm_b_re, ssm_b_im, ssm_c_re, ssm_c_im, ssm_d, w_glu, b_glu, kv_norm_g, w_uk,
           w_uv, g_ssm_out, g_attn_out, w_out, w_route_group, b_route_group, w_route_expert,
           b_route_expert, w_exp_gate, w_exp_up, w_exp_down, final_norm_g):
    bp, tp, d = x_prompt.shape
    bs, ts_, _ = x_sample.shape
    depth = w_in.shape[0]
    kv_rank = w_uk.shape[1]
    ssm_w = d // 2
    n_groups = ssm_w // SSM_CH
    nst = n_groups * SSM_STATE
    n_pages, page = page_table.shape[1], cache_kv_latent.shape[2]
    past_len = n_pages * page
    rp = bp * tp
    rs = bs * ts_

    q_cols = w_in[..., ssm_w:ssm_w + N_HEADS * QK_DIM].reshape(depth, d, N_HEADS, QK_DIM)
    off = ssm_w + N_HEADS * QK_DIM
    w_in_p = jnp.concatenate([
        w_in[..., :ssm_w],
        q_cols[..., :QK_NOPE].reshape(depth, d, N_HEADS * QK_NOPE),
        q_cols[..., QK_NOPE:].reshape(depth, d, N_HEADS * QK_ROPE),
        w_in[..., off:off + kv_rank],
        w_in[..., off + kv_rank:], w_in[..., off + kv_rank:]], axis=-1).astype(BF16)
    wuk_t = jnp.transpose(w_uk, (0, 2, 3, 1)).astype(BF16)
    wuv_tt = jnp.transpose(w_uv, (0, 2, 3, 1)).astype(BF16)
    wuv_flat = w_uv.reshape(depth, kv_rank, N_HEADS * V_HEAD_DIM).astype(BF16)
    w_glu_b = w_glu.astype(BF16)
    w_out_b = w_out.astype(BF16)
    n_route = MOE_GROUPS + N_EXPERTS
    w_route = jnp.concatenate([w_route_group, w_route_expert.reshape(depth, d, N_EXPERTS),
                               jnp.zeros((depth, d, LANES - n_route), F32)], axis=-1).astype(BF16)
    b_route = jnp.concatenate([b_route_group, b_route_expert.reshape(depth, N_EXPERTS),
                               jnp.zeros((depth, LANES - n_route), F32)], axis=-1).reshape(depth, 1, LANES)
    tabs_all = _ssm_tables(ssm_lam_re, ssm_lam_im, ssm_log_dt, ssm_b_re, ssm_b_im, ssm_c_re, ssm_c_im,
                           ssm_d, SSM_CHUNK)
    cos_p, sin_p = _rope_tables(jnp.arange(tp))
    cos_s, sin_s = _rope_tables(past_len + jnp.arange(ts_))
    cos_s = cos_s.reshape(ts_, 1, LANES)
    sin_s = sin_s.reshape(ts_, 1, LANES)

    pad = (-(bs + bp)) % 8
    c_all = jnp.concatenate([c_sample, c_prompt, jnp.zeros((pad, d), F32)], axis=0)
    mod = _ada(c_all, w_ada, b_ada)

    tm_p = _pick_tile(tp, 256)
    ts_ssm = _pick_tile(tp, 512)
    tq = _pick_tile(tp, 512)
    tk = _pick_tile(tp, 512)
    moe_tile = 256
    tm_final = _pick_tile(rp, 512)

    xp = x_prompt.reshape(rp, d)
    xs = jnp.transpose(x_sample, (1, 0, 2)).reshape(rs, d)
    cache_pe_t = jnp.transpose(cache_k_rope, (0, 1, 3, 2))
    h0re = state_ssm_re.reshape(depth, bs, nst)
    h0im = state_ssm_im.reshape(depth, bs, nst)

    lat_p, pe_p, sre_p, sim_p = [], [], [], []
    lat_s, pe_s, sre_s, sim_s = [], [], [], []
    for l in range(depth):
        mod_s = [mod[l, :bs, i * d:(i + 1) * d] for i in range(6)]
        mod_p = [mod[l, bs:bs + bp, i * d:(i + 1) * d].reshape(bp, 1, d) for i in range(6)]
        tabs = {k: (v if k == "tri" else v[l]) for k, v in tabs_all.items()}
        g_mix = norm_mix_g[l].reshape(1, d)
        g_ffn = norm_ffn_g[l].reshape(1, d)
        kvg = kv_norm_g[l].reshape(1, kv_rank)
        bglu = b_glu[l].reshape(1, ssm_w)
        gso = g_ssm_out[l].reshape(1, ssm_w)
        gao = g_attn_out[l].reshape(1, N_HEADS * V_HEAD_DIM)

        u_p, q_t, kcat_p, ckv_p, kpe_p, v_t = _inproj(
            xp, g_mix, mod_p[0], mod_p[1], w_in_p[l], wuk_t[l], kvg, cos_p, sin_p,
            tm=tm_p, mod_bcast=True, rope_bcast=False, rows_per_batch=tp, transposed_q=True)
        ssm_p, s_re, s_im = _ssm_prompt(u_p.reshape(bp, tp, ssm_w), tabs, w_glu_b[l], bglu, gso, ts=ts_ssm)
        att_p = _attn_prompt(kcat_p, q_t, v_t, wuv_tt[l], gao, batch=bp, seq=tp, tq=tq, tk=tk,
                             heads_per_group=N_HEADS)
        xp, h2_p, ridx_p, rw_p = _outproj(
            xp, ssm_p.reshape(rp, ssm_w), att_p, w_out_b[l], mod_p[2], g_ffn, mod_p[3], mod_p[4],
            w_route[l], b_route[l], tm=tm_p, mod_bcast=True, rows_per_batch=tp)
        lat_p.append(ckv_p.reshape(bp, tp, kv_rank))
        pe_p.append(kpe_p.reshape(bp, tp, QK_ROPE))
        sre_p.append(s_re.reshape(bp, n_groups, SSM_STATE))
        sim_p.append(s_im.reshape(bp, n_groups, SSM_STATE))

        u_s, q_s, kcat_s, ckv_s, kpe_s = _inproj(
            xs, g_mix, mod_s[0], mod_s[1], w_in_p[l], wuk_t[l], kvg, cos_s, sin_s,
            tm=bs, mod_bcast=False, rope_bcast=True, rows_per_batch=bs, transposed_q=False)
        ssm_s, s_re, s_im = _ssm_sample(u_s, tabs, h0re[l], h0im[l], w_glu_b[l], bglu, gso)
        q_sb = jnp.transpose(q_s.reshape(N_HEADS, ts_, bs, kv_rank + QK_ROPE), (2, 0, 1, 3))
        q_sb = q_sb.reshape(bs, N_HEADS * ts_, kv_rank + QK_ROPE)
        k_new = jnp.transpose(kcat_s.reshape(ts_, bs, kv_rank + QK_ROPE), (1, 0, 2))
        att_s = _attn_sample(page_table, q_sb, k_new, wuv_flat[l], gao, cache_kv_latent, cache_pe_t,
                             layer=l, n_chunks=4)
        att_s = jnp.transpose(att_s, (1, 0, 2)).reshape(rs, N_HEADS * V_HEAD_DIM)
        xs, h2_s, ridx_s, rw_s = _outproj(
            xs, ssm_s, att_s, w_out_b[l], mod_s[2], g_ffn, mod_s[3], mod_s[4],
            w_route[l], b_route[l], tm=bs, mod_bcast=False, rows_per_batch=bs)
        lat_s.append(jnp.transpose(ckv_s.reshape(ts_, bs, kv_rank), (1, 0, 2)))
        pe_s.append(jnp.transpose(kpe_s.reshape(ts_, bs, QK_ROPE), (1, 0, 2)))
        sre_s.append(s_re.reshape(bs, n_groups, SSM_STATE))
        sim_s.append(s_im.reshape(bs, n_groups, SSM_STATE))

        e_pairs = jnp.concatenate([ridx_p[:, :TOP_K].reshape(-1), ridx_s[:, :TOP_K].reshape(-1)])
        pos, tile_e, n_used, n_tiles = _route_meta(e_pairs, moe_tile)
        pos_p, pos_s = pos[:TOP_K * rp], pos[TOP_K * rp:]
        xsrt = jnp.zeros((n_tiles * moe_tile, d), F32)
        xsrt = _dispatch(pos_p, h2_p, xsrt, tm=tm_p)
        xsrt = _dispatch(pos_s, h2_s, xsrt, tm=bs)
        y = _moe_gemm(tile_e, n_used, xsrt, w_exp_gate, w_exp_up, w_exp_down, layer=l, tile=moe_tile)
        xp = _combine(pos_p, xp, rw_p, mod_p[5], y, tm=tm_p, mod_bcast=True, rows_per_batch=tp)
        xs = _combine(pos_s, xs, rw_s, mod_s[5], y, tm=bs, mod_bcast=False, rows_per_batch=bs)

    fg = final_norm_g.reshape(1, d)
    y_prompt = _final_norm(xp, fg, tm=tm_final).reshape(bp, tp, d)
    y_sample = jnp.transpose(_final_norm(xs, fg, tm=bs).reshape(ts_, bs, d), (1, 0, 2))
    return (y_prompt, y_sample,
            jnp.stack(lat_p), jnp.stack(pe_p), jnp.stack(sre_p), jnp.stack(sim_p),
            jnp.stack(lat_s), jnp.stack(pe_s), jnp.stack(sre_s), jnp.stack(sim_s))
```

```python
import functools
import math

import numpy as np
import jax
import jax.numpy as jnp
from jax import lax
from jax.experimental import pallas as pl
from jax.experimental.pallas import tpu as pltpu

F32 = jnp.float32
BF16 = jnp.bfloat16
I32 = jnp.int32

SSM_CH = 16
SSM_STATE = 64
N_HEADS = 8
QK_NOPE = 128
QK_ROPE = 64
QK_DIM = QK_NOPE + QK_ROPE
V_HEAD_DIM = 128
ROPE_THETA = 10000.0
EPS = 1e-6
SM_SCALE = QK_DIM ** -0.5
MOE_GROUPS = 4
EXPERTS_PER_GROUP = 4
N_EXPERTS = MOE_GROUPS * EXPERTS_PER_GROUP
TOP_K = 2

LANES = 128
MXU_DIM = 256
VMEM_LIMIT = 56 * 1024 * 1024

GROUPS_PER_BLOCK = MXU_DIM // SSM_CH
SSM_CHUNK = 64
SAMPLE_PREFETCH = 2
NEG_BIG = -1e30


def _cparams(n_axes):
    return pltpu.CompilerParams(dimension_semantics=("arbitrary",) * n_axes,
                                vmem_limit_bytes=VMEM_LIMIT)


def _rms(x, g):
    return x * lax.rsqrt(jnp.mean(x * x, axis=-1, keepdims=True) + EPS) * g


def _ada_body(c_ref, w_ref, b_ref, o_ref):
    c = c_ref[...]
    s = (c * jax.nn.sigmoid(c)).astype(BF16)
    o_ref[0] = jnp.dot(s, w_ref[0].astype(BF16), preferred_element_type=F32) + b_ref[0]


def _ada(c_all, w_ada, b_ada):
    depth, d, n6 = w_ada.shape
    r = c_all.shape[0]
    tn = 1024
    return pl.pallas_call(
        _ada_body,
        grid=(depth, n6 // tn),
        in_specs=[pl.BlockSpec((r, d), lambda l, j: (0, 0)),
                  pl.BlockSpec((1, d, tn), lambda l, j: (l, 0, j)),
                  pl.BlockSpec((1, 1, tn), lambda l, j: (l, 0, j))],
        out_specs=pl.BlockSpec((1, r, tn), lambda l, j: (l, 0, j)),
        out_shape=jax.ShapeDtypeStruct((depth, r, n6), F32),
        compiler_params=_cparams(2),
        name="ada_ln",
    )(c_all, w_ada, b_ada.reshape(depth, 1, n6))


def _rope_pairs(p, cos4, sin4, first_half):
    swapped = jnp.where(first_half, pltpu.roll(p, 96, 1), pltpu.roll(p, 32, 1))
    return p * cos4 + swapped * sin4


def _inproj_body(x_ref, g_ref, sh_ref, sc_ref, w_ref, wuk_ref, kvg_ref, cos_ref, sin_ref,
                 u_ref, q_ref, kcat_ref, ckv_ref, kpe_ref, *maybe_vt_ref,
                 mod_bcast, rope_bcast, ssm_w, kv_rank, transposed_q):
    x = x_ref[...]
    tm = x.shape[0]
    sh = sh_ref[0] if mod_bcast else sh_ref[...]
    sc = sc_ref[0] if mod_bcast else sc_ref[...]
    h = _rms(x, g_ref[...]) * (1.0 + sc) + sh
    z = jnp.dot(h.astype(BF16), w_ref[...], preferred_element_type=F32)
    u_ref[...] = z[:, :ssm_w]
    cos4 = cos_ref[0] if rope_bcast else cos_ref[...]
    sin4 = sin_ref[0] if rope_bcast else sin_ref[...]
    lane = lax.broadcasted_iota(I32, (tm, LANES), 1)
    first_half = (lane % QK_ROPE) < (QK_ROPE // 2)
    off_kv = ssm_w + N_HEADS * QK_DIM
    off_kr = off_kv + kv_rank
    low_half = lane < QK_ROPE
    nopes = []
    for hp in range(N_HEADS // 2):
        base = ssm_w + hp * 2 * QK_DIM
        t0 = z[:, base:base + LANES]
        t1 = z[:, base + LANES:base + 2 * LANES]
        t2 = z[:, base + 2 * LANES:base + 3 * LANES]
        nopes.append(t0)
        nopes.append(jnp.where(low_half, t2, t1))
        pr = jnp.where(low_half, t1, t2)
        rp = _rope_pairs(pr, cos4, sin4, first_half) * SM_SCALE
        if transposed_q:
            rpt = rp.T.astype(BF16)
            q_ref[2 * hp, kv_rank:kv_rank + QK_ROPE, :] = rpt[:QK_ROPE]
            q_ref[2 * hp + 1, kv_rank:kv_rank + QK_ROPE, :] = rpt[QK_ROPE:]
        else:
            rp = rp.astype(BF16)
            q_ref[2 * hp, :, kv_rank:kv_rank + QK_ROPE] = rp[:, :QK_ROPE]
            q_ref[2 * hp + 1, :, kv_rank:kv_rank + QK_ROPE] = rp[:, QK_ROPE:]
    for hd in range(N_HEADS):
        ql = jnp.dot(nopes[hd].astype(BF16), wuk_ref[hd], preferred_element_type=F32) * SM_SCALE
        if transposed_q:
            q_ref[hd, :kv_rank, :] = ql.T.astype(BF16)
        else:
            q_ref[hd, :, :kv_rank] = ql.astype(BF16)
    ckv = _rms(z[:, off_kv:off_kr], kvg_ref[...])
    ckv_ref[...] = ckv
    kr = z[:, off_kr:off_kr + QK_ROPE]
    kk = _rope_pairs(jnp.concatenate([kr, kr], axis=1), cos4, sin4, first_half)
    kpe_ref[...] = kk[:, :QK_ROPE]
    kcat_ref[:, :kv_rank] = ckv.astype(BF16)
    kcat_ref[:, kv_rank:kv_rank + QK_ROPE] = kk[:, :QK_ROPE].astype(BF16)
    if transposed_q:
        maybe_vt_ref[0][...] = ckv.T.astype(BF16)


def _inproj(x, g, shift, scale, w, wuk_t, kvg, cos4, sin4, *, tm, mod_bcast, rope_bcast, rows_per_batch,
            transposed_q):
    r, d = x.shape
    ncols = w.shape[1]
    kv_rank = wuk_t.shape[2]
    ssm_w = d // 2
    qd = kv_rank + QK_ROPE
    tpb = rows_per_batch // tm if mod_bcast else 1
    if mod_bcast:
        mod_spec = pl.BlockSpec((1, 1, d), lambda i: (i // tpb, 0, 0))
    else:
        mod_spec = pl.BlockSpec((tm, d), lambda i: (0, 0))
    if rope_bcast:
        rope_spec = pl.BlockSpec((1, 1, LANES), lambda i: (i, 0, 0))
    else:
        rope_spec = pl.BlockSpec((tm, LANES), lambda i: (i % tpb, 0))
    if transposed_q:
        q_spec = pl.BlockSpec((N_HEADS, qd, tm), lambda i: (0, 0, i))
        q_shape = jax.ShapeDtypeStruct((N_HEADS, qd, r), BF16)
    else:
        q_spec = pl.BlockSpec((N_HEADS, tm, qd), lambda i: (0, i, 0))
        q_shape = jax.ShapeDtypeStruct((N_HEADS, r, qd), BF16)
    out_specs = [pl.BlockSpec((tm, ssm_w), lambda i: (i, 0)),
                 q_spec,
                 pl.BlockSpec((tm, qd), lambda i: (i, 0)),
                 pl.BlockSpec((tm, kv_rank), lambda i: (i, 0)),
                 pl.BlockSpec((tm, QK_ROPE), lambda i: (i, 0))]
    out_shape = [jax.ShapeDtypeStruct((r, ssm_w), F32),
                 q_shape,
                 jax.ShapeDtypeStruct((r, qd), BF16),
                 jax.ShapeDtypeStruct((r, kv_rank), F32),
                 jax.ShapeDtypeStruct((r, QK_ROPE), F32)]
    if transposed_q:
        out_specs.append(pl.BlockSpec((kv_rank, tm), lambda i: (0, i)))
        out_shape.append(jax.ShapeDtypeStruct((kv_rank, r), BF16))
    body = functools.partial(_inproj_body, mod_bcast=mod_bcast, rope_bcast=rope_bcast,
                             ssm_w=ssm_w, kv_rank=kv_rank, transposed_q=transposed_q)
    return pl.pallas_call(
        body,
        grid=(r // tm,),
        in_specs=[pl.BlockSpec((tm, d), lambda i: (i, 0)),
                  pl.BlockSpec((1, d), lambda i: (0, 0)),
                  mod_spec, mod_spec,
                  pl.BlockSpec((d, ncols), lambda i: (0, 0)),
                  pl.BlockSpec(wuk_t.shape, lambda i: (0, 0, 0)),
                  pl.BlockSpec((1, kv_rank), lambda i: (0, 0)),
                  rope_spec, rope_spec],
        out_specs=out_specs,
        out_shape=out_shape,
        compiler_params=_cparams(1),
        name="in_proj_mla_prep",
    )(x, g, shift, scale, w, wuk_t, kvg, cos4, sin4)


def _glu_norm(y, wglu_ref, bglu_ref, gout_ref):
    g = jax.nn.gelu(y)
    gate = jnp.dot(g.astype(BF16), wglu_ref[...], preferred_element_type=F32) + bglu_ref[...]
    return _rms(g * jax.nn.sigmoid(gate), gout_ref[...])


def _ssm_prompt_body(u_ref, bbd_ref, cbd_ref, tri_ref, pre_re_ref, pre_im_ref, post_re_ref, post_im_ref,
                     a_re_ref, a_im_ref, d_ref, wglu_ref, bglu_ref, gout_ref,
                     y_ref, sre_ref, sim_ref, bu_scr, h_scr, hprev_scr, yacc_scr, *, n_blocks, chunk):
    tc = pl.program_id(1)
    ts = u_ref.shape[1]
    sl = GROUPS_PER_BLOCK * SSM_STATE

    @pl.when(tc == 0)
    def _():
        hprev_scr[...] = jnp.zeros_like(hprev_scr)

    u = u_ref[0]
    ub = u.astype(BF16)
    tri = tri_ref[...]
    for j in range(n_blocks):
        cols = slice(j * MXU_DIM, (j + 1) * MXU_DIM)
        lanes = slice(j * sl, (j + 1) * sl)
        bu_scr[...] = jnp.dot(ub[:, cols], bbd_ref[j], preferred_element_type=F32)
        a_re = a_re_ref[:, lanes]
        a_im = a_im_ref[:, lanes]

        def sub(s, carry):
            hr, hi = carry
            r0 = s * chunk
            b_re = bu_scr[pl.ds(r0, chunk), 0:sl]
            b_im = bu_scr[pl.ds(r0, chunk), sl:2 * sl]
            p_re = pre_re_ref[:, lanes]
            p_im = pre_im_ref[:, lanes]
            x_re = (p_re * b_re - p_im * b_im).astype(BF16)
            x_im = (p_re * b_im + p_im * b_re).astype(BF16)
            z_re = jnp.dot(tri, x_re, preferred_element_type=F32) + (a_re * hr - a_im * hi)
            z_im = jnp.dot(tri, x_im, preferred_element_type=F32) + (a_re * hi + a_im * hr)
            q_re = post_re_ref[:, lanes]
            q_im = post_im_ref[:, lanes]
            h_re = q_re * z_re - q_im * z_im
            h_im = q_re * z_im + q_im * z_re
            h_scr[pl.ds(r0, chunk), 0:sl] = h_re.astype(BF16)
            h_scr[pl.ds(r0, chunk), sl:2 * sl] = h_im.astype(BF16)
            return h_re[chunk - 1:chunk], h_im[chunk - 1:chunk]

        carry = (hprev_scr[2 * j:2 * j + 1], hprev_scr[2 * j + 1:2 * j + 2])
        for s in range(ts // chunk):
            carry = sub(s, carry)
        hr, hi = carry
        hprev_scr[2 * j:2 * j + 1] = hr
        hprev_scr[2 * j + 1:2 * j + 2] = hi
        sre_ref[0, :, lanes] = hr
        sim_ref[0, :, lanes] = hi
        yacc_scr[:, cols] = (jnp.dot(h_scr[...], cbd_ref[j], preferred_element_type=F32)
                             + d_ref[:, cols] * u[:, cols])
    y_ref[0] = _glu_norm(yacc_scr[...], wglu_ref, bglu_ref, gout_ref).astype(BF16)


def _ssm_prompt(u, tabs, wglu, bglu, gout, *, ts):
    b, t, w = u.shape
    n_blocks = w // MXU_DIM
    sl2 = 2 * GROUPS_PER_BLOCK * SSM_STATE
    nst = (w // SSM_CH) * SSM_STATE
    chunk = SSM_CHUNK
    full2 = lambda shape: pl.BlockSpec(shape, lambda i, j: (0,) * len(shape))
    body = functools.partial(_ssm_prompt_body, n_blocks=n_blocks, chunk=chunk)
    return pl.pallas_call(
        body,
        grid=(b, t // ts),
        in_specs=[pl.BlockSpec((1, ts, w), lambda i, j: (i, j, 0)),
                  full2((n_blocks, MXU_DIM, sl2)), full2((n_blocks, sl2, MXU_DIM)),
                  full2((chunk, chunk)),
                  full2((chunk, nst)), full2((chunk, nst)), full2((chunk, nst)), full2((chunk, nst)),
                  full2((1, nst)), full2((1, nst)), full2((1, w)),
                  full2((w, w)), full2((1, w)), full2((1, w))],
        out_specs=[pl.BlockSpec((1, ts, w), lambda i, j: (i, j, 0)),
                   pl.BlockSpec((1, 1, nst), lambda i, j: (i, 0, 0)),
                   pl.BlockSpec((1, 1, nst), lambda i, j: (i, 0, 0))],
        out_shape=[jax.ShapeDtypeStruct((b, t, w), BF16),
                   jax.ShapeDtypeStruct((b, 1, nst), F32),
                   jax.ShapeDtypeStruct((b, 1, nst), F32)],
        scratch_shapes=[pltpu.VMEM((ts, sl2), F32), pltpu.VMEM((ts, sl2), BF16),
                        pltpu.VMEM((2 * n_blocks, sl2 // 2), F32), pltpu.VMEM((ts, w), F32)],
        compiler_params=_cparams(2),
        name="ssm_prompt",
    )(u, tabs["bbd"], tabs["cbd"], tabs["tri"], tabs["pre_re"], tabs["pre_im"], tabs["post_re"],
      tabs["post_im"], tabs["a_re"], tabs["a_im"], tabs["d"], wglu, bglu, gout)


def _ssm_sample_body(u_ref, bbd_ref, cbd_ref, a_re_ref, a_im_ref, d_ref, h0re_ref, h0im_ref,
                     wglu_ref, bglu_ref, gout_ref, y_ref, sre_ref, sim_ref,
                     bu_scr, h_scr, yacc_scr, *, n_blocks, n_steps):
    sl = GROUPS_PER_BLOCK * SSM_STATE
    bs = h0re_ref.shape[0]
    u = u_ref[...]
    ub = u.astype(BF16)
    for j in range(n_blocks):
        cols = slice(j * MXU_DIM, (j + 1) * MXU_DIM)
        lanes = slice(j * sl, (j + 1) * sl)
        bu_scr[...] = jnp.dot(ub[:, cols], bbd_ref[j], preferred_element_type=F32)
        a_re = a_re_ref[:, lanes]
        a_im = a_im_ref[:, lanes]
        hr = h0re_ref[:, lanes]
        hi = h0im_ref[:, lanes]
        for t in range(n_steps):
            rows = slice(t * bs, (t + 1) * bs)
            nr = (a_re * hr - a_im * hi) + bu_scr[rows, 0:sl]
            ni = (a_re * hi + a_im * hr) + bu_scr[rows, sl:2 * sl]
            hr, hi = nr, ni
            h_scr[rows, 0:sl] = hr.astype(BF16)
            h_scr[rows, sl:2 * sl] = hi.astype(BF16)
        sre_ref[:, lanes] = hr
        sim_ref[:, lanes] = hi
        yacc_scr[:, cols] = (jnp.dot(h_scr[...], cbd_ref[j], preferred_element_type=F32)
                             + d_ref[:, cols] * u[:, cols])
    y_ref[...] = _glu_norm(yacc_scr[...], wglu_ref, bglu_ref, gout_ref).astype(BF16)


def _ssm_sample(u, tabs, h0re, h0im, wglu, bglu, gout):
    r, w = u.shape
    bs, nst = h0re.shape
    n_blocks = w // MXU_DIM
    sl2 = 2 * GROUPS_PER_BLOCK * SSM_STATE
    body = functools.partial(_ssm_sample_body, n_blocks=n_blocks, n_steps=r // bs)
    return pl.pallas_call(
        body,
        out_shape=[jax.ShapeDtypeStruct((r, w), BF16),
                   jax.ShapeDtypeStruct((bs, nst), F32),
                   jax.ShapeDtypeStruct((bs, nst), F32)],
        scratch_shapes=[pltpu.VMEM((r, sl2), F32), pltpu.VMEM((r, sl2), BF16), pltpu.VMEM((r, w), F32)],
        compiler_params=pltpu.CompilerParams(vmem_limit_bytes=VMEM_LIMIT),
        name="ssm_sample",
    )(u, tabs["bbd"], tabs["cbd"], tabs["a_re"], tabs["a_im"], tabs["d"], h0re, h0im, wglu, bglu, gout)


def _ssm_tables(lam_re, lam_im, log_dt, b_re, b_im, c_re, c_im, d_skip, chunk):
    depth, g, n = lam_re.shape
    p = b_re.shape[-1]
    gb = GROUPS_PER_BLOCK
    nb = g // gb
    lr = lam_re.astype(F32)
    li = lam_im.astype(F32)
    dt = jnp.exp(log_dt.astype(F32))[..., None]
    mag = jnp.exp(lr * dt)
    ab_re = mag * jnp.cos(li * dt)
    ab_im = mag * jnp.sin(li * dt)
    den = lr * lr + li * li
    f_re = ((ab_re - 1.0) * lr + ab_im * li) / den
    f_im = (ab_im * lr - (ab_re - 1.0) * li) / den
    bb_re = f_re[..., None] * b_re - f_im[..., None] * b_im
    bb_im = f_re[..., None] * b_im + f_im[..., None] * b_re
    ks = np.arange(chunk)
    pw_re = jnp.ones((chunk, depth, g, n), F32)
    pw_im = jnp.zeros((chunk, depth, g, n), F32)
    sq_re, sq_im = ab_re, ab_im
    for bit in range(max(1, int(chunk - 1).bit_length())):
        sel = jnp.asarray(((ks >> bit) & 1).astype(np.float32))[:, None, None, None]
        m_re = sel * sq_re + (1.0 - sel)
        m_im = sel * sq_im
        pw_re, pw_im = pw_re * m_re - pw_im * m_im, pw_re * m_im + pw_im * m_re
        sq_re, sq_im = sq_re * sq_re - sq_im * sq_im, 2.0 * sq_re * sq_im
    nrm = pw_re * pw_re + pw_im * pw_im
    inv_re = pw_re / nrm
    inv_im = -pw_im / nrm
    flat = lambda a: jnp.moveaxis(a, 0, 1).reshape(depth, chunk, g * n)
    eye = jnp.eye(gb, dtype=F32)

    def bdiag_in(bb):
        bbj = bb.reshape(depth, nb, gb, n, p)
        return jnp.einsum("gh,djgnp->djgphn", eye, bbj).reshape(depth, nb, gb * p, gb * n)

    def bdiag_out(cc):
        ccj = cc.reshape(depth, nb, gb, p, n)
        return jnp.einsum("gh,djgpn->djgnhp", eye, ccj).reshape(depth, nb, gb * n, gb * p)

    bbd = jnp.concatenate([bdiag_in(bb_re), bdiag_in(bb_im)], axis=-1).astype(BF16)
    cbd = jnp.concatenate([bdiag_out(c_re.astype(F32)), bdiag_out(-c_im.astype(F32))], axis=-2).astype(BF16)
    tri = jnp.asarray(np.tril(np.ones((chunk, chunk), np.float32))).astype(BF16)
    return {
        "bbd": bbd, "cbd": cbd, "tri": tri,
        "pre_re": flat(inv_re), "pre_im": flat(inv_im),
        "post_re": flat(pw_re), "post_im": flat(pw_im),
        "a_re": ab_re.reshape(depth, 1, g * n), "a_im": ab_im.reshape(depth, 1, g * n),
        "d": d_skip.astype(F32).reshape(depth, 1, g * p),
    }


def _attn_prompt_body(qi_ref, ki_ref, last_ref, k_ref, qt_ref, vt_ref, wuvt_ref, g_ref, o_ref,
                      m_scr, l_scr, acc_scr, *, tq, tk, hg):
    pidx = pl.program_id(1)
    qi = qi_ref[pidx]
    ki = ki_ref[pidx]

    @pl.when(ki == 0)
    def _():
        m_scr[...] = jnp.full_like(m_scr, NEG_BIG)
        l_scr[...] = jnp.zeros_like(l_scr)
        acc_scr[...] = jnp.zeros_like(acc_scr)

    k = k_ref[...]
    vt = vt_ref[...]
    kpos = ki * tk + lax.broadcasted_iota(I32, (tk, hg * tq), 0)
    qpos = qi * tq + (lax.broadcasted_iota(I32, (tk, hg * tq), 1) & (tq - 1))
    visible = kpos <= qpos
    for g0 in range(0, N_HEADS, hg):
        cols = slice(g0 * tq, (g0 + hg) * tq)
        qt = jnp.concatenate([qt_ref[hd] for hd in range(g0, g0 + hg)], axis=1)
        st = jnp.dot(k, qt, preferred_element_type=F32)
        st = jnp.where(visible, st, NEG_BIG)
        m_prev = m_scr[:, cols]
        m_new = jnp.maximum(m_prev, jnp.max(st, axis=0, keepdims=True))
        alpha = jnp.exp(m_prev - m_new)
        p = jnp.exp(st - m_new)
        l_scr[:, cols] = alpha * l_scr[:, cols] + jnp.sum(p, axis=0, keepdims=True)
        acc_scr[:, cols] = alpha * acc_scr[:, cols] + jnp.dot(vt, p.astype(BF16),
                                                              preferred_element_type=F32)
        m_scr[:, cols] = m_new

    @pl.when(ki == last_ref[pidx])
    def _():
        outs = []
        for hd in range(N_HEADS):
            cols = slice(hd * tq, (hd + 1) * tq)
            ot = (acc_scr[:, cols] / l_scr[:, cols]).astype(BF16)
            outs.append(jnp.dot(wuvt_ref[hd], ot, preferred_element_type=F32))
        att = jnp.concatenate(outs, axis=0).T
        o_ref[...] = _rms(att, g_ref[...]).astype(BF16)


def _attn_prompt(k_cat, q_t, v_t, wuv_tt, g_attn, *, batch, seq, tq, tk, heads_per_group):
    qd = k_cat.shape[1]
    kv_rank = v_t.shape[0]
    nq = seq // tq
    qi_l, ki_l, last_l = [], [], []
    for qi in range(nq):
        last = (qi * tq + tq - 1) // tk
        for ki in range(last + 1):
            qi_l.append(qi)
            ki_l.append(ki)
            last_l.append(last)
    npairs = len(qi_l)
    qi_a = jnp.asarray(np.array(qi_l, np.int32))
    ki_a = jnp.asarray(np.array(ki_l, np.int32))
    last_a = jnp.asarray(np.array(last_l, np.int32))
    nqb = seq // tq
    nkb = seq // tk
    d_out = N_HEADS * V_HEAD_DIM
    grid_spec = pltpu.PrefetchScalarGridSpec(
        num_scalar_prefetch=3,
        grid=(batch, npairs),
        in_specs=[pl.BlockSpec((tk, qd), lambda b, p, qi, ki, la: (b * nkb + ki[p], 0)),
                  pl.BlockSpec((N_HEADS, qd, tq), lambda b, p, qi, ki, la: (0, 0, b * nqb + qi[p])),
                  pl.BlockSpec((kv_rank, tk), lambda b, p, qi, ki, la: (0, b * nkb + ki[p])),
                  pl.BlockSpec(wuv_tt.shape, lambda b, p, qi, ki, la: (0, 0, 0)),
                  pl.BlockSpec((1, d_out), lambda b, p, qi, ki, la: (0, 0))],
        out_specs=pl.BlockSpec((tq, d_out), lambda b, p, qi, ki, la: (b * nqb + qi[p], 0)),
        scratch_shapes=[pltpu.VMEM((1, N_HEADS * tq), F32), pltpu.VMEM((1, N_HEADS * tq), F32),
                        pltpu.VMEM((kv_rank, N_HEADS * tq), F32)],
    )
    body = functools.partial(_attn_prompt_body, tq=tq, tk=tk, hg=heads_per_group)
    return pl.pallas_call(
        body,
        grid_spec=grid_spec,
        out_shape=jax.ShapeDtypeStruct((batch * seq, d_out), BF16),
        compiler_params=_cparams(2),
        name="attn_prompt",
    )(qi_a, ki_a, last_a, k_cat, q_t, v_t, wuv_tt, g_attn)


def _attn_sample_body(pt_ref, q_ref, knew_ref, wuv_ref, g_ref, lat_hbm, pet_hbm, o_ref,
                      lat_buf, pet_buf, sems, *, layer, n_pages, page, kv_rank, n_new, n_chunks):
    b = pl.program_id(0)
    nb = pl.num_programs(0)

    def page_copies(bb, slot, j):
        pg = pt_ref[bb * n_pages + j]
        return (pltpu.make_async_copy(lat_hbm.at[layer, pg], lat_buf.at[slot, pl.ds(j * page, page)],
                                      sems.at[slot, 0]),
                pltpu.make_async_copy(pet_hbm.at[layer, pg], pet_buf.at[slot, :, pl.ds(j * page, page)],
                                      sems.at[slot, 1]))

    def start_all(bb, slot):
        for j in range(n_pages):
            c0, c1 = page_copies(bb, slot, j)
            c0.start()
            c1.start()

    @pl.when(b == 0)
    def _():
        for ahead in range(SAMPLE_PREFETCH):
            start_all(ahead, ahead)

    slot = b % (SAMPLE_PREFETCH + 1)

    @pl.when(b + SAMPLE_PREFETCH < nb)
    def _():
        start_all(b + SAMPLE_PREFETCH, (b + SAMPLE_PREFETCH) % (SAMPLE_PREFETCH + 1))

    for j in range(n_pages):
        c0, c1 = page_copies(b, slot, j)
        c0.wait()
        c1.wait()

    q = q_ref[0]
    rows = q.shape[0]
    ql = q[:, :kv_rank]
    qp = q[:, kv_rank:]
    contract_last = (((1,), (1,)), ((), ()))
    ck = (n_pages * page) // n_chunks
    ms, ls, os_ = [], [], []
    lats, scores = [], []
    for c in range(n_chunks):
        lat = lat_buf[slot, pl.ds(c * ck, ck), :].astype(BF16)
        pet = pet_buf[slot, :, pl.ds(c * ck, ck)].astype(BF16)
        lats.append(lat)
        scores.append(lax.dot_general(ql, lat, contract_last, preferred_element_type=F32)
                      + jnp.dot(qp, pet, preferred_element_type=F32))
    for lat, s in zip(lats, scores):
        m = jnp.max(s, axis=-1, keepdims=True)
        p = jnp.exp(s - m)
        ms.append(m)
        ls.append(jnp.sum(p, axis=-1, keepdims=True))
        os_.append(jnp.dot(p.astype(BF16), lat, preferred_element_type=F32))
    kn = knew_ref[0]
    sn = lax.dot_general(q, kn, contract_last, preferred_element_type=F32)
    t_row = lax.broadcasted_iota(I32, (rows, n_new), 0) % n_new
    t_col = lax.broadcasted_iota(I32, (rows, n_new), 1)
    sn = jnp.where(t_col <= t_row, sn, NEG_BIG)
    mn = jnp.max(sn, axis=-1, keepdims=True)
    pn = jnp.exp(sn - mn)
    ms.append(mn)
    ls.append(jnp.sum(pn, axis=-1, keepdims=True))
    os_.append(jnp.dot(pn.astype(BF16), kn[:, :kv_rank], preferred_element_type=F32))
    m_all = functools.reduce(jnp.maximum, ms)
    l = jnp.zeros_like(m_all)
    o = jnp.zeros_like(os_[0])
    for mi, li, oi in zip(ms, ls, os_):
        w = jnp.exp(mi - m_all)
        l = l + w * li
        o = o + w * oi
    o = o / l
    res = jnp.dot(o.astype(BF16), wuv_ref[...], preferred_element_type=F32)
    col_head = lax.broadcasted_iota(I32, (n_new, N_HEADS * V_HEAD_DIM), 1) // V_HEAD_DIM
    out = jnp.zeros((n_new, N_HEADS * V_HEAD_DIM), F32)
    for hd in range(N_HEADS):
        out = jnp.where(col_head == hd, res[hd * n_new:(hd + 1) * n_new], out)
    o_ref[0] = _rms(out, g_ref[...]).astype(BF16)


def _attn_sample(page_table, q_s, k_new, wuv_flat, g_attn, cache_lat, cache_pe_t, *, layer, n_chunks):
    bs, rows, qd = q_s.shape
    n_new = k_new.shape[1]
    n_pages = page_table.shape[1]
    page = cache_lat.shape[2]
    kv_rank = cache_lat.shape[3]
    past = n_pages * page
    d_out = N_HEADS * V_HEAD_DIM
    assert bs >= SAMPLE_PREFETCH
    grid_spec = pltpu.PrefetchScalarGridSpec(
        num_scalar_prefetch=1,
        grid=(bs,),
        in_specs=[pl.BlockSpec((1, rows, qd), lambda b, pt: (b, 0, 0)),
                  pl.BlockSpec((1, n_new, qd), lambda b, pt: (b, 0, 0)),
                  pl.BlockSpec(wuv_flat.shape, lambda b, pt: (0, 0)),
                  pl.BlockSpec((1, d_out), lambda b, pt: (0, 0)),
                  pl.BlockSpec(memory_space=pl.ANY),
                  pl.BlockSpec(memory_space=pl.ANY)],
        out_specs=pl.BlockSpec((1, n_new, d_out), lambda b, pt: (b, 0, 0)),
        scratch_shapes=[pltpu.VMEM((SAMPLE_PREFETCH + 1, past, kv_rank), F32),
                        pltpu.VMEM((SAMPLE_PREFETCH + 1, QK_ROPE, past), F32),
                        pltpu.SemaphoreType.DMA((SAMPLE_PREFETCH + 1, 2))],
    )
    body = functools.partial(_attn_sample_body, layer=layer, n_pages=n_pages, page=page,
                             kv_rank=kv_rank, n_new=n_new, n_chunks=n_chunks)
    return pl.pallas_call(
        body,
        grid_spec=grid_spec,
        out_shape=jax.ShapeDtypeStruct((bs, n_new, d_out), BF16),
        compiler_params=_cparams(1),
        name="attn_sample",
    )(page_table.reshape(-1), q_s, k_new, wuv_flat, g_attn, cache_lat, cache_pe_t)


def _outproj_body(x_ref, ssm_ref, att_ref, w_ref, gate_ref, g2_ref, sh_ref, sc_ref, wr_ref, br_ref,
                  xo_ref, h2_ref, ridx_ref, rw_ref, *, mod_bcast):
    half = ssm_ref.shape[1]
    merged = (jnp.dot(ssm_ref[...], w_ref[:half], preferred_element_type=F32)
              + jnp.dot(att_ref[...], w_ref[half:], preferred_element_type=F32))
    gate = gate_ref[0] if mod_bcast else gate_ref[...]
    sh = sh_ref[0] if mod_bcast else sh_ref[...]
    sc = sc_ref[0] if mod_bcast else sc_ref[...]
    xn = x_ref[...] + gate * merged
    xo_ref[...] = xn
    h2 = _rms(xn, g2_ref[...]) * (1.0 + sc) + sh
    h2_ref[...] = h2
    logits = jnp.dot(h2.astype(BF16), wr_ref[...], preferred_element_type=F32) + br_ref[...]
    tm = logits.shape[0]
    lane = lax.broadcasted_iota(I32, (tm, LANES), 1)
    lanef = lane.astype(F32)
    lg = jnp.where(lane < MOE_GROUPS, logits, NEG_BIG)
    mg = jnp.max(lg, axis=-1, keepdims=True)
    gsel = jnp.min(jnp.where(lg == mg, lanef, float(LANES)), axis=-1, keepdims=True)
    wg = 1.0 / jnp.sum(jnp.exp(lg - mg), axis=-1, keepdims=True)
    lo = float(MOE_GROUPS) + float(EXPERTS_PER_GROUP) * gsel
    le = jnp.where(lanef >= lo, jnp.where(lanef < lo + float(EXPERTS_PER_GROUP), logits, NEG_BIG), NEG_BIG)
    v1 = jnp.max(le, axis=-1, keepdims=True)
    i1 = jnp.min(jnp.where(le == v1, lanef, float(LANES)), axis=-1, keepdims=True)
    le2 = jnp.where(lanef == i1, NEG_BIG, le)
    v2 = jnp.max(le2, axis=-1, keepdims=True)
    i2 = jnp.min(jnp.where(le2 == v2, lanef, float(LANES)), axis=-1, keepdims=True)
    e2 = jnp.exp(v2 - v1)
    w1 = wg / (1.0 + e2)
    w2 = wg * e2 / (1.0 + e2)
    ex1 = (i1 - float(MOE_GROUPS)).astype(I32)
    ex2 = (i2 - float(MOE_GROUPS)).astype(I32)
    ridx_ref[...] = jnp.where(lane == 0, ex1, jnp.where(lane == 1, ex2, 0))
    rw_ref[...] = jnp.where(lane == 0, w1, jnp.where(lane == 1, w2, 0.0))


def _outproj(x, ssm_n, att_n, w_out, gate1, g2, shift2, scale2, w_route, b_route, *, tm, mod_bcast,
             rows_per_batch):
    r, d = x.shape
    half = ssm_n.shape[1]
    tpb = rows_per_batch // tm if mod_bcast else 1
    if mod_bcast:
        mod_spec = pl.BlockSpec((1, 1, d), lambda i: (i // tpb, 0, 0))
    else:
        mod_spec = pl.BlockSpec((tm, d), lambda i: (0, 0))
    body = functools.partial(_outproj_body, mod_bcast=mod_bcast)
    return pl.pallas_call(
        body,
        grid=(r // tm,),
        in_specs=[pl.BlockSpec((tm, d), lambda i: (i, 0)),
                  pl.BlockSpec((tm, half), lambda i: (i, 0)),
                  pl.BlockSpec((tm, half), lambda i: (i, 0)),
                  pl.BlockSpec(w_out.shape, lambda i: (0, 0)),
                  mod_spec,
                  pl.BlockSpec((1, d), lambda i: (0, 0)),
                  mod_spec, mod_spec,
                  pl.BlockSpec((d, LANES), lambda i: (0, 0)),
                  pl.BlockSpec((1, LANES), lambda i: (0, 0))],
        out_specs=[pl.BlockSpec((tm, d), lambda i: (i, 0)),
                   pl.BlockSpec((tm, d), lambda i: (i, 0)),
                   pl.BlockSpec((tm, LANES), lambda i: (i, 0)),
                   pl.BlockSpec((tm, LANES), lambda i: (i, 0))],
        out_shape=[jax.ShapeDtypeStruct((r, d), F32),
                   jax.ShapeDtypeStruct((r, d), F32),
                   jax.ShapeDtypeStruct((r, LANES), I32),
                   jax.ShapeDtypeStruct((r, LANES), F32)],
        compiler_params=_cparams(1),
        name="out_proj_router",
    )(x, ssm_n, att_n, w_out, gate1, g2, shift2, scale2, w_route, b_route)


def _route_meta(e_pairs, tile):
    npairs = e_pairs.shape[0]
    n_tiles = -(-npairs // tile) + N_EXPERTS
    oh = (e_pairs[:, None] == jnp.arange(N_EXPERTS, dtype=I32)[None, :]).astype(I32)
    cs = jnp.cumsum(oh, axis=0)
    rank = jnp.sum(cs * oh, axis=1) - 1
    counts = cs[-1]
    padded = ((counts + tile - 1) // tile) * tile
    ends = jnp.cumsum(padded)
    starts = ends - padded
    pos = jnp.sum(oh * starts[None, :], axis=1) + rank
    n_used = ends[-1] // tile
    tile_start = jnp.arange(n_tiles, dtype=I32) * tile
    tile_e = jnp.sum((tile_start[:, None] >= ends[None, :]).astype(I32), axis=1)
    last_e = jnp.sum((jnp.maximum(n_used - 1, 0) * tile >= ends).astype(I32))
    tile_e = jnp.minimum(tile_e, last_e).astype(I32)
    return pos.astype(I32), tile_e, n_used.astype(I32).reshape(1), n_tiles


def _dispatch_body(pos_ref, h_ref, xs_in_ref, xs_ref, sem, *, tm):
    del xs_in_ref
    base = pl.program_id(0) * tm

    def issue(r, c):
        for k in range(TOP_K):
            p = pos_ref[TOP_K * (base + r) + k]
            pltpu.make_async_copy(h_ref.at[pl.ds(r, 1)], xs_ref.at[pl.ds(p, 1)], sem).start(priority=k)
        return c

    lax.fori_loop(0, tm, issue, 0, unroll=8)
    for k in range(TOP_K):
        pltpu.make_async_copy(h_ref, xs_ref.at[pl.ds(0, tm)], sem).wait()


def _dispatch(pos, h2, xs, *, tm):
    r, d = h2.shape
    grid_spec = pltpu.PrefetchScalarGridSpec(
        num_scalar_prefetch=1,
        grid=(r // tm,),
        in_specs=[pl.BlockSpec((tm, d), lambda i, p: (i, 0)),
                  pl.BlockSpec(memory_space=pl.ANY)],
        out_specs=pl.BlockSpec(memory_space=pl.ANY),
        scratch_shapes=[pltpu.SemaphoreType.DMA(())],
    )
    return pl.pallas_call(
        functools.partial(_dispatch_body, tm=tm),
        grid_spec=grid_spec,
        out_shape=jax.ShapeDtypeStruct(xs.shape, xs.dtype),
        input_output_aliases={2: 0},
        compiler_params=_cparams(1),
        name="moe_dispatch",
    )(pos, h2, xs)


def _moe_gemm_body(te_ref, nu_ref, x_ref, wg_ref, wu_ref, wd_ref, y_ref, wg_s, wu_s, wd_s):
    i = pl.program_id(0)
    e = te_ref[i]
    prev = te_ref[jnp.maximum(i - 1, 0)]

    @pl.when(jnp.logical_or(i == 0, e != prev))
    def _():
        wg_s[...] = wg_ref[0].astype(BF16)
        wu_s[...] = wu_ref[0].astype(BF16)
        wd_s[...] = wd_ref[0].astype(BF16)

    @pl.when(i < nu_ref[0])
    def _():
        xb = x_ref[...].astype(BF16)
        a = jnp.dot(xb, wg_s[...], preferred_element_type=F32)
        b = jnp.dot(xb, wu_s[...], preferred_element_type=F32)
        act = (a * jax.nn.sigmoid(a) * b).astype(BF16)
        y_ref[...] = jnp.dot(act, wd_s[...], preferred_element_type=F32)

    @pl.when(i >= nu_ref[0])
    def _():
        y_ref[...] = jnp.zeros_like(y_ref)


def _moe_gemm(tile_e, n_used, xs, w_g, w_u, w_d, *, layer, tile):
    rows, d = xs.shape
    f = w_g.shape[-1]
    ne = w_g.shape[1]
    n_tiles = rows // tile
    grid_spec = pltpu.PrefetchScalarGridSpec(
        num_scalar_prefetch=2,
        grid=(n_tiles,),
        in_specs=[pl.BlockSpec((tile, d), lambda i, te, nu: (jnp.minimum(i, nu[0] - 1), 0)),
                  pl.BlockSpec((1, d, f), lambda i, te, nu: (layer * ne + te[i], 0, 0)),
                  pl.BlockSpec((1, d, f), lambda i, te, nu: (layer * ne + te[i], 0, 0)),
                  pl.BlockSpec((1, f, d), lambda i, te, nu: (layer * ne + te[i], 0, 0))],
        out_specs=pl.BlockSpec((tile, d), lambda i, te, nu: (i, 0)),
        scratch_shapes=[pltpu.VMEM((d, f), BF16), pltpu.VMEM((d, f), BF16), pltpu.VMEM((f, d), BF16)],
    )
    depth = w_g.shape[0]
    return pl.pallas_call(
        _moe_gemm_body,
        grid_spec=grid_spec,
        out_shape=jax.ShapeDtypeStruct((rows, d), F32),
        compiler_params=_cparams(1),
        name="moe_gemm",
    )(tile_e, n_used, xs, w_g.reshape(depth * ne, d, f), w_u.reshape(depth * ne, d, f),
      w_d.reshape(depth * ne, f, d))


def _combine_body(pos_ref, x_ref, rw_ref, gate_ref, y_hbm, o_ref, ybuf, sem, *, tm, mod_bcast):
    base = pl.program_id(0) * tm

    def issue(r, c):
        for k in range(TOP_K):
            p = pos_ref[TOP_K * (base + r) + k]
            pltpu.make_async_copy(y_hbm.at[pl.ds(p, 1)], ybuf.at[k, pl.ds(r, 1)], sem).start(priority=k)
        return c

    lax.fori_loop(0, tm, issue, 0, unroll=8)
    for k in range(TOP_K):
        pltpu.make_async_copy(y_hbm.at[pl.ds(0, tm)], ybuf.at[k], sem).wait()
    gate = gate_ref[0] if mod_bcast else gate_ref[...]
    rw = rw_ref[...]
    moe = rw[:, 0:1] * ybuf[0] + rw[:, 1:2] * ybuf[1]
    o_ref[...] = x_ref[...] + gate * moe


def _combine(pos, x, rw, gate2, y, *, tm, mod_bcast, rows_per_batch):
    r, d = x.shape
    tpb = rows_per_batch // tm if mod_bcast else 1
    if mod_bcast:
        mod_spec = pl.BlockSpec((1, 1, d), lambda i, p: (i // tpb, 0, 0))
    else:
        mod_spec = pl.BlockSpec((tm, d), lambda i, p: (0, 0))
    grid_spec = pltpu.PrefetchScalarGridSpec(
        num_scalar_prefetch=1,
        grid=(r // tm,),
        in_specs=[pl.BlockSpec((tm, d), lambda i, p: (i, 0)),
                  pl.BlockSpec((tm, LANES), lambda i, p: (i, 0)),
                  mod_spec,
                  pl.BlockSpec(memory_space=pl.ANY)],
        out_specs=pl.BlockSpec((tm, d), lambda i, p: (i, 0)),
        scratch_shapes=[pltpu.VMEM((TOP_K, tm, d), F32), pltpu.SemaphoreType.DMA(())],
    )
    return pl.pallas_call(
        functools.partial(_combine_body, tm=tm, mod_bcast=mod_bcast),
        grid_spec=grid_spec,
        out_shape=jax.ShapeDtypeStruct((r, d), F32),
        compiler_params=_cparams(1),
        name="moe_combine",
    )(pos, x, rw, gate2, y)


def _final_norm_body(x_ref, g_ref, o_ref):
    o_ref[...] = _rms(x_ref[...], g_ref[...])


def _final_norm(x, g, *, tm):
    r, d = x.shape
    return pl.pallas_call(
        _final_norm_body,
        grid=(r // tm,),
        in_specs=[pl.BlockSpec((tm, d), lambda i: (i, 0)), pl.BlockSpec((1, d), lambda i: (0, 0))],
        out_specs=pl.BlockSpec((tm, d), lambda i: (i, 0)),
        out_shape=jax.ShapeDtypeStruct((r, d), F32),
        compiler_params=_cparams(1),
        name="final_norm",
    )(x, g)


def _rope_tables(pos):
    half = QK_ROPE // 2
    inv = ROPE_THETA ** (-jnp.arange(half, dtype=F32) * (2.0 / QK_ROPE))
    ang = pos.astype(F32)[:, None] * inv[None, :]
    c, s = jnp.cos(ang), jnp.sin(ang)
    reps = LANES // half
    cos4 = jnp.tile(c, (1, reps))
    sin4 = jnp.tile(jnp.concatenate([-s, s], axis=-1), (1, reps // 2))
    return cos4, sin4


def _pick_tile(n, pref):
    t = min(pref, n)
    while n % t:
        t //= 2
    return t


def kernel(x_prompt, x_sample, cache_kv_latent, cache_k_rope, state_ssm_re, state_ssm_im, page_table,
           c_prompt, c_sample, norm_mix_g, norm_ffn_g, w_ada, b_ada, w_in, ssm_lam_re, ssm_lam_im,
           ssm_log_dt, ssm_b_re, ssm_b_im, ssm_c_re, ssm_c_im, ssm_d, w_glu, b_glu, kv_norm_g, w_uk,
           w_uv, g_ssm_out, g_attn_out, w_out, w_route_group, b_route_group, w_route_expert,
           b_route_expert, w_exp_gate, w_exp_up, w_exp_down, final_norm_g):
    bp, tp, d = x_prompt.shape
    bs, ts_, _ = x_sample.shape
    depth = w_in.shape[0]
    kv_rank = w_uk.shape[1]
    ssm_w = d // 2
    n_groups = ssm_w // SSM_CH
    nst = n_groups * SSM_STATE
    n_pages, page = page_table.shape[1], cache_kv_latent.shape[2]
    past_len = n_pages * page
    rp = bp * tp
    rs = bs * ts_

    w_in_p = w_in.astype(BF16)
    wuk_t = jnp.transpose(w_uk, (0, 2, 3, 1))
    odd_head = (jnp.arange(N_HEADS) % 2 == 1)[None, :, None, None]
    wuk_t = jnp.where(odd_head, jnp.roll(wuk_t, QK_NOPE // 2, axis=2), wuk_t).astype(BF16)
    wuv_tt = jnp.transpose(w_uv, (0, 2, 3, 1)).astype(BF16)
    wuv_flat = w_uv.reshape(depth, kv_rank, N_HEADS * V_HEAD_DIM).astype(BF16)
    w_glu_b = w_glu.astype(BF16)
    w_out_b = w_out.astype(BF16)
    n_route = MOE_GROUPS + N_EXPERTS
    w_route = jnp.concatenate([w_route_group, w_route_expert.reshape(depth, d, N_EXPERTS),
                               jnp.zeros((depth, d, LANES - n_route), F32)], axis=-1).astype(BF16)
    b_route = jnp.concatenate([b_route_group, b_route_expert.reshape(depth, N_EXPERTS),
                               jnp.zeros((depth, LANES - n_route), F32)], axis=-1).reshape(depth, 1, LANES)
    tabs_all = _ssm_tables(ssm_lam_re, ssm_lam_im, ssm_log_dt, ssm_b_re, ssm_b_im, ssm_c_re, ssm_c_im,
                           ssm_d, SSM_CHUNK)
    cos_p, sin_p = _rope_tables(jnp.arange(tp))
    cos_s, sin_s = _rope_tables(past_len + jnp.arange(ts_))
    cos_s = cos_s.reshape(ts_, 1, LANES)
    sin_s = sin_s.reshape(ts_, 1, LANES)

    pad = (-(bs + bp)) % 8
    c_all = jnp.concatenate([c_sample, c_prompt, jnp.zeros((pad, d), F32)], axis=0)
    mod = _ada(c_all, w_ada, b_ada)

    tm_p = _pick_tile(tp, 256)
    ts_ssm = _pick_tile(tp, 512)
    tq = _pick_tile(tp, 512)
    tk = _pick_tile(tp, 512)
    moe_tile = 256
    tm_final = _pick_tile(rp, 512)

    xp = x_prompt.reshape(rp, d)
    xs = jnp.transpose(x_sample, (1, 0, 2)).reshape(rs, d)
    cache_pe_t = jnp.transpose(cache_k_rope, (0, 1, 3, 2))
    h0re = state_ssm_re.reshape(depth, bs, nst)
    h0im = state_ssm_im.reshape(depth, bs, nst)

    lat_p, pe_p, sre_p, sim_p = [], [], [], []
    lat_s, pe_s, sre_s, sim_s = [], [], [], []
    for l in range(depth):
        mod_s = [mod[l, :bs, i * d:(i + 1) * d] for i in range(6)]
        mod_p = [mod[l, bs:bs + bp, i * d:(i + 1) * d].reshape(bp, 1, d) for i in range(6)]
        tabs = {k: (v if k == "tri" else v[l]) for k, v in tabs_all.items()}
        g_mix = norm_mix_g[l].reshape(1, d)
        g_ffn = norm_ffn_g[l].reshape(1, d)
        kvg = kv_norm_g[l].reshape(1, kv_rank)
        bglu = b_glu[l].reshape(1, ssm_w)
        gso = g_ssm_out[l].reshape(1, ssm_w)
        gao = g_attn_out[l].reshape(1, N_HEADS * V_HEAD_DIM)

        u_p, q_t, kcat_p, ckv_p, kpe_p, v_t = _inproj(
            xp, g_mix, mod_p[0], mod_p[1], w_in_p[l], wuk_t[l], kvg, cos_p, sin_p,
            tm=tm_p, mod_bcast=True, rope_bcast=False, rows_per_batch=tp, transposed_q=True)
        ssm_p, s_re, s_im = _ssm_prompt(u_p.reshape(bp, tp, ssm_w), tabs, w_glu_b[l], bglu, gso, ts=ts_ssm)
        att_p = _attn_prompt(kcat_p, q_t, v_t, wuv_tt[l], gao, batch=bp, seq=tp, tq=tq, tk=tk,
                             heads_per_group=N_HEADS)
        xp, h2_p, ridx_p, rw_p = _outproj(
            xp, ssm_p.reshape(rp, ssm_w), att_p, w_out_b[l], mod_p[2], g_ffn, mod_p[3], mod_p[4],
            w_route[l], b_route[l], tm=tm_p, mod_bcast=True, rows_per_batch=tp)
        lat_p.append(ckv_p.reshape(bp, tp, kv_rank))
        pe_p.append(kpe_p.reshape(bp, tp, QK_ROPE))
        sre_p.append(s_re.reshape(bp, n_groups, SSM_STATE))
        sim_p.append(s_im.reshape(bp, n_groups, SSM_STATE))

        u_s, q_s, kcat_s, ckv_s, kpe_s = _inproj(
            xs, g_mix, mod_s[0], mod_s[1], w_in_p[l], wuk_t[l], kvg, cos_s, sin_s,
            tm=bs, mod_bcast=False, rope_bcast=True, rows_per_batch=bs, transposed_q=False)
        ssm_s, s_re, s_im = _ssm_sample(u_s, tabs, h0re[l], h0im[l], w_glu_b[l], bglu, gso)
        q_sb = jnp.transpose(q_s.reshape(N_HEADS, ts_, bs, kv_rank + QK_ROPE), (2, 0, 1, 3))
        q_sb = q_sb.reshape(bs, N_HEADS * ts_, kv_rank + QK_ROPE)
        k_new = jnp.transpose(kcat_s.reshape(ts_, bs, kv_rank + QK_ROPE), (1, 0, 2))
        att_s = _attn_sample(page_table, q_sb, k_new, wuv_flat[l], gao, cache_kv_latent, cache_pe_t,
                             layer=l, n_chunks=4)
        att_s = jnp.transpose(att_s, (1, 0, 2)).reshape(rs, N_HEADS * V_HEAD_DIM)
        xs, h2_s, ridx_s, rw_s = _outproj(
            xs, ssm_s, att_s, w_out_b[l], mod_s[2], g_ffn, mod_s[3], mod_s[4],
            w_route[l], b_route[l], tm=bs, mod_bcast=False, rows_per_batch=bs)
        lat_s.append(jnp.transpose(ckv_s.reshape(ts_, bs, kv_rank), (1, 0, 2)))
        pe_s.append(jnp.transpose(kpe_s.reshape(ts_, bs, QK_ROPE), (1, 0, 2)))
        sre_s.append(s_re.reshape(bs, n_groups, SSM_STATE))
        sim_s.append(s_im.reshape(bs, n_groups, SSM_STATE))

        e_pairs = jnp.concatenate([ridx_p[:, :TOP_K].reshape(-1), ridx_s[:, :TOP_K].reshape(-1)])
        pos, tile_e, n_used, n_tiles = _route_meta(e_pairs, moe_tile)
        pos_p, pos_s = pos[:TOP_K * rp], pos[TOP_K * rp:]
        xsrt = jnp.zeros((n_tiles * moe_tile, d), F32)
        xsrt = _dispatch(pos_p, h2_p, xsrt, tm=tm_p)
        xsrt = _dispatch(pos_s, h2_s, xsrt, tm=bs)
        y = _moe_gemm(tile_e, n_used, xsrt, w_exp_gate, w_exp_up, w_exp_down, layer=l, tile=moe_tile)
        xp = _combine(pos_p, xp, rw_p, mod_p[5], y, tm=tm_p, mod_bcast=True, rows_per_batch=tp)
        xs = _combine(pos_s, xs, rw_s, mod_s[5], y, tm=bs, mod_bcast=False, rows_per_batch=bs)

    fg = final_norm_g.reshape(1, d)
    y_prompt = _final_norm(xp, fg, tm=tm_final).reshape(bp, tp, d)
    y_sample = jnp.transpose(_final_norm(xs, fg, tm=bs).reshape(ts_, bs, d), (1, 0, 2))
    return (y_prompt, y_sample,
            jnp.stack(lat_p), jnp.stack(pe_p), jnp.stack(sre_p), jnp.stack(sim_p),
            jnp.stack(lat_s), jnp.stack(pe_s), jnp.stack(sre_s), jnp.stack(sim_s))
```

```python
import functools
import math

import numpy as np
import jax
import jax.numpy as jnp
from jax import lax
from jax.experimental import pallas as pl
from jax.experimental.pallas import tpu as pltpu

F32 = jnp.float32
BF16 = jnp.bfloat16
I32 = jnp.int32

SSM_CH = 16
SSM_STATE = 64
N_HEADS = 8
QK_NOPE = 128
QK_ROPE = 64
QK_DIM = QK_NOPE + QK_ROPE
V_HEAD_DIM = 128
ROPE_THETA = 10000.0
EPS = 1e-6
SM_SCALE = QK_DIM ** -0.5
MOE_GROUPS = 4
EXPERTS_PER_GROUP = 4
N_EXPERTS = MOE_GROUPS * EXPERTS_PER_GROUP
TOP_K = 2

LANES = 128
MXU_DIM = 256
VMEM_LIMIT = 56 * 1024 * 1024

GROUPS_PER_BLOCK = MXU_DIM // SSM_CH
SSM_CHUNK = 64
SAMPLE_PREFETCH = 2
NEG_BIG = -1e30


def _cparams(n_axes):
    return pltpu.CompilerParams(dimension_semantics=("arbitrary",) * n_axes,
                                vmem_limit_bytes=VMEM_LIMIT)


def _rms(x, g):
    return x * lax.rsqrt(jnp.mean(x * x, axis=-1, keepdims=True) + EPS) * g


def _ada_body(c_ref, w_ref, b_ref, o_ref):
    c = c_ref[...]
    s = (c * jax.nn.sigmoid(c)).astype(BF16)
    o_ref[0] = jnp.dot(s, w_ref[0].astype(BF16), preferred_element_type=F32) + b_ref[0]


def _ada(c_all, w_ada, b_ada):
    depth, d, n6 = w_ada.shape
    r = c_all.shape[0]
    tn = 1024
    return pl.pallas_call(
        _ada_body,
        grid=(depth, n6 // tn),
        in_specs=[pl.BlockSpec((r, d), lambda l, j: (0, 0)),
                  pl.BlockSpec((1, d, tn), lambda l, j: (l, 0, j)),
                  pl.BlockSpec((1, 1, tn), lambda l, j: (l, 0, j))],
        out_specs=pl.BlockSpec((1, r, tn), lambda l, j: (l, 0, j)),
        out_shape=jax.ShapeDtypeStruct((depth, r, n6), F32),
        compiler_params=_cparams(2),
        name="ada_ln",
    )(c_all, w_ada, b_ada.reshape(depth, 1, n6))


def _rope_pairs(p, cos4, sin4, first_half):
    swapped = jnp.where(first_half, pltpu.roll(p, 96, 1), pltpu.roll(p, 32, 1))
    return p * cos4 + swapped * sin4


def _inproj_body(x_ref, g_ref, sh_ref, sc_ref, w_ref, wuk_ref, kvg_ref, cos_ref, sin_ref,
                 u_ref, q_ref, kcat_ref, ckv_ref, kpe_ref, *maybe_vt_ref,
                 mod_bcast, rope_bcast, ssm_w, kv_rank, transposed_q):
    x = x_ref[...]
    tm = x.shape[0]
    sh = sh_ref[0] if mod_bcast else sh_ref[...]
    sc = sc_ref[0] if mod_bcast else sc_ref[...]
    h = _rms(x, g_ref[...]) * (1.0 + sc) + sh
    z = jnp.dot(h.astype(BF16), w_ref[...], preferred_element_type=F32)
    u_ref[...] = z[:, :ssm_w]
    cos4 = cos_ref[0] if rope_bcast else cos_ref[...]
    sin4 = sin_ref[0] if rope_bcast else sin_ref[...]
    lane = lax.broadcasted_iota(I32, (tm, LANES), 1)
    first_half = (lane % QK_ROPE) < (QK_ROPE // 2)
    off_kv = ssm_w + N_HEADS * QK_DIM
    off_kr = off_kv + kv_rank
    low_half = lane < QK_ROPE
    nopes = []
    for hp in range(N_HEADS // 2):
        base = ssm_w + hp * 2 * QK_DIM
        t0 = z[:, base:base + LANES]
        t1 = z[:, base + LANES:base + 2 * LANES]
        t2 = z[:, base + 2 * LANES:base + 3 * LANES]
        nopes.append(t0)
        nopes.append(jnp.where(low_half, t2, t1))
        pr = jnp.where(low_half, t1, t2)
        rp = _rope_pairs(pr, cos4, sin4, first_half) * SM_SCALE
        if transposed_q:
            rpt = rp.T.astype(BF16)
            q_ref[2 * hp, kv_rank:kv_rank + QK_ROPE, :] = rpt[:QK_ROPE]
            q_ref[2 * hp + 1, kv_rank:kv_rank + QK_ROPE, :] = rpt[QK_ROPE:]
        else:
            rp = rp.astype(BF16)
            q_ref[2 * hp, :, kv_rank:kv_rank + QK_ROPE] = rp[:, :QK_ROPE]
            q_ref[2 * hp + 1, :, kv_rank:kv_rank + QK_ROPE] = rp[:, QK_ROPE:]
    for hd in range(N_HEADS):
        ql = jnp.dot(nopes[hd].astype(BF16), wuk_ref[hd], preferred_element_type=F32) * SM_SCALE
        if transposed_q:
            q_ref[hd, :kv_rank, :] = ql.T.astype(BF16)
        else:
            q_ref[hd, :, :kv_rank] = ql.astype(BF16)
    ckv = _rms(z[:, off_kv:off_kr], kvg_ref[...])
    ckv_ref[...] = ckv
    kr = z[:, off_kr:off_kr + QK_ROPE]
    kk = _rope_pairs(jnp.concatenate([kr, kr], axis=1), cos4, sin4, first_half)
    kpe_ref[...] = kk[:, :QK_ROPE]
    kcat_ref[:, :kv_rank] = ckv.astype(BF16)
    kcat_ref[:, kv_rank:kv_rank + QK_ROPE] = kk[:, :QK_ROPE].astype(BF16)
    if transposed_q:
        maybe_vt_ref[0][...] = ckv.T.astype(BF16)


def _inproj(x, g, shift, scale, w, wuk_t, kvg, cos4, sin4, *, tm, mod_bcast, rope_bcast, rows_per_batch,
            transposed_q, layer):
    r, d = x.shape
    ncols = w.shape[2]
    kv_rank = wuk_t.shape[2]
    ssm_w = d // 2
    qd = kv_rank + QK_ROPE
    tpb = rows_per_batch // tm if mod_bcast else 1
    if mod_bcast:
        mod_spec = pl.BlockSpec((1, 1, d), lambda i: (i // tpb, 0, 0))
    else:
        mod_spec = pl.BlockSpec((tm, d), lambda i: (0, 0))
    if rope_bcast:
        rope_spec = pl.BlockSpec((1, 1, LANES), lambda i: (i, 0, 0))
    else:
        rope_spec = pl.BlockSpec((tm, LANES), lambda i: (i % tpb, 0))
    if transposed_q:
        q_spec = pl.BlockSpec((N_HEADS, qd, tm), lambda i: (0, 0, i))
        q_shape = jax.ShapeDtypeStruct((N_HEADS, qd, r), BF16)
    else:
        q_spec = pl.BlockSpec((N_HEADS, tm, qd), lambda i: (0, i, 0))
        q_shape = jax.ShapeDtypeStruct((N_HEADS, r, qd), BF16)
    out_specs = [pl.BlockSpec((tm, ssm_w), lambda i: (i, 0)),
                 q_spec,
                 pl.BlockSpec((tm, qd), lambda i: (i, 0)),
                 pl.BlockSpec((tm, kv_rank), lambda i: (i, 0)),
                 pl.BlockSpec((tm, QK_ROPE), lambda i: (i, 0))]
    out_shape = [jax.ShapeDtypeStruct((r, ssm_w), F32),
                 q_shape,
                 jax.ShapeDtypeStruct((r, qd), BF16),
                 jax.ShapeDtypeStruct((r, kv_rank), F32),
                 jax.ShapeDtypeStruct((r, QK_ROPE), F32)]
    if transposed_q:
        out_specs.append(pl.BlockSpec((kv_rank, tm), lambda i: (0, i)))
        out_shape.append(jax.ShapeDtypeStruct((kv_rank, r), BF16))
    body = functools.partial(_inproj_body, mod_bcast=mod_bcast, rope_bcast=rope_bcast,
                             ssm_w=ssm_w, kv_rank=kv_rank, transposed_q=transposed_q)
    return pl.pallas_call(
        body,
        grid=(r // tm,),
        in_specs=[pl.BlockSpec((tm, d), lambda i: (i, 0)),
                  pl.BlockSpec((1, d), lambda i: (0, 0)),
                  mod_spec, mod_spec,
                  pl.BlockSpec((None, d, ncols), lambda i: (layer, 0, 0)),
                  pl.BlockSpec(wuk_t.shape, lambda i: (0, 0, 0)),
                  pl.BlockSpec((1, kv_rank), lambda i: (0, 0)),
                  rope_spec, rope_spec],
        out_specs=out_specs,
        out_shape=out_shape,
        compiler_params=_cparams(1),
        name="in_proj_mla_prep",
    )(x, g, shift, scale, w, wuk_t, kvg, cos4, sin4)


def _glu_norm(y, wglu_ref, bglu_ref, gout_ref):
    g = jax.nn.gelu(y)
    gate = jnp.dot(g.astype(BF16), wglu_ref[...], preferred_element_type=F32) + bglu_ref[...]
    return _rms(g * jax.nn.sigmoid(gate), gout_ref[...])


def _ssm_prompt_body(u_ref, bbd_ref, cbd_ref, tri_ref, pre_re_ref, pre_im_ref, post_re_ref, post_im_ref,
                     a_re_ref, a_im_ref, d_ref, wglu_ref, bglu_ref, gout_ref,
                     y_ref, sre_ref, sim_ref, bu_scr, h_scr, hprev_scr, yacc_scr, *, n_blocks, chunk):
    tc = pl.program_id(1)
    ts = u_ref.shape[1]
    sl = GROUPS_PER_BLOCK * SSM_STATE

    @pl.when(tc == 0)
    def _():
        hprev_scr[...] = jnp.zeros_like(hprev_scr)

    u = u_ref[0]
    ub = u.astype(BF16)
    tri = tri_ref[...]
    for j in range(n_blocks):
        cols = slice(j * MXU_DIM, (j + 1) * MXU_DIM)
        lanes = slice(j * sl, (j + 1) * sl)
        bu_scr[...] = jnp.dot(ub[:, cols], bbd_ref[j], preferred_element_type=F32)
        a_re = a_re_ref[:, lanes]
        a_im = a_im_ref[:, lanes]

        def sub(s, carry):
            hr, hi = carry
            r0 = s * chunk
            b_re = bu_scr[pl.ds(r0, chunk), 0:sl]
            b_im = bu_scr[pl.ds(r0, chunk), sl:2 * sl]
            p_re = pre_re_ref[:, lanes]
            p_im = pre_im_ref[:, lanes]
            x_re = (p_re * b_re - p_im * b_im).astype(BF16)
            x_im = (p_re * b_im + p_im * b_re).astype(BF16)
            z_re = jnp.dot(tri, x_re, preferred_element_type=F32) + (a_re * hr - a_im * hi)
            z_im = jnp.dot(tri, x_im, preferred_element_type=F32) + (a_re * hi + a_im * hr)
            q_re = post_re_ref[:, lanes]
            q_im = post_im_ref[:, lanes]
            h_re = q_re * z_re - q_im * z_im
            h_im = q_re * z_im + q_im * z_re
            h_scr[pl.ds(r0, chunk), 0:sl] = h_re.astype(BF16)
            h_scr[pl.ds(r0, chunk), sl:2 * sl] = h_im.astype(BF16)
            return h_re[chunk - 1:chunk], h_im[chunk - 1:chunk]

        carry = (hprev_scr[2 * j:2 * j + 1], hprev_scr[2 * j + 1:2 * j + 2])
        for s in range(ts // chunk):
            carry = sub(s, carry)
        hr, hi = carry
        hprev_scr[2 * j:2 * j + 1] = hr
        hprev_scr[2 * j + 1:2 * j + 2] = hi
        sre_ref[0, :, lanes] = hr
        sim_ref[0, :, lanes] = hi
        yacc_scr[:, cols] = (jnp.dot(h_scr[...], cbd_ref[j], preferred_element_type=F32)
                             + d_ref[:, cols] * u[:, cols])
    y_ref[0] = _glu_norm(yacc_scr[...], wglu_ref, bglu_ref, gout_ref).astype(BF16)


def _ssm_prompt(u, tabs, wglu, bglu, gout, *, ts, layer):
    b, t, w = u.shape
    n_blocks = w // MXU_DIM
    sl2 = 2 * GROUPS_PER_BLOCK * SSM_STATE
    nst = (w // SSM_CH) * SSM_STATE
    chunk = SSM_CHUNK
    full2 = lambda shape: pl.BlockSpec(shape, lambda i, j: (0,) * len(shape))
    lsel = lambda shape: pl.BlockSpec((None,) + shape, lambda i, j: (layer,) + (0,) * len(shape))
    body = functools.partial(_ssm_prompt_body, n_blocks=n_blocks, chunk=chunk)
    return pl.pallas_call(
        body,
        grid=(b, t // ts),
        in_specs=[pl.BlockSpec((1, ts, w), lambda i, j: (i, j, 0)),
                  lsel((n_blocks, MXU_DIM, sl2)), lsel((n_blocks, sl2, MXU_DIM)),
                  full2((chunk, chunk)),
                  lsel((chunk, nst)), lsel((chunk, nst)), lsel((chunk, nst)), lsel((chunk, nst)),
                  lsel((1, nst)), lsel((1, nst)), lsel((1, w)),
                  lsel((w, w)), full2((1, w)), full2((1, w))],
        out_specs=[pl.BlockSpec((1, ts, w), lambda i, j: (i, j, 0)),
                   pl.BlockSpec((1, 1, nst), lambda i, j: (i, 0, 0)),
                   pl.BlockSpec((1, 1, nst), lambda i, j: (i, 0, 0))],
        out_shape=[jax.ShapeDtypeStruct((b, t, w), BF16),
                   jax.ShapeDtypeStruct((b, 1, nst), F32),
                   jax.ShapeDtypeStruct((b, 1, nst), F32)],
        scratch_shapes=[pltpu.VMEM((ts, sl2), F32), pltpu.VMEM((ts, sl2), BF16),
                        pltpu.VMEM((2 * n_blocks, sl2 // 2), F32), pltpu.VMEM((ts, w), F32)],
        compiler_params=_cparams(2),
        name="ssm_prompt",
    )(u, tabs["bbd"], tabs["cbd"], tabs["tri"], tabs["pre_re"], tabs["pre_im"], tabs["post_re"],
      tabs["post_im"], tabs["a_re"], tabs["a_im"], tabs["d"], wglu, bglu, gout)


def _ssm_sample_body(u_ref, bbd_ref, cbd_ref, a_re_ref, a_im_ref, d_ref, h0re_ref, h0im_ref,
                     wglu_ref, bglu_ref, gout_ref, y_ref, sre_ref, sim_ref,
                     bu_scr, h_scr, yacc_scr, *, n_blocks, n_steps):
    sl = GROUPS_PER_BLOCK * SSM_STATE
    bs = h0re_ref.shape[0]
    u = u_ref[...]
    ub = u.astype(BF16)
    for j in range(n_blocks):
        cols = slice(j * MXU_DIM, (j + 1) * MXU_DIM)
        lanes = slice(j * sl, (j + 1) * sl)
        bu_scr[...] = jnp.dot(ub[:, cols], bbd_ref[j], preferred_element_type=F32)
        a_re = a_re_ref[:, lanes]
        a_im = a_im_ref[:, lanes]
        hr = h0re_ref[:, lanes]
        hi = h0im_ref[:, lanes]
        for t in range(n_steps):
            rows = slice(t * bs, (t + 1) * bs)
            nr = (a_re * hr - a_im * hi) + bu_scr[rows, 0:sl]
            ni = (a_re * hi + a_im * hr) + bu_scr[rows, sl:2 * sl]
            hr, hi = nr, ni
            h_scr[rows, 0:sl] = hr.astype(BF16)
            h_scr[rows, sl:2 * sl] = hi.astype(BF16)
        sre_ref[:, lanes] = hr
        sim_ref[:, lanes] = hi
        yacc_scr[:, cols] = (jnp.dot(h_scr[...], cbd_ref[j], preferred_element_type=F32)
                             + d_ref[:, cols] * u[:, cols])
    y_ref[...] = _glu_norm(yacc_scr[...], wglu_ref, bglu_ref, gout_ref).astype(BF16)


def _ssm_sample(u, tabs, h0re, h0im, wglu, bglu, gout):
    r, w = u.shape
    bs, nst = h0re.shape
    n_blocks = w // MXU_DIM
    sl2 = 2 * GROUPS_PER_BLOCK * SSM_STATE
    body = functools.partial(_ssm_sample_body, n_blocks=n_blocks, n_steps=r // bs)
    return pl.pallas_call(
        body,
        out_shape=[jax.ShapeDtypeStruct((r, w), BF16),
                   jax.ShapeDtypeStruct((bs, nst), F32),
                   jax.ShapeDtypeStruct((bs, nst), F32)],
        scratch_shapes=[pltpu.VMEM((r, sl2), F32), pltpu.VMEM((r, sl2), BF16), pltpu.VMEM((r, w), F32)],
        compiler_params=pltpu.CompilerParams(vmem_limit_bytes=VMEM_LIMIT),
        name="ssm_sample",
    )(u, tabs["bbd"], tabs["cbd"], tabs["a_re"], tabs["a_im"], tabs["d"], h0re, h0im, wglu, bglu, gout)


def _ssm_tables(lam_re, lam_im, log_dt, b_re, b_im, c_re, c_im, d_skip, chunk):
    depth, g, n = lam_re.shape
    p = b_re.shape[-1]
    gb = GROUPS_PER_BLOCK
    nb = g // gb
    lr = lam_re.astype(F32)
    li = lam_im.astype(F32)
    dt = jnp.exp(log_dt.astype(F32))[..., None]
    mag = jnp.exp(lr * dt)
    ab_re = mag * jnp.cos(li * dt)
    ab_im = mag * jnp.sin(li * dt)
    den = lr * lr + li * li
    f_re = ((ab_re - 1.0) * lr + ab_im * li) / den
    f_im = (ab_im * lr - (ab_re - 1.0) * li) / den
    bb_re = f_re[..., None] * b_re - f_im[..., None] * b_im
    bb_im = f_re[..., None] * b_im + f_im[..., None] * b_re
    ks = np.arange(chunk)
    pw_re = jnp.ones((chunk, depth, g, n), F32)
    pw_im = jnp.zeros((chunk, depth, g, n), F32)
    sq_re, sq_im = ab_re, ab_im
    for bit in range(max(1, int(chunk - 1).bit_length())):
        sel = jnp.asarray(((ks >> bit) & 1).astype(np.float32))[:, None, None, None]
        m_re = sel * sq_re + (1.0 - sel)
        m_im = sel * sq_im
        pw_re, pw_im = pw_re * m_re - pw_im * m_im, pw_re * m_im + pw_im * m_re
        sq_re, sq_im = sq_re * sq_re - sq_im * sq_im, 2.0 * sq_re * sq_im
    nrm = pw_re * pw_re + pw_im * pw_im
    inv_re = pw_re / nrm
    inv_im = -pw_im / nrm
    flat = lambda a: jnp.moveaxis(a, 0, 1).reshape(depth, chunk, g * n)
    mask_in = jnp.asarray((np.arange(gb * p)[:, None] // p) == (np.arange(gb * n)[None, :] // n))
    mask_out = jnp.asarray((np.arange(gb * n)[:, None] // n) == (np.arange(gb * p)[None, :] // p))

    def bdiag_in(bb):
        small = jnp.swapaxes(bb.reshape(depth, nb, gb, n, p), -1, -2).reshape(depth, nb, gb * p, n)
        return jnp.where(mask_in, jnp.tile(small, (1, 1, 1, gb)), 0.0)

    def bdiag_out(cc):
        small = jnp.swapaxes(cc.reshape(depth, nb, gb, p, n), -1, -2).reshape(depth, nb, gb * n, p)
        return jnp.where(mask_out, jnp.tile(small, (1, 1, 1, gb)), 0.0)

    bbd = jnp.concatenate([bdiag_in(bb_re), bdiag_in(bb_im)], axis=-1).astype(BF16)
    cbd = jnp.concatenate([bdiag_out(c_re.astype(F32)), bdiag_out(-c_im.astype(F32))], axis=-2).astype(BF16)
    tri = jnp.asarray(np.tril(np.ones((chunk, chunk), np.float32))).astype(BF16)
    return {
        "bbd": bbd, "cbd": cbd, "tri": tri,
        "pre_re": flat(inv_re), "pre_im": flat(inv_im),
        "post_re": flat(pw_re), "post_im": flat(pw_im),
        "a_re": ab_re.reshape(depth, 1, g * n), "a_im": ab_im.reshape(depth, 1, g * n),
        "d": d_skip.astype(F32).reshape(depth, 1, g * p),
    }


def _attn_prompt_body(qi_ref, ki_ref, last_ref, k_ref, qt_ref, vt_ref, wuvt_ref, g_ref, o_ref,
                      m_scr, l_scr, acc_scr, *, tq, tk, hg):
    pidx = pl.program_id(1)
    qi = qi_ref[pidx]
    ki = ki_ref[pidx]

    @pl.when(ki == 0)
    def _():
        m_scr[...] = jnp.full_like(m_scr, NEG_BIG)
        l_scr[...] = jnp.zeros_like(l_scr)
        acc_scr[...] = jnp.zeros_like(acc_scr)

    k = k_ref[...]
    vt = vt_ref[...]
    kpos = ki * tk + lax.broadcasted_iota(I32, (tk, hg * tq), 0)
    qpos = qi * tq + (lax.broadcasted_iota(I32, (tk, hg * tq), 1) & (tq - 1))
    visible = kpos <= qpos
    for g0 in range(0, N_HEADS, hg):
        cols = slice(g0 * tq, (g0 + hg) * tq)
        qt = jnp.concatenate([qt_ref[hd] for hd in range(g0, g0 + hg)], axis=1)
        st = jnp.dot(k, qt, preferred_element_type=F32)
        st = jnp.where(visible, st, NEG_BIG)
        m_prev = m_scr[:, cols]
        m_new = jnp.maximum(m_prev, jnp.max(st, axis=0, keepdims=True))
        alpha = jnp.exp(m_prev - m_new)
        p = jnp.exp(st - m_new)
        l_scr[:, cols] = alpha * l_scr[:, cols] + jnp.sum(p, axis=0, keepdims=True)
        acc_scr[:, cols] = alpha * acc_scr[:, cols] + jnp.dot(vt, p.astype(BF16),
                                                              preferred_element_type=F32)
        m_scr[:, cols] = m_new

    @pl.when(ki == last_ref[pidx])
    def _():
        outs = []
        for hd in range(N_HEADS):
            cols = slice(hd * tq, (hd + 1) * tq)
            ot = (acc_scr[:, cols] / l_scr[:, cols]).astype(BF16)
            outs.append(jnp.dot(wuvt_ref[hd], ot, preferred_element_type=F32))
        att = jnp.concatenate(outs, axis=0).T
        o_ref[...] = _rms(att, g_ref[...]).astype(BF16)


def _attn_prompt(k_cat, q_t, v_t, wuv_tt, g_attn, *, batch, seq, tq, tk, heads_per_group):
    qd = k_cat.shape[1]
    kv_rank = v_t.shape[0]
    nq = seq // tq
    qi_l, ki_l, last_l = [], [], []
    for qi in range(nq):
        last = (qi * tq + tq - 1) // tk
        for ki in range(last + 1):
            qi_l.append(qi)
            ki_l.append(ki)
            last_l.append(last)
    npairs = len(qi_l)
    qi_a = jnp.asarray(np.array(qi_l, np.int32))
    ki_a = jnp.asarray(np.array(ki_l, np.int32))
    last_a = jnp.asarray(np.array(last_l, np.int32))
    nqb = seq // tq
    nkb = seq // tk
    d_out = N_HEADS * V_HEAD_DIM
    grid_spec = pltpu.PrefetchScalarGridSpec(
        num_scalar_prefetch=3,
        grid=(batch, npairs),
        in_specs=[pl.BlockSpec((tk, qd), lambda b, p, qi, ki, la: (b * nkb + ki[p], 0)),
                  pl.BlockSpec((N_HEADS, qd, tq), lambda b, p, qi, ki, la: (0, 0, b * nqb + qi[p])),
                  pl.BlockSpec((kv_rank, tk), lambda b, p, qi, ki, la: (0, b * nkb + ki[p])),
                  pl.BlockSpec(wuv_tt.shape, lambda b, p, qi, ki, la: (0, 0, 0)),
                  pl.BlockSpec((1, d_out), lambda b, p, qi, ki, la: (0, 0))],
        out_specs=pl.BlockSpec((tq, d_out), lambda b, p, qi, ki, la: (b * nqb + qi[p], 0)),
        scratch_shapes=[pltpu.VMEM((1, N_HEADS * tq), F32), pltpu.VMEM((1, N_HEADS * tq), F32),
                        pltpu.VMEM((kv_rank, N_HEADS * tq), F32)],
    )
    body = functools.partial(_attn_prompt_body, tq=tq, tk=tk, hg=heads_per_group)
    return pl.pallas_call(
        body,
        grid_spec=grid_spec,
        out_shape=jax.ShapeDtypeStruct((batch * seq, d_out), BF16),
        compiler_params=_cparams(2),
        name="attn_prompt",
    )(qi_a, ki_a, last_a, k_cat, q_t, v_t, wuv_tt, g_attn)


def _attn_sample_body(pt_ref, q_ref, knew_ref, wuv_ref, g_ref, lat_hbm, pet_hbm, o_ref,
                      lat_buf, pet_buf, sems, *, layer, n_pages, page, kv_rank, n_new, n_chunks):
    b = pl.program_id(0)
    nb = pl.num_programs(0)

    def page_copies(bb, slot, j):
        pg = pt_ref[bb * n_pages + j]
        return (pltpu.make_async_copy(lat_hbm.at[layer, pg], lat_buf.at[slot, pl.ds(j * page, page)],
                                      sems.at[slot, 0]),
                pltpu.make_async_copy(pet_hbm.at[layer, pg], pet_buf.at[slot, :, pl.ds(j * page, page)],
                                      sems.at[slot, 1]))

    def start_all(bb, slot):
        for j in range(n_pages):
            c0, c1 = page_copies(bb, slot, j)
            c0.start()
            c1.start()

    @pl.when(b == 0)
    def _():
        for ahead in range(SAMPLE_PREFETCH):
            start_all(ahead, ahead)

    slot = b % (SAMPLE_PREFETCH + 1)

    @pl.when(b + SAMPLE_PREFETCH < nb)
    def _():
        start_all(b + SAMPLE_PREFETCH, (b + SAMPLE_PREFETCH) % (SAMPLE_PREFETCH + 1))

    for j in range(n_pages):
        c0, c1 = page_copies(b, slot, j)
        c0.wait()
        c1.wait()

    q = q_ref[0]
    rows = q.shape[0]
    ql = q[:, :kv_rank]
    qp = q[:, kv_rank:]
    contract_last = (((1,), (1,)), ((), ()))
    ck = (n_pages * page) // n_chunks
    ms, ls, os_ = [], [], []
    lats, scores = [], []
    for c in range(n_chunks):
        lat = lat_buf[slot, pl.ds(c * ck, ck), :].astype(BF16)
        pet = pet_buf[slot, :, pl.ds(c * ck, ck)].astype(BF16)
        lats.append(lat)
        scores.append(lax.dot_general(ql, lat, contract_last, preferred_element_type=F32)
                      + jnp.dot(qp, pet, preferred_element_type=F32))
    for lat, s in zip(lats, scores):
        m = jnp.max(s, axis=-1, keepdims=True)
        p = jnp.exp(s - m)
        ms.append(m)
        ls.append(jnp.sum(p, axis=-1, keepdims=True))
        os_.append(jnp.dot(p.astype(BF16), lat, preferred_element_type=F32))
    kn = knew_ref[0]
    sn = lax.dot_general(q, kn, contract_last, preferred_element_type=F32)
    t_row = lax.broadcasted_iota(I32, (rows, n_new), 0) % n_new
    t_col = lax.broadcasted_iota(I32, (rows, n_new), 1)
    sn = jnp.where(t_col <= t_row, sn, NEG_BIG)
    mn = jnp.max(sn, axis=-1, keepdims=True)
    pn = jnp.exp(sn - mn)
    ms.append(mn)
    ls.append(jnp.sum(pn, axis=-1, keepdims=True))
    os_.append(jnp.dot(pn.astype(BF16), kn[:, :kv_rank], preferred_element_type=F32))
    m_all = functools.reduce(jnp.maximum, ms)
    l = jnp.zeros_like(m_all)
    o = jnp.zeros_like(os_[0])
    for mi, li, oi in zip(ms, ls, os_):
        w = jnp.exp(mi - m_all)
        l = l + w * li
        o = o + w * oi
    o = o / l
    res = jnp.dot(o.astype(BF16), wuv_ref[...], preferred_element_type=F32)
    col_head = lax.broadcasted_iota(I32, (n_new, N_HEADS * V_HEAD_DIM), 1) // V_HEAD_DIM
    out = jnp.zeros((n_new, N_HEADS * V_HEAD_DIM), F32)
    for hd in range(N_HEADS):
        out = jnp.where(col_head == hd, res[hd * n_new:(hd + 1) * n_new], out)
    o_ref[0] = _rms(out, g_ref[...]).astype(BF16)


def _attn_sample(page_table, q_s, k_new, wuv_flat, g_attn, cache_lat, cache_pe_t, *, layer, n_chunks):
    bs, rows, qd = q_s.shape
    n_new = k_new.shape[1]
    n_pages = page_table.shape[1]
    page = cache_lat.shape[2]
    kv_rank = cache_lat.shape[3]
    past = n_pages * page
    d_out = N_HEADS * V_HEAD_DIM
    assert bs >= SAMPLE_PREFETCH
    grid_spec = pltpu.PrefetchScalarGridSpec(
        num_scalar_prefetch=1,
        grid=(bs,),
        in_specs=[pl.BlockSpec((1, rows, qd), lambda b, pt: (b, 0, 0)),
                  pl.BlockSpec((1, n_new, qd), lambda b, pt: (b, 0, 0)),
                  pl.BlockSpec(wuv_flat.shape, lambda b, pt: (0, 0)),
                  pl.BlockSpec((1, d_out), lambda b, pt: (0, 0)),
                  pl.BlockSpec(memory_space=pl.ANY),
                  pl.BlockSpec(memory_space=pl.ANY)],
        out_specs=pl.BlockSpec((1, n_new, d_out), lambda b, pt: (b, 0, 0)),
        scratch_shapes=[pltpu.VMEM((SAMPLE_PREFETCH + 1, past, kv_rank), F32),
                        pltpu.VMEM((SAMPLE_PREFETCH + 1, QK_ROPE, past), F32),
                        pltpu.SemaphoreType.DMA((SAMPLE_PREFETCH + 1, 2))],
    )
    body = functools.partial(_attn_sample_body, layer=layer, n_pages=n_pages, page=page,
                             kv_rank=kv_rank, n_new=n_new, n_chunks=n_chunks)
    return pl.pallas_call(
        body,
        grid_spec=grid_spec,
        out_shape=jax.ShapeDtypeStruct((bs, n_new, d_out), BF16),
        compiler_params=_cparams(1),
        name="attn_sample",
    )(page_table.reshape(-1), q_s, k_new, wuv_flat, g_attn, cache_lat, cache_pe_t)


def _outproj_body(x_ref, ssm_ref, att_ref, w_ref, gate_ref, g2_ref, sh_ref, sc_ref, wr_ref, br_ref,
                  xo_ref, h2_ref, ridx_ref, rw_ref, *, mod_bcast):
    half = ssm_ref.shape[1]
    merged = (jnp.dot(ssm_ref[...], w_ref[:half], preferred_element_type=F32)
              + jnp.dot(att_ref[...], w_ref[half:], preferred_element_type=F32))
    gate = gate_ref[0] if mod_bcast else gate_ref[...]
    sh = sh_ref[0] if mod_bcast else sh_ref[...]
    sc = sc_ref[0] if mod_bcast else sc_ref[...]
    xn = x_ref[...] + gate * merged
    xo_ref[...] = xn
    h2 = _rms(xn, g2_ref[...]) * (1.0 + sc) + sh
    h2_ref[...] = h2
    logits = jnp.dot(h2.astype(BF16), wr_ref[...], preferred_element_type=F32) + br_ref[...]
    tm = logits.shape[0]
    lane = lax.broadcasted_iota(I32, (tm, LANES), 1)
    lanef = lane.astype(F32)
    lg = jnp.where(lane < MOE_GROUPS, logits, NEG_BIG)
    mg = jnp.max(lg, axis=-1, keepdims=True)
    gsel = jnp.min(jnp.where(lg == mg, lanef, float(LANES)), axis=-1, keepdims=True)
    wg = 1.0 / jnp.sum(jnp.exp(lg - mg), axis=-1, keepdims=True)
    lo = float(MOE_GROUPS) + float(EXPERTS_PER_GROUP) * gsel
    le = jnp.where(lanef >= lo, jnp.where(lanef < lo + float(EXPERTS_PER_GROUP), logits, NEG_BIG), NEG_BIG)
    v1 = jnp.max(le, axis=-1, keepdims=True)
    i1 = jnp.min(jnp.where(le == v1, lanef, float(LANES)), axis=-1, keepdims=True)
    le2 = jnp.where(lanef == i1, NEG_BIG, le)
    v2 = jnp.max(le2, axis=-1, keepdims=True)
    i2 = jnp.min(jnp.where(le2 == v2, lanef, float(LANES)), axis=-1, keepdims=True)
    e2 = jnp.exp(v2 - v1)
    w1 = wg / (1.0 + e2)
    w2 = wg * e2 / (1.0 + e2)
    ex1 = (i1 - float(MOE_GROUPS)).astype(I32)
    ex2 = (i2 - float(MOE_GROUPS)).astype(I32)
    ridx_ref[...] = jnp.where(lane == 0, ex1, jnp.where(lane == 1, ex2, 0))
    rw_ref[...] = jnp.where(lane == 0, w1, jnp.where(lane == 1, w2, 0.0))


def _outproj(x, ssm_n, att_n, w_out, gate1, g2, shift2, scale2, w_route, b_route, *, tm, mod_bcast,
             rows_per_batch, layer):
    r, d = x.shape
    half = ssm_n.shape[1]
    tpb = rows_per_batch // tm if mod_bcast else 1
    if mod_bcast:
        mod_spec = pl.BlockSpec((1, 1, d), lambda i: (i // tpb, 0, 0))
    else:
        mod_spec = pl.BlockSpec((tm, d), lambda i: (0, 0))
    body = functools.partial(_outproj_body, mod_bcast=mod_bcast)
    return pl.pallas_call(
        body,
        grid=(r // tm,),
        in_specs=[pl.BlockSpec((tm, d), lambda i: (i, 0)),
                  pl.BlockSpec((tm, half), lambda i: (i, 0)),
                  pl.BlockSpec((tm, half), lambda i: (i, 0)),
                  pl.BlockSpec((None,) + w_out.shape[1:], lambda i: (layer, 0, 0)),
                  mod_spec,
                  pl.BlockSpec((1, d), lambda i: (0, 0)),
                  mod_spec, mod_spec,
                  pl.BlockSpec((d, LANES), lambda i: (0, 0)),
                  pl.BlockSpec((1, LANES), lambda i: (0, 0))],
        out_specs=[pl.BlockSpec((tm, d), lambda i: (i, 0)),
                   pl.BlockSpec((tm, d), lambda i: (i, 0)),
                   pl.BlockSpec((tm, LANES), lambda i: (i, 0)),
                   pl.BlockSpec((tm, LANES), lambda i: (i, 0))],
        out_shape=[jax.ShapeDtypeStruct((r, d), F32),
                   jax.ShapeDtypeStruct((r, d), F32),
                   jax.ShapeDtypeStruct((r, LANES), I32),
                   jax.ShapeDtypeStruct((r, LANES), F32)],
        compiler_params=_cparams(1),
        name="out_proj_router",
    )(x, ssm_n, att_n, w_out, gate1, g2, shift2, scale2, w_route, b_route)


def _route_meta(e_pairs, tile):
    npairs = e_pairs.shape[0]
    n_tiles = -(-npairs // tile) + N_EXPERTS
    oh = (e_pairs[:, None] == jnp.arange(N_EXPERTS, dtype=I32)[None, :]).astype(I32)
    cs = jnp.cumsum(oh, axis=0)
    rank = jnp.sum(cs * oh, axis=1) - 1
    counts = cs[-1]
    padded = ((counts + tile - 1) // tile) * tile
    ends = jnp.cumsum(padded)
    starts = ends - padded
    pos = jnp.sum(oh * starts[None, :], axis=1) + rank
    n_used = ends[-1] // tile
    tile_start = jnp.arange(n_tiles, dtype=I32) * tile
    tile_e = jnp.sum((tile_start[:, None] >= ends[None, :]).astype(I32), axis=1)
    last_e = jnp.sum((jnp.maximum(n_used - 1, 0) * tile >= ends).astype(I32))
    tile_e = jnp.minimum(tile_e, last_e).astype(I32)
    return pos.astype(I32), tile_e, n_used.astype(I32).reshape(1), n_tiles


def _dispatch_body(pos_ref, h_ref, xs_in_ref, xs_ref, sem, *, tm):
    del xs_in_ref
    base = pl.program_id(0) * tm

    def issue(r, c):
        for k in range(TOP_K):
            p = pos_ref[TOP_K * (base + r) + k]
            pltpu.make_async_copy(h_ref.at[pl.ds(r, 1)], xs_ref.at[pl.ds(p, 1)], sem).start(priority=k)
        return c

    lax.fori_loop(0, tm, issue, 0, unroll=8)
    for k in range(TOP_K):
        pltpu.make_async_copy(h_ref, xs_ref.at[pl.ds(0, tm)], sem).wait()


def _dispatch(pos, h2, xs, *, tm):
    r, d = h2.shape
    grid_spec = pltpu.PrefetchScalarGridSpec(
        num_scalar_prefetch=1,
        grid=(r // tm,),
        in_specs=[pl.BlockSpec((tm, d), lambda i, p: (i, 0)),
                  pl.BlockSpec(memory_space=pl.ANY)],
        out_specs=pl.BlockSpec(memory_space=pl.ANY),
        scratch_shapes=[pltpu.SemaphoreType.DMA(())],
    )
    return pl.pallas_call(
        functools.partial(_dispatch_body, tm=tm),
        grid_spec=grid_spec,
        out_shape=jax.ShapeDtypeStruct(xs.shape, xs.dtype),
        input_output_aliases={2: 0},
        compiler_params=_cparams(1),
        name="moe_dispatch",
    )(pos, h2, xs)


def _moe_gemm_body(te_ref, nu_ref, x_ref, wg_ref, wu_ref, wd_ref, y_ref, wg_s, wu_s, wd_s):
    i = pl.program_id(0)
    e = te_ref[i]
    prev = te_ref[jnp.maximum(i - 1, 0)]

    @pl.when(jnp.logical_or(i == 0, e != prev))
    def _():
        wg_s[...] = wg_ref[0].astype(BF16)
        wu_s[...] = wu_ref[0].astype(BF16)
        wd_s[...] = wd_ref[0].astype(BF16)

    @pl.when(i < nu_ref[0])
    def _():
        xb = x_ref[...].astype(BF16)
        a = jnp.dot(xb, wg_s[...], preferred_element_type=F32)
        b = jnp.dot(xb, wu_s[...], preferred_element_type=F32)
        act = (a * jax.nn.sigmoid(a) * b).astype(BF16)
        y_ref[...] = jnp.dot(act, wd_s[...], preferred_element_type=F32)

    @pl.when(i >= nu_ref[0])
    def _():
        y_ref[...] = jnp.zeros_like(y_ref)


def _moe_gemm(tile_e, n_used, xs, w_g, w_u, w_d, *, layer, tile):
    rows, d = xs.shape
    f = w_g.shape[-1]
    ne = w_g.shape[1]
    n_tiles = rows // tile
    grid_spec = pltpu.PrefetchScalarGridSpec(
        num_scalar_prefetch=2,
        grid=(n_tiles,),
        in_specs=[pl.BlockSpec((tile, d), lambda i, te, nu: (jnp.minimum(i, nu[0] - 1), 0)),
                  pl.BlockSpec((1, d, f), lambda i, te, nu: (layer * ne + te[i], 0, 0)),
                  pl.BlockSpec((1, d, f), lambda i, te, nu: (layer * ne + te[i], 0, 0)),
                  pl.BlockSpec((1, f, d), lambda i, te, nu: (layer * ne + te[i], 0, 0))],
        out_specs=pl.BlockSpec((tile, d), lambda i, te, nu: (i, 0)),
        scratch_shapes=[pltpu.VMEM((d, f), BF16), pltpu.VMEM((d, f), BF16), pltpu.VMEM((f, d), BF16)],
    )
    depth = w_g.shape[0]
    return pl.pallas_call(
        _moe_gemm_body,
        grid_spec=grid_spec,
        out_shape=jax.ShapeDtypeStruct((rows, d), F32),
        compiler_params=_cparams(1),
        name="moe_gemm",
    )(tile_e, n_used, xs, w_g.reshape(depth * ne, d, f), w_u.reshape(depth * ne, d, f),
      w_d.reshape(depth * ne, f, d))


def _combine_body(pos_ref, x_ref, rw_ref, gate_ref, fg_ref, y_hbm, o_ref, ybuf, sem, *, tm, mod_bcast,
                  final_norm):
    base = pl.program_id(0) * tm

    def issue(r, c):
        for k in range(TOP_K):
            p = pos_ref[TOP_K * (base + r) + k]
            pltpu.make_async_copy(y_hbm.at[pl.ds(p, 1)], ybuf.at[k, pl.ds(r, 1)], sem).start(priority=k)
        return c

    lax.fori_loop(0, tm, issue, 0, unroll=8)
    for k in range(TOP_K):
        pltpu.make_async_copy(y_hbm.at[pl.ds(0, tm)], ybuf.at[k], sem).wait()
    gate = gate_ref[0] if mod_bcast else gate_ref[...]
    rw = rw_ref[...]
    moe = rw[:, 0:1] * ybuf[0] + rw[:, 1:2] * ybuf[1]
    xn = x_ref[...] + gate * moe
    o_ref[...] = _rms(xn, fg_ref[...]) if final_norm else xn


def _combine(pos, x, rw, gate2, final_g, y, *, tm, mod_bcast, rows_per_batch, final_norm):
    r, d = x.shape
    tpb = rows_per_batch // tm if mod_bcast else 1
    if mod_bcast:
        mod_spec = pl.BlockSpec((1, 1, d), lambda i, p: (i // tpb, 0, 0))
    else:
        mod_spec = pl.BlockSpec((tm, d), lambda i, p: (0, 0))
    grid_spec = pltpu.PrefetchScalarGridSpec(
        num_scalar_prefetch=1,
        grid=(r // tm,),
        in_specs=[pl.BlockSpec((tm, d), lambda i, p: (i, 0)),
                  pl.BlockSpec((tm, LANES), lambda i, p: (i, 0)),
                  mod_spec,
                  pl.BlockSpec((1, d), lambda i, p: (0, 0)),
                  pl.BlockSpec(memory_space=pl.ANY)],
        out_specs=pl.BlockSpec((tm, d), lambda i, p: (i, 0)),
        scratch_shapes=[pltpu.VMEM((TOP_K, tm, d), F32), pltpu.SemaphoreType.DMA(())],
    )
    return pl.pallas_call(
        functools.partial(_combine_body, tm=tm, mod_bcast=mod_bcast, final_norm=final_norm),
        grid_spec=grid_spec,
        out_shape=jax.ShapeDtypeStruct((r, d), F32),
        compiler_params=_cparams(1),
        name="moe_combine",
    )(pos, x, rw, gate2, final_g, y)


def _rope_tables(pos):
    half = QK_ROPE // 2
    inv = ROPE_THETA ** (-jnp.arange(half, dtype=F32) * (2.0 / QK_ROPE))
    ang = pos.astype(F32)[:, None] * inv[None, :]
    c, s = jnp.cos(ang), jnp.sin(ang)
    reps = LANES // half
    cos4 = jnp.tile(c, (1, reps))
    sin4 = jnp.tile(jnp.concatenate([-s, s], axis=-1), (1, reps // 2))
    return cos4, sin4


def _pick_tile(n, pref):
    t = min(pref, n)
    while n % t:
        t //= 2
    return t


def kernel(x_prompt, x_sample, cache_kv_latent, cache_k_rope, state_ssm_re, state_ssm_im, page_table,
           c_prompt, c_sample, norm_mix_g, norm_ffn_g, w_ada, b_ada, w_in, ssm_lam_re, ssm_lam_im,
           ssm_log_dt, ssm_b_re, ssm_b_im, ssm_c_re, ssm_c_im, ssm_d, w_glu, b_glu, kv_norm_g, w_uk,
           w_uv, g_ssm_out, g_attn_out, w_out, w_route_group, b_route_group, w_route_expert,
           b_route_expert, w_exp_gate, w_exp_up, w_exp_down, final_norm_g):
    bp, tp, d = x_prompt.shape
    bs, ts_, _ = x_sample.shape
    depth = w_in.shape[0]
    kv_rank = w_uk.shape[1]
    ssm_w = d // 2
    n_groups = ssm_w // SSM_CH
    nst = n_groups * SSM_STATE
    n_pages, page = page_table.shape[1], cache_kv_latent.shape[2]
    past_len = n_pages * page
    rp = bp * tp
    rs = bs * ts_

    w_in_p = w_in.astype(BF16)
    wuk_t = jnp.transpose(w_uk, (0, 2, 3, 1))
    odd_head = (jnp.arange(N_HEADS) % 2 == 1)[None, :, None, None]
    wuk_t = jnp.where(odd_head, jnp.roll(wuk_t, QK_NOPE // 2, axis=2), wuk_t).astype(BF16)
    wuv_tt = jnp.transpose(w_uv, (0, 2, 3, 1)).astype(BF16)
    wuv_flat = w_uv.reshape(depth, kv_rank, N_HEADS * V_HEAD_DIM).astype(BF16)
    w_glu_b = w_glu.astype(BF16)
    w_out_b = w_out.astype(BF16)
    n_route = MOE_GROUPS + N_EXPERTS
    w_route = jnp.concatenate([w_route_group, w_route_expert.reshape(depth, d, N_EXPERTS),
                               jnp.zeros((depth, d, LANES - n_route), F32)], axis=-1).astype(BF16)
    b_route = jnp.concatenate([b_route_group, b_route_expert.reshape(depth, N_EXPERTS),
                               jnp.zeros((depth, LANES - n_route), F32)], axis=-1).reshape(depth, 1, LANES)
    tabs_all = _ssm_tables(ssm_lam_re, ssm_lam_im, ssm_log_dt, ssm_b_re, ssm_b_im, ssm_c_re, ssm_c_im,
                           ssm_d, SSM_CHUNK)
    cos_p, sin_p = _rope_tables(jnp.arange(tp))
    cos_s, sin_s = _rope_tables(past_len + jnp.arange(ts_))
    cos_s = cos_s.reshape(ts_, 1, LANES)
    sin_s = sin_s.reshape(ts_, 1, LANES)

    pad = (-(bs + bp)) % 8
    c_all = jnp.concatenate([c_sample, c_prompt, jnp.zeros((pad, d), F32)], axis=0)
    mod = _ada(c_all, w_ada, b_ada)

    tm_p = _pick_tile(tp, 256)
    ts_ssm = _pick_tile(tp, 512)
    tq = _pick_tile(tp, 512)
    tk = _pick_tile(tp, 512)
    moe_tile = 256
    fg = final_norm_g.reshape(1, d)

    xp = x_prompt.reshape(rp, d)
    xs = jnp.transpose(x_sample, (1, 0, 2)).reshape(rs, d)
    cache_pe_t = jnp.transpose(cache_k_rope, (0, 1, 3, 2))
    h0re = state_ssm_re.reshape(depth, bs, nst)
    h0im = state_ssm_im.reshape(depth, bs, nst)

    lat_p, pe_p, sre_p, sim_p = [], [], [], []
    lat_s, pe_s, sre_s, sim_s = [], [], [], []
    for l in range(depth):
        mod_s = [mod[l, :bs, i * d:(i + 1) * d] for i in range(6)]
        mod_p = [mod[l, bs:bs + bp, i * d:(i + 1) * d].reshape(bp, 1, d) for i in range(6)]
        tabs = {k: tabs_all[k][l] for k in ("bbd", "cbd", "a_re", "a_im", "d")}
        g_mix = norm_mix_g[l].reshape(1, d)
        g_ffn = norm_ffn_g[l].reshape(1, d)
        kvg = kv_norm_g[l].reshape(1, kv_rank)
        bglu = b_glu[l].reshape(1, ssm_w)
        gso = g_ssm_out[l].reshape(1, ssm_w)
        gao = g_attn_out[l].reshape(1, N_HEADS * V_HEAD_DIM)

        u_p, q_t, kcat_p, ckv_p, kpe_p, v_t = _inproj(
            xp, g_mix, mod_p[0], mod_p[1], w_in_p, wuk_t[l], kvg, cos_p, sin_p,
            tm=tm_p, mod_bcast=True, rope_bcast=False, rows_per_batch=tp, transposed_q=True, layer=l)
        ssm_p, s_re, s_im = _ssm_prompt(u_p.reshape(bp, tp, ssm_w), tabs_all, w_glu_b, bglu, gso, ts=ts_ssm,
                                        layer=l)
        att_p = _attn_prompt(kcat_p, q_t, v_t, wuv_tt[l], gao, batch=bp, seq=tp, tq=tq, tk=tk,
                             heads_per_group=N_HEADS)
        xp, h2_p, ridx_p, rw_p = _outproj(
            xp, ssm_p.reshape(rp, ssm_w), att_p, w_out_b, mod_p[2], g_ffn, mod_p[3], mod_p[4],
            w_route[l], b_route[l], tm=tm_p, mod_bcast=True, rows_per_batch=tp, layer=l)
        lat_p.append(ckv_p.reshape(bp, tp, kv_rank))
        pe_p.append(kpe_p.reshape(bp, tp, QK_ROPE))
        sre_p.append(s_re.reshape(bp, n_groups, SSM_STATE))
        sim_p.append(s_im.reshape(bp, n_groups, SSM_STATE))

        u_s, q_s, kcat_s, ckv_s, kpe_s = _inproj(
            xs, g_mix, mod_s[0], mod_s[1], w_in_p, wuk_t[l], kvg, cos_s, sin_s,
            tm=bs, mod_bcast=False, rope_bcast=True, rows_per_batch=bs, transposed_q=False, layer=l)
        ssm_s, s_re, s_im = _ssm_sample(u_s, tabs, h0re[l], h0im[l], w_glu_b[l], bglu, gso)
        q_sb = jnp.transpose(q_s.reshape(N_HEADS, ts_, bs, kv_rank + QK_ROPE), (2, 0, 1, 3))
        q_sb = q_sb.reshape(bs, N_HEADS * ts_, kv_rank + QK_ROPE)
        k_new = jnp.transpose(kcat_s.reshape(ts_, bs, kv_rank + QK_ROPE), (1, 0, 2))
        att_s = _attn_sample(page_table, q_sb, k_new, wuv_flat[l], gao, cache_kv_latent, cache_pe_t,
                             layer=l, n_chunks=4)
        att_s = jnp.transpose(att_s, (1, 0, 2)).reshape(rs, N_HEADS * V_HEAD_DIM)
        xs, h2_s, ridx_s, rw_s = _outproj(
            xs, ssm_s, att_s, w_out_b, mod_s[2], g_ffn, mod_s[3], mod_s[4],
            w_route[l], b_route[l], tm=bs, mod_bcast=False, rows_per_batch=bs, layer=l)
        lat_s.append(jnp.transpose(ckv_s.reshape(ts_, bs, kv_rank), (1, 0, 2)))
        pe_s.append(jnp.transpose(kpe_s.reshape(ts_, bs, QK_ROPE), (1, 0, 2)))
        sre_s.append(s_re.reshape(bs, n_groups, SSM_STATE))
        sim_s.append(s_im.reshape(bs, n_groups, SSM_STATE))

        e_pairs = jnp.concatenate([ridx_p[:, :TOP_K].reshape(-1), ridx_s[:, :TOP_K].reshape(-1)])
        pos, tile_e, n_used, n_tiles = _route_meta(e_pairs, moe_tile)
        pos_p, pos_s = pos[:TOP_K * rp], pos[TOP_K * rp:]
        xsrt = jnp.zeros((n_tiles * moe_tile, d), F32)
        xsrt = _dispatch(pos_p, h2_p, xsrt, tm=tm_p)
        xsrt = _dispatch(pos_s, h2_s, xsrt, tm=bs)
        y = _moe_gemm(tile_e, n_used, xsrt, w_exp_gate, w_exp_up, w_exp_down, layer=l, tile=moe_tile)
        last = l == depth - 1
        xp = _combine(pos_p, xp, rw_p, mod_p[5], fg, y, tm=tm_p, mod_bcast=True, rows_per_batch=tp,
                      final_norm=last)
        xs = _combine(pos_s, xs, rw_s, mod_s[5], fg, y, tm=bs, mod_bcast=False, rows_per_batch=bs,
                      final_norm=last)

    y_prompt = xp.reshape(bp, tp, d)
    y_sample = jnp.transpose(xs.reshape(ts_, bs, d), (1, 0, 2))
    return (y_prompt, y_sample,
            jnp.stack(lat_p), jnp.stack(pe_p), jnp.stack(sre_p), jnp.stack(sim_p),
            jnp.stack(lat_s), jnp.stack(pe_s), jnp.stack(sre_s), jnp.stack(sim_s))
```

```python
import functools
import math

import numpy as np
import jax
import jax.numpy as jnp
from jax import lax
from jax.experimental import pallas as pl
from jax.experimental.pallas import tpu as pltpu

F32 = jnp.float32
BF16 = jnp.bfloat16
I32 = jnp.int32

SSM_CH = 16
SSM_STATE = 64
N_HEADS = 8
QK_NOPE = 128
QK_ROPE = 64
QK_DIM = QK_NOPE + QK_ROPE
V_HEAD_DIM = 128
ROPE_THETA = 10000.0
EPS = 1e-6
SM_SCALE = QK_DIM ** -0.5
MOE_GROUPS = 4
EXPERTS_PER_GROUP = 4
N_EXPERTS = MOE_GROUPS * EXPERTS_PER_GROUP
TOP_K = 2

LANES = 128
MXU_DIM = 256
VMEM_LIMIT = 56 * 1024 * 1024

GROUPS_PER_BLOCK = MXU_DIM // SSM_CH
SSM_CHUNK = 64
SAMPLE_PREFETCH = 3
NEG_BIG = -1e30


def _cparams(n_axes):
    return pltpu.CompilerParams(dimension_semantics=("arbitrary",) * n_axes,
                                vmem_limit_bytes=VMEM_LIMIT)


def _rms(x, g):
    return x * lax.rsqrt(jnp.mean(x * x, axis=-1, keepdims=True) + EPS) * g


def _ada_body(c_ref, w_ref, b_ref, o_ref):
    c = c_ref[...]
    s = (c * jax.nn.sigmoid(c)).astype(BF16)
    o_ref[0] = jnp.dot(s, w_ref[0].astype(BF16), preferred_element_type=F32) + b_ref[0]


def _ada(c_all, w_ada, b_ada):
    depth, d, n6 = w_ada.shape
    r = c_all.shape[0]
    tn = 1024
    return pl.pallas_call(
        _ada_body,
        grid=(depth, n6 // tn),
        in_specs=[pl.BlockSpec((r, d), lambda l, j: (0, 0)),
                  pl.BlockSpec((1, d, tn), lambda l, j: (l, 0, j)),
                  pl.BlockSpec((1, 1, tn), lambda l, j: (l, 0, j))],
        out_specs=pl.BlockSpec((1, r, tn), lambda l, j: (l, 0, j)),
        out_shape=jax.ShapeDtypeStruct((depth, r, n6), F32),
        compiler_params=_cparams(2),
        name="ada_ln",
    )(c_all, w_ada, b_ada.reshape(depth, 1, n6))


def _rope_pairs(p, cos4, sin4, first_half):
    swapped = jnp.where(first_half, pltpu.roll(p, 96, 1), pltpu.roll(p, 32, 1))
    return p * cos4 + swapped * sin4


def _inproj_body(x_ref, g_ref, sh_ref, sc_ref, w_ref, wuk_ref, kvg_ref, cos_ref, sin_ref,
                 u_ref, q_ref, kcat_ref, ckv_ref, kpe_ref, *maybe_vt_ref,
                 mod_bcast, rope_bcast, ssm_w, kv_rank, transposed_q):
    x = x_ref[...]
    tm = x.shape[0]
    sh = sh_ref[0] if mod_bcast else sh_ref[...]
    sc = sc_ref[0] if mod_bcast else sc_ref[...]
    h = _rms(x, g_ref[...]) * (1.0 + sc) + sh
    z = jnp.dot(h.astype(BF16), w_ref[...], preferred_element_type=F32)
    u_ref[...] = z[:, :ssm_w]
    cos4 = cos_ref[0] if rope_bcast else cos_ref[...]
    sin4 = sin_ref[0] if rope_bcast else sin_ref[...]
    lane = lax.broadcasted_iota(I32, (tm, LANES), 1)
    first_half = (lane % QK_ROPE) < (QK_ROPE // 2)
    off_kv = ssm_w + N_HEADS * QK_DIM
    off_kr = off_kv + kv_rank
    low_half = lane < QK_ROPE
    nopes = []
    for hp in range(N_HEADS // 2):
        base = ssm_w + hp * 2 * QK_DIM
        t0 = z[:, base:base + LANES]
        t1 = z[:, base + LANES:base + 2 * LANES]
        t2 = z[:, base + 2 * LANES:base + 3 * LANES]
        nopes.append(t0)
        nopes.append(jnp.where(low_half, t2, t1))
        pr = jnp.where(low_half, t1, t2)
        rp = _rope_pairs(pr, cos4, sin4, first_half) * SM_SCALE
        if transposed_q:
            rpt = rp.T.astype(BF16)
            q_ref[2 * hp, kv_rank:kv_rank + QK_ROPE, :] = rpt[:QK_ROPE]
            q_ref[2 * hp + 1, kv_rank:kv_rank + QK_ROPE, :] = rpt[QK_ROPE:]
        else:
            rp = rp.astype(BF16)
            q_ref[2 * hp, :, kv_rank:kv_rank + QK_ROPE] = rp[:, :QK_ROPE]
            q_ref[2 * hp + 1, :, kv_rank:kv_rank + QK_ROPE] = rp[:, QK_ROPE:]
    for hd in range(N_HEADS):
        ql = jnp.dot(nopes[hd].astype(BF16), wuk_ref[hd], preferred_element_type=F32) * SM_SCALE
        if transposed_q:
            q_ref[hd, :kv_rank, :] = ql.T.astype(BF16)
        else:
            q_ref[hd, :, :kv_rank] = ql.astype(BF16)
    ckv = _rms(z[:, off_kv:off_kr], kvg_ref[...])
    ckv_ref[...] = ckv
    kr = z[:, off_kr:off_kr + QK_ROPE]
    kk = _rope_pairs(jnp.concatenate([kr, kr], axis=1), cos4, sin4, first_half)
    kpe_ref[...] = kk[:, :QK_ROPE]
    kcat_ref[:, :kv_rank] = ckv.astype(BF16)
    kcat_ref[:, kv_rank:kv_rank + QK_ROPE] = kk[:, :QK_ROPE].astype(BF16)
    if transposed_q:
        maybe_vt_ref[0][...] = ckv.T.astype(BF16)


def _inproj(x, g, shift, scale, w, wuk_t, kvg, cos4, sin4, *, tm, mod_bcast, rope_bcast, rows_per_batch,
            transposed_q, layer):
    r, d = x.shape
    ncols = w.shape[2]
    kv_rank = wuk_t.shape[2]
    ssm_w = d // 2
    qd = kv_rank + QK_ROPE
    tpb = rows_per_batch // tm if mod_bcast else 1
    if mod_bcast:
        mod_spec = pl.BlockSpec((1, 1, d), lambda i: (i // tpb, 0, 0))
    else:
        mod_spec = pl.BlockSpec((tm, d), lambda i: (0, 0))
    if rope_bcast:
        rope_spec = pl.BlockSpec((1, 1, LANES), lambda i: (i, 0, 0))
    else:
        rope_spec = pl.BlockSpec((tm, LANES), lambda i: (i % tpb, 0))
    if transposed_q:
        q_spec = pl.BlockSpec((N_HEADS, qd, tm), lambda i: (0, 0, i))
        q_shape = jax.ShapeDtypeStruct((N_HEADS, qd, r), BF16)
    else:
        q_spec = pl.BlockSpec((N_HEADS, tm, qd), lambda i: (0, i, 0))
        q_shape = jax.ShapeDtypeStruct((N_HEADS, r, qd), BF16)
    out_specs = [pl.BlockSpec((tm, ssm_w), lambda i: (i, 0)),
                 q_spec,
                 pl.BlockSpec((tm, qd), lambda i: (i, 0)),
                 pl.BlockSpec((tm, kv_rank), lambda i: (i, 0)),
                 pl.BlockSpec((tm, QK_ROPE), lambda i: (i, 0))]
    out_shape = [jax.ShapeDtypeStruct((r, ssm_w), F32),
                 q_shape,
                 jax.ShapeDtypeStruct((r, qd), BF16),
                 jax.ShapeDtypeStruct((r, kv_rank), F32),
                 jax.ShapeDtypeStruct((r, QK_ROPE), F32)]
    if transposed_q:
        out_specs.append(pl.BlockSpec((kv_rank, tm), lambda i: (0, i)))
        out_shape.append(jax.ShapeDtypeStruct((kv_rank, r), BF16))
    body = functools.partial(_inproj_body, mod_bcast=mod_bcast, rope_bcast=rope_bcast,
                             ssm_w=ssm_w, kv_rank=kv_rank, transposed_q=transposed_q)
    return pl.pallas_call(
        body,
        grid=(r // tm,),
        in_specs=[pl.BlockSpec((tm, d), lambda i: (i, 0)),
                  pl.BlockSpec((1, d), lambda i: (0, 0)),
                  mod_spec, mod_spec,
                  pl.BlockSpec((None, d, ncols), lambda i: (layer, 0, 0)),
                  pl.BlockSpec(wuk_t.shape, lambda i: (0, 0, 0)),
                  pl.BlockSpec((1, kv_rank), lambda i: (0, 0)),
                  rope_spec, rope_spec],
        out_specs=out_specs,
        out_shape=out_shape,
        compiler_params=_cparams(1),
        name="in_proj_mla_prep",
    )(x, g, shift, scale, w, wuk_t, kvg, cos4, sin4)


def _glu_norm(y, wglu_ref, bglu_ref, gout_ref):
    g = jax.nn.gelu(y)
    gate = jnp.dot(g.astype(BF16), wglu_ref[...], preferred_element_type=F32) + bglu_ref[...]
    return _rms(g * jax.nn.sigmoid(gate), gout_ref[...])


def _ssm_prompt_body(u_ref, bbd_ref, cbd_ref, tri_ref, pre_re_ref, pre_im_ref, post_re_ref, post_im_ref,
                     a_re_ref, a_im_ref, d_ref, wglu_ref, bglu_ref, gout_ref,
                     y_ref, sre_ref, sim_ref, bu_scr, h_scr, hprev_scr, yacc_scr, *, n_blocks, chunk):
    tc = pl.program_id(1)
    ts = u_ref.shape[1]
    sl = GROUPS_PER_BLOCK * SSM_STATE

    @pl.when(tc == 0)
    def _():
        hprev_scr[...] = jnp.zeros_like(hprev_scr)

    u = u_ref[0]
    ub = u.astype(BF16)
    tri = tri_ref[...]
    for j in range(n_blocks):
        cols = slice(j * MXU_DIM, (j + 1) * MXU_DIM)
        lanes = slice(j * sl, (j + 1) * sl)
        bu_scr[...] = jnp.dot(ub[:, cols], bbd_ref[j], preferred_element_type=F32)
        a_re = a_re_ref[:, lanes]
        a_im = a_im_ref[:, lanes]

        def sub(s, carry):
            hr, hi = carry
            r0 = s * chunk
            b_re = bu_scr[pl.ds(r0, chunk), 0:sl]
            b_im = bu_scr[pl.ds(r0, chunk), sl:2 * sl]
            p_re = pre_re_ref[:, lanes]
            p_im = pre_im_ref[:, lanes]
            x_re = (p_re * b_re - p_im * b_im).astype(BF16)
            x_im = (p_re * b_im + p_im * b_re).astype(BF16)
            z_re = jnp.dot(tri, x_re, preferred_element_type=F32) + (a_re * hr - a_im * hi)
            z_im = jnp.dot(tri, x_im, preferred_element_type=F32) + (a_re * hi + a_im * hr)
            q_re = post_re_ref[:, lanes]
            q_im = post_im_ref[:, lanes]
            h_re = q_re * z_re - q_im * z_im
            h_im = q_re * z_im + q_im * z_re
            h_scr[pl.ds(r0, chunk), 0:sl] = h_re.astype(BF16)
            h_scr[pl.ds(r0, chunk), sl:2 * sl] = h_im.astype(BF16)
            return h_re[chunk - 1:chunk], h_im[chunk - 1:chunk]

        carry = (hprev_scr[2 * j:2 * j + 1], hprev_scr[2 * j + 1:2 * j + 2])
        for s in range(ts // chunk):
            carry = sub(s, carry)
        hr, hi = carry
        hprev_scr[2 * j:2 * j + 1] = hr
        hprev_scr[2 * j + 1:2 * j + 2] = hi
        sre_ref[0, :, lanes] = hr
        sim_ref[0, :, lanes] = hi
        yacc_scr[:, cols] = (jnp.dot(h_scr[...], cbd_ref[j], preferred_element_type=F32)
                             + d_ref[:, cols] * u[:, cols])
    y_ref[0] = _glu_norm(yacc_scr[...], wglu_ref, bglu_ref, gout_ref).astype(BF16)


def _ssm_prompt(u, tabs, wglu, bglu, gout, *, ts, layer):
    b, t, w = u.shape
    n_blocks = w // MXU_DIM
    sl2 = 2 * GROUPS_PER_BLOCK * SSM_STATE
    nst = (w // SSM_CH) * SSM_STATE
    chunk = SSM_CHUNK
    full2 = lambda shape: pl.BlockSpec(shape, lambda i, j: (0,) * len(shape))
    lsel = lambda shape: pl.BlockSpec((None,) + shape, lambda i, j: (layer,) + (0,) * len(shape))
    body = functools.partial(_ssm_prompt_body, n_blocks=n_blocks, chunk=chunk)
    return pl.pallas_call(
        body,
        grid=(b, t // ts),
        in_specs=[pl.BlockSpec((1, ts, w), lambda i, j: (i, j, 0)),
                  lsel((n_blocks, MXU_DIM, sl2)), lsel((n_blocks, sl2, MXU_DIM)),
                  full2((chunk, chunk)),
                  lsel((chunk, nst)), lsel((chunk, nst)), lsel((chunk, nst)), lsel((chunk, nst)),
                  lsel((1, nst)), lsel((1, nst)), lsel((1, w)),
                  lsel((w, w)), full2((1, w)), full2((1, w))],
        out_specs=[pl.BlockSpec((1, ts, w), lambda i, j: (i, j, 0)),
                   pl.BlockSpec((1, 1, nst), lambda i, j: (i, 0, 0)),
                   pl.BlockSpec((1, 1, nst), lambda i, j: (i, 0, 0))],
        out_shape=[jax.ShapeDtypeStruct((b, t, w), BF16),
                   jax.ShapeDtypeStruct((b, 1, nst), F32),
                   jax.ShapeDtypeStruct((b, 1, nst), F32)],
        scratch_shapes=[pltpu.VMEM((ts, sl2), F32), pltpu.VMEM((ts, sl2), BF16),
                        pltpu.VMEM((2 * n_blocks, sl2 // 2), F32), pltpu.VMEM((ts, w), F32)],
        compiler_params=_cparams(2),
        name="ssm_prompt",
    )(u, tabs["bbd"], tabs["cbd"], tabs["tri"], tabs["pre_re"], tabs["pre_im"], tabs["post_re"],
      tabs["post_im"], tabs["a_re"], tabs["a_im"], tabs["d"], wglu, bglu, gout)


def _ssm_sample_body(u_ref, bbd_ref, cbd_ref, a_re_ref, a_im_ref, d_ref, h0re_ref, h0im_ref,
                     wglu_ref, bglu_ref, gout_ref, y_ref, sre_ref, sim_ref,
                     bu_scr, h_scr, yacc_scr, *, n_blocks, n_steps):
    sl = GROUPS_PER_BLOCK * SSM_STATE
    bs = h0re_ref.shape[0]
    u = u_ref[...]
    ub = u.astype(BF16)
    for j in range(n_blocks):
        cols = slice(j * MXU_DIM, (j + 1) * MXU_DIM)
        lanes = slice(j * sl, (j + 1) * sl)
        bu_scr[...] = jnp.dot(ub[:, cols], bbd_ref[j], preferred_element_type=F32)
        a_re = a_re_ref[:, lanes]
        a_im = a_im_ref[:, lanes]
        hr = h0re_ref[:, lanes]
        hi = h0im_ref[:, lanes]
        for t in range(n_steps):
            rows = slice(t * bs, (t + 1) * bs)
            nr = (a_re * hr - a_im * hi) + bu_scr[rows, 0:sl]
            ni = (a_re * hi + a_im * hr) + bu_scr[rows, sl:2 * sl]
            hr, hi = nr, ni
            h_scr[rows, 0:sl] = hr.astype(BF16)
            h_scr[rows, sl:2 * sl] = hi.astype(BF16)
        sre_ref[:, lanes] = hr
        sim_ref[:, lanes] = hi
        yacc_scr[:, cols] = (jnp.dot(h_scr[...], cbd_ref[j], preferred_element_type=F32)
                             + d_ref[:, cols] * u[:, cols])
    y_ref[...] = _glu_norm(yacc_scr[...], wglu_ref, bglu_ref, gout_ref).astype(BF16)


def _ssm_sample(u, tabs, h0re, h0im, wglu, bglu, gout):
    r, w = u.shape
    bs, nst = h0re.shape
    n_blocks = w // MXU_DIM
    sl2 = 2 * GROUPS_PER_BLOCK * SSM_STATE
    body = functools.partial(_ssm_sample_body, n_blocks=n_blocks, n_steps=r // bs)
    return pl.pallas_call(
        body,
        out_shape=[jax.ShapeDtypeStruct((r, w), BF16),
                   jax.ShapeDtypeStruct((bs, nst), F32),
                   jax.ShapeDtypeStruct((bs, nst), F32)],
        scratch_shapes=[pltpu.VMEM((r, sl2), F32), pltpu.VMEM((r, sl2), BF16), pltpu.VMEM((r, w), F32)],
        compiler_params=pltpu.CompilerParams(vmem_limit_bytes=VMEM_LIMIT),
        name="ssm_sample",
    )(u, tabs["bbd"], tabs["cbd"], tabs["a_re"], tabs["a_im"], tabs["d"], h0re, h0im, wglu, bglu, gout)


def _ssm_tables(lam_re, lam_im, log_dt, b_re, b_im, c_re, c_im, d_skip, chunk):
    depth, g, n = lam_re.shape
    p = b_re.shape[-1]
    gb = GROUPS_PER_BLOCK
    nb = g // gb
    lr = lam_re.astype(F32)
    li = lam_im.astype(F32)
    dt = jnp.exp(log_dt.astype(F32))[..., None]
    mag = jnp.exp(lr * dt)
    ab_re = mag * jnp.cos(li * dt)
    ab_im = mag * jnp.sin(li * dt)
    den = lr * lr + li * li
    f_re = ((ab_re - 1.0) * lr + ab_im * li) / den
    f_im = (ab_im * lr - (ab_re - 1.0) * li) / den
    bb_re = f_re[..., None] * b_re - f_im[..., None] * b_im
    bb_im = f_re[..., None] * b_im + f_im[..., None] * b_re
    ks = np.arange(chunk)
    pw_re = jnp.ones((chunk, depth, g, n), F32)
    pw_im = jnp.zeros((chunk, depth, g, n), F32)
    sq_re, sq_im = ab_re, ab_im
    for bit in range(max(1, int(chunk - 1).bit_length())):
        sel = jnp.asarray(((ks >> bit) & 1).astype(np.float32))[:, None, None, None]
        m_re = sel * sq_re + (1.0 - sel)
        m_im = sel * sq_im
        pw_re, pw_im = pw_re * m_re - pw_im * m_im, pw_re * m_im + pw_im * m_re
        sq_re, sq_im = sq_re * sq_re - sq_im * sq_im, 2.0 * sq_re * sq_im
    nrm = pw_re * pw_re + pw_im * pw_im
    inv_re = pw_re / nrm
    inv_im = -pw_im / nrm
    flat = lambda a: jnp.moveaxis(a, 0, 1).reshape(depth, chunk, g * n)
    mask_in = jnp.asarray((np.arange(gb * p)[:, None] // p) == (np.arange(gb * n)[None, :] // n))
    mask_out = jnp.asarray((np.arange(gb * n)[:, None] // n) == (np.arange(gb * p)[None, :] // p))

    def bdiag_in(bb):
        small = jnp.swapaxes(bb.reshape(depth, nb, gb, n, p), -1, -2).reshape(depth, nb, gb * p, n)
        return jnp.where(mask_in, jnp.tile(small, (1, 1, 1, gb)), 0.0)

    def bdiag_out(cc):
        small = jnp.swapaxes(cc.reshape(depth, nb, gb, p, n), -1, -2).reshape(depth, nb, gb * n, p)
        return jnp.where(mask_out, jnp.tile(small, (1, 1, 1, gb)), 0.0)

    bbd = jnp.concatenate([bdiag_in(bb_re), bdiag_in(bb_im)], axis=-1).astype(BF16)
    cbd = jnp.concatenate([bdiag_out(c_re.astype(F32)), bdiag_out(-c_im.astype(F32))], axis=-2).astype(BF16)
    tri = jnp.asarray(np.tril(np.ones((chunk, chunk), np.float32))).astype(BF16)
    return {
        "bbd": bbd, "cbd": cbd, "tri": tri,
        "pre_re": flat(inv_re), "pre_im": flat(inv_im),
        "post_re": flat(pw_re), "post_im": flat(pw_im),
        "a_re": ab_re.reshape(depth, 1, g * n), "a_im": ab_im.reshape(depth, 1, g * n),
        "d": d_skip.astype(F32).reshape(depth, 1, g * p),
    }


def _attn_prompt_body(qi_ref, ki_ref, last_ref, k_ref, qt_ref, vt_ref, wuvt_ref, g_ref, o_ref,
                      m_scr, l_scr, acc_scr, *, tq, tk, hg):
    pidx = pl.program_id(1)
    qi = qi_ref[pidx]
    ki = ki_ref[pidx]

    @pl.when(ki == 0)
    def _():
        m_scr[...] = jnp.full_like(m_scr, NEG_BIG)
        l_scr[...] = jnp.zeros_like(l_scr)
        acc_scr[...] = jnp.zeros_like(acc_scr)

    k = k_ref[...]
    vt = vt_ref[...]
    kpos = ki * tk + lax.broadcasted_iota(I32, (tk, hg * tq), 0)
    qpos = qi * tq + (lax.broadcasted_iota(I32, (tk, hg * tq), 1) & (tq - 1))
    visible = kpos <= qpos
    for g0 in range(0, N_HEADS, hg):
        cols = slice(g0 * tq, (g0 + hg) * tq)
        qt = jnp.concatenate([qt_ref[hd] for hd in range(g0, g0 + hg)], axis=1)
        st = jnp.dot(k, qt, preferred_element_type=F32)
        st = jnp.where(visible, st, NEG_BIG)
        m_prev = m_scr[:, cols]
        m_new = jnp.maximum(m_prev, jnp.max(st, axis=0, keepdims=True))
        alpha = jnp.exp(m_prev - m_new)
        p = jnp.exp(st - m_new)
        l_scr[:, cols] = alpha * l_scr[:, cols] + jnp.sum(p, axis=0, keepdims=True)
        acc_scr[:, cols] = alpha * acc_scr[:, cols] + jnp.dot(vt, p.astype(BF16),
                                                              preferred_element_type=F32)
        m_scr[:, cols] = m_new

    @pl.when(ki == last_ref[pidx])
    def _():
        outs = []
        for hd in range(N_HEADS):
            cols = slice(hd * tq, (hd + 1) * tq)
            ot = (acc_scr[:, cols] / l_scr[:, cols]).astype(BF16)
            outs.append(jnp.dot(wuvt_ref[hd], ot, preferred_element_type=F32))
        att = jnp.concatenate(outs, axis=0).T
        o_ref[...] = _rms(att, g_ref[...]).astype(BF16)


def _attn_prompt(k_cat, q_t, v_t, wuv_tt, g_attn, *, batch, seq, tq, tk, heads_per_group):
    qd = k_cat.shape[1]
    kv_rank = v_t.shape[0]
    nq = seq // tq
    qi_l, ki_l, last_l = [], [], []
    for qi in range(nq):
        last = (qi * tq + tq - 1) // tk
        for ki in range(last + 1):
            qi_l.append(qi)
            ki_l.append(ki)
            last_l.append(last)
    npairs = len(qi_l)
    qi_a = jnp.asarray(np.array(qi_l, np.int32))
    ki_a = jnp.asarray(np.array(ki_l, np.int32))
    last_a = jnp.asarray(np.array(last_l, np.int32))
    nqb = seq // tq
    nkb = seq // tk
    d_out = N_HEADS * V_HEAD_DIM
    grid_spec = pltpu.PrefetchScalarGridSpec(
        num_scalar_prefetch=3,
        grid=(batch, npairs),
        in_specs=[pl.BlockSpec((tk, qd), lambda b, p, qi, ki, la: (b * nkb + ki[p], 0)),
                  pl.BlockSpec((N_HEADS, qd, tq), lambda b, p, qi, ki, la: (0, 0, b * nqb + qi[p])),
                  pl.BlockSpec((kv_rank, tk), lambda b, p, qi, ki, la: (0, b * nkb + ki[p])),
                  pl.BlockSpec(wuv_tt.shape, lambda b, p, qi, ki, la: (0, 0, 0)),
                  pl.BlockSpec((1, d_out), lambda b, p, qi, ki, la: (0, 0))],
        out_specs=pl.BlockSpec((tq, d_out), lambda b, p, qi, ki, la: (b * nqb + qi[p], 0)),
        scratch_shapes=[pltpu.VMEM((1, N_HEADS * tq), F32), pltpu.VMEM((1, N_HEADS * tq), F32),
                        pltpu.VMEM((kv_rank, N_HEADS * tq), F32)],
    )
    body = functools.partial(_attn_prompt_body, tq=tq, tk=tk, hg=heads_per_group)
    return pl.pallas_call(
        body,
        grid_spec=grid_spec,
        out_shape=jax.ShapeDtypeStruct((batch * seq, d_out), BF16),
        compiler_params=_cparams(2),
        name="attn_prompt",
    )(qi_a, ki_a, last_a, k_cat, q_t, v_t, wuv_tt, g_attn)


def _attn_sample_body(pt_ref, q_ref, knew_ref, wuv_ref, g_ref, lat_hbm, pet_hbm, o_ref,
                      lat_buf, pet_buf, sems, *, layer, n_pages, page, kv_rank, n_new, n_chunks):
    b = pl.program_id(0)
    nb = pl.num_programs(0)

    def page_copies(bb, slot, j):
        pg = pt_ref[bb * n_pages + j]
        return (pltpu.make_async_copy(lat_hbm.at[layer, pg], lat_buf.at[slot, pl.ds(j * page, page)],
                                      sems.at[slot, 0]),
                pltpu.make_async_copy(pet_hbm.at[layer, pg], pet_buf.at[slot, :, pl.ds(j * page, page)],
                                      sems.at[slot, 1]))

    def start_all(bb, slot):
        for j in range(n_pages):
            c0, c1 = page_copies(bb, slot, j)
            c0.start()
            c1.start()

    @pl.when(b == 0)
    def _():
        for ahead in range(SAMPLE_PREFETCH):
            start_all(ahead, ahead)

    slot = b % (SAMPLE_PREFETCH + 1)

    @pl.when(b + SAMPLE_PREFETCH < nb)
    def _():
        start_all(b + SAMPLE_PREFETCH, (b + SAMPLE_PREFETCH) % (SAMPLE_PREFETCH + 1))

    for j in range(n_pages):
        c0, c1 = page_copies(b, slot, j)
        c0.wait()
        c1.wait()

    q = q_ref[0]
    rows = q.shape[0]
    ql = q[:, :kv_rank]
    qp = q[:, kv_rank:]
    contract_last = (((1,), (1,)), ((), ()))
    ck = (n_pages * page) // n_chunks
    ms, ls, os_ = [], [], []
    lats, scores = [], []
    for c in range(n_chunks):
        lat = lat_buf[slot, pl.ds(c * ck, ck), :].astype(BF16)
        pet = pet_buf[slot, :, pl.ds(c * ck, ck)].astype(BF16)
        lats.append(lat)
        scores.append(lax.dot_general(ql, lat, contract_last, preferred_element_type=F32)
                      + jnp.dot(qp, pet, preferred_element_type=F32))
    for lat, s in zip(lats, scores):
        m = jnp.max(s, axis=-1, keepdims=True)
        p = jnp.exp(s - m)
        ms.append(m)
        ls.append(jnp.sum(p, axis=-1, keepdims=True))
        os_.append(jnp.dot(p.astype(BF16), lat, preferred_element_type=F32))
    kn = knew_ref[0]
    sn = lax.dot_general(q, kn, contract_last, preferred_element_type=F32)
    t_row = lax.broadcasted_iota(I32, (rows, n_new), 0) % n_new
    t_col = lax.broadcasted_iota(I32, (rows, n_new), 1)
    sn = jnp.where(t_col <= t_row, sn, NEG_BIG)
    mn = jnp.max(sn, axis=-1, keepdims=True)
    pn = jnp.exp(sn - mn)
    ms.append(mn)
    ls.append(jnp.sum(pn, axis=-1, keepdims=True))
    os_.append(jnp.dot(pn.astype(BF16), kn[:, :kv_rank], preferred_element_type=F32))
    m_all = functools.reduce(jnp.maximum, ms)
    l = jnp.zeros_like(m_all)
    o = jnp.zeros_like(os_[0])
    for mi, li, oi in zip(ms, ls, os_):
        w = jnp.exp(mi - m_all)
        l = l + w * li
        o = o + w * oi
    o = o / l
    res = jnp.dot(o.astype(BF16), wuv_ref[...], preferred_element_type=F32)
    col_head = lax.broadcasted_iota(I32, (n_new, N_HEADS * V_HEAD_DIM), 1) // V_HEAD_DIM
    out = jnp.zeros((n_new, N_HEADS * V_HEAD_DIM), F32)
    for hd in range(N_HEADS):
        out = jnp.where(col_head == hd, res[hd * n_new:(hd + 1) * n_new], out)
    o_ref[0] = _rms(out, g_ref[...]).astype(BF16)


def _attn_sample(page_table, q_s, k_new, wuv_flat, g_attn, cache_lat, cache_pe_t, *, layer, n_chunks):
    bs, rows, qd = q_s.shape
    n_new = k_new.shape[1]
    n_pages = page_table.shape[1]
    page = cache_lat.shape[2]
    kv_rank = cache_lat.shape[3]
    past = n_pages * page
    d_out = N_HEADS * V_HEAD_DIM
    assert bs >= SAMPLE_PREFETCH
    grid_spec = pltpu.PrefetchScalarGridSpec(
        num_scalar_prefetch=1,
        grid=(bs,),
        in_specs=[pl.BlockSpec((1, rows, qd), lambda b, pt: (b, 0, 0)),
                  pl.BlockSpec((1, n_new, qd), lambda b, pt: (b, 0, 0)),
                  pl.BlockSpec(wuv_flat.shape, lambda b, pt: (0, 0)),
                  pl.BlockSpec((1, d_out), lambda b, pt: (0, 0)),
                  pl.BlockSpec(memory_space=pl.ANY),
                  pl.BlockSpec(memory_space=pl.ANY)],
        out_specs=pl.BlockSpec((1, n_new, d_out), lambda b, pt: (b, 0, 0)),
        scratch_shapes=[pltpu.VMEM((SAMPLE_PREFETCH + 1, past, kv_rank), F32),
                        pltpu.VMEM((SAMPLE_PREFETCH + 1, QK_ROPE, past), F32),
                        pltpu.SemaphoreType.DMA((SAMPLE_PREFETCH + 1, 2))],
    )
    body = functools.partial(_attn_sample_body, layer=layer, n_pages=n_pages, page=page,
                             kv_rank=kv_rank, n_new=n_new, n_chunks=n_chunks)
    return pl.pallas_call(
        body,
        grid_spec=grid_spec,
        out_shape=jax.ShapeDtypeStruct((bs, n_new, d_out), BF16),
        compiler_params=_cparams(1),
        name="attn_sample",
    )(page_table.reshape(-1), q_s, k_new, wuv_flat, g_attn, cache_lat, cache_pe_t)


def _outproj_body(x_ref, ssm_ref, att_ref, w_ref, gate_ref, g2_ref, sh_ref, sc_ref, wr_ref, br_ref,
                  xo_ref, h2_ref, ridx_ref, rw_ref, *, mod_bcast):
    half = ssm_ref.shape[1]
    merged = (jnp.dot(ssm_ref[...], w_ref[:half], preferred_element_type=F32)
              + jnp.dot(att_ref[...], w_ref[half:], preferred_element_type=F32))
    gate = gate_ref[0] if mod_bcast else gate_ref[...]
    sh = sh_ref[0] if mod_bcast else sh_ref[...]
    sc = sc_ref[0] if mod_bcast else sc_ref[...]
    xn = x_ref[...] + gate * merged
    xo_ref[...] = xn
    h2 = _rms(xn, g2_ref[...]) * (1.0 + sc) + sh
    h2_ref[...] = h2
    logits = jnp.dot(h2.astype(BF16), wr_ref[...], preferred_element_type=F32) + br_ref[...]
    tm = logits.shape[0]
    lane = lax.broadcasted_iota(I32, (tm, LANES), 1)
    lanef = lane.astype(F32)
    lg = jnp.where(lane < MOE_GROUPS, logits, NEG_BIG)
    mg = jnp.max(lg, axis=-1, keepdims=True)
    gsel = jnp.min(jnp.where(lg == mg, lanef, float(LANES)), axis=-1, keepdims=True)
    wg = 1.0 / jnp.sum(jnp.exp(lg - mg), axis=-1, keepdims=True)
    lo = float(MOE_GROUPS) + float(EXPERTS_PER_GROUP) * gsel
    le = jnp.where(lanef >= lo, jnp.where(lanef < lo + float(EXPERTS_PER_GROUP), logits, NEG_BIG), NEG_BIG)
    v1 = jnp.max(le, axis=-1, keepdims=True)
    i1 = jnp.min(jnp.where(le == v1, lanef, float(LANES)), axis=-1, keepdims=True)
    le2 = jnp.where(lanef == i1, NEG_BIG, le)
    v2 = jnp.max(le2, axis=-1, keepdims=True)
    i2 = jnp.min(jnp.where(le2 == v2, lanef, float(LANES)), axis=-1, keepdims=True)
    e2 = jnp.exp(v2 - v1)
    w1 = wg / (1.0 + e2)
    w2 = wg * e2 / (1.0 + e2)
    ex1 = (i1 - float(MOE_GROUPS)).astype(I32)
    ex2 = (i2 - float(MOE_GROUPS)).astype(I32)
    ridx_ref[...] = jnp.where(lane == 0, ex1, jnp.where(lane == 1, ex2, 0))
    rw_ref[...] = jnp.where(lane == 0, w1, jnp.where(lane == 1, w2, 0.0))


def _outproj(x, ssm_n, att_n, w_out, gate1, g2, shift2, scale2, w_route, b_route, *, tm, mod_bcast,
             rows_per_batch, layer):
    r, d = x.shape
    half = ssm_n.shape[1]
    tpb = rows_per_batch // tm if mod_bcast else 1
    if mod_bcast:
        mod_spec = pl.BlockSpec((1, 1, d), lambda i: (i // tpb, 0, 0))
    else:
        mod_spec = pl.BlockSpec((tm, d), lambda i: (0, 0))
    body = functools.partial(_outproj_body, mod_bcast=mod_bcast)
    return pl.pallas_call(
        body,
        grid=(r // tm,),
        in_specs=[pl.BlockSpec((tm, d), lambda i: (i, 0)),
                  pl.BlockSpec((tm, half), lambda i: (i, 0)),
                  pl.BlockSpec((tm, half), lambda i: (i, 0)),
                  pl.BlockSpec((None,) + w_out.shape[1:], lambda i: (layer, 0, 0)),
                  mod_spec,
                  pl.BlockSpec((1, d), lambda i: (0, 0)),
                  mod_spec, mod_spec,
                  pl.BlockSpec((d, LANES), lambda i: (0, 0)),
                  pl.BlockSpec((1, LANES), lambda i: (0, 0))],
        out_specs=[pl.BlockSpec((tm, d), lambda i: (i, 0)),
                   pl.BlockSpec((tm, d), lambda i: (i, 0)),
                   pl.BlockSpec((tm, LANES), lambda i: (i, 0)),
                   pl.BlockSpec((tm, LANES), lambda i: (i, 0))],
        out_shape=[jax.ShapeDtypeStruct((r, d), F32),
                   jax.ShapeDtypeStruct((r, d), F32),
                   jax.ShapeDtypeStruct((r, LANES), I32),
                   jax.ShapeDtypeStruct((r, LANES), F32)],
        compiler_params=_cparams(1),
        name="out_proj_router",
    )(x, ssm_n, att_n, w_out, gate1, g2, shift2, scale2, w_route, b_route)


def _route_meta(e_pairs, tile):
    npairs = e_pairs.shape[0]
    n_tiles = -(-npairs // tile) + N_EXPERTS
    oh = (e_pairs[:, None] == jnp.arange(N_EXPERTS, dtype=I32)[None, :]).astype(I32)
    cs = jnp.cumsum(oh, axis=0)
    rank = jnp.sum(cs * oh, axis=1) - 1
    counts = cs[-1]
    padded = ((counts + tile - 1) // tile) * tile
    ends = jnp.cumsum(padded)
    starts = ends - padded
    pos = jnp.sum(oh * starts[None, :], axis=1) + rank
    n_used = ends[-1] // tile
    tile_start = jnp.arange(n_tiles, dtype=I32) * tile
    tile_e = jnp.sum((tile_start[:, None] >= ends[None, :]).astype(I32), axis=1)
    last_e = jnp.sum((jnp.maximum(n_used - 1, 0) * tile >= ends).astype(I32))
    tile_e = jnp.minimum(tile_e, last_e).astype(I32)
    return pos.astype(I32), tile_e, n_used.astype(I32).reshape(1), n_tiles


def _dispatch_body(pos_ref, h_ref, xs_in_ref, xs_ref, sem, *, tm):
    del xs_in_ref
    base = pl.program_id(0) * tm

    def issue(r, c):
        for k in range(TOP_K):
            p = pos_ref[TOP_K * (base + r) + k]
            pltpu.make_async_copy(h_ref.at[pl.ds(r, 1)], xs_ref.at[pl.ds(p, 1)], sem).start(priority=k)
        return c

    lax.fori_loop(0, tm, issue, 0, unroll=8)
    for k in range(TOP_K):
        pltpu.make_async_copy(h_ref, xs_ref.at[pl.ds(0, tm)], sem).wait()


def _dispatch(pos, h2, xs, *, tm):
    r, d = h2.shape
    grid_spec = pltpu.PrefetchScalarGridSpec(
        num_scalar_prefetch=1,
        grid=(r // tm,),
        in_specs=[pl.BlockSpec((tm, d), lambda i, p: (i, 0)),
                  pl.BlockSpec(memory_space=pl.ANY)],
        out_specs=pl.BlockSpec(memory_space=pl.ANY),
        scratch_shapes=[pltpu.SemaphoreType.DMA(())],
    )
    return pl.pallas_call(
        functools.partial(_dispatch_body, tm=tm),
        grid_spec=grid_spec,
        out_shape=jax.ShapeDtypeStruct(xs.shape, xs.dtype),
        input_output_aliases={2: 0},
        compiler_params=_cparams(1),
        name="moe_dispatch",
    )(pos, h2, xs)


def _moe_gemm_body(te_ref, nu_ref, x_ref, wg_ref, wu_ref, wd_ref, y_ref, wg_s, wu_s, wd_s):
    i = pl.program_id(0)
    e = te_ref[i]
    prev = te_ref[jnp.maximum(i - 1, 0)]

    @pl.when(jnp.logical_or(i == 0, e != prev))
    def _():
        wg_s[...] = wg_ref[0].astype(BF16)
        wu_s[...] = wu_ref[0].astype(BF16)
        wd_s[...] = wd_ref[0].astype(BF16)

    @pl.when(i < nu_ref[0])
    def _():
        xb = x_ref[...].astype(BF16)
        a = jnp.dot(xb, wg_s[...], preferred_element_type=F32)
        b = jnp.dot(xb, wu_s[...], preferred_element_type=F32)
        act = (a * jax.nn.sigmoid(a) * b).astype(BF16)
        y_ref[...] = jnp.dot(act, wd_s[...], preferred_element_type=F32)

    @pl.when(i >= nu_ref[0])
    def _():
        y_ref[...] = jnp.zeros_like(y_ref)


def _moe_gemm(tile_e, n_used, xs, w_g, w_u, w_d, *, layer, tile):
    rows, d = xs.shape
    f = w_g.shape[-1]
    ne = w_g.shape[1]
    n_tiles = rows // tile
    grid_spec = pltpu.PrefetchScalarGridSpec(
        num_scalar_prefetch=2,
        grid=(n_tiles,),
        in_specs=[pl.BlockSpec((tile, d), lambda i, te, nu: (jnp.minimum(i, nu[0] - 1), 0)),
                  pl.BlockSpec((1, d, f), lambda i, te, nu: (layer * ne + te[i], 0, 0)),
                  pl.BlockSpec((1, d, f), lambda i, te, nu: (layer * ne + te[i], 0, 0)),
                  pl.BlockSpec((1, f, d), lambda i, te, nu: (layer * ne + te[i], 0, 0))],
        out_specs=pl.BlockSpec((tile, d), lambda i, te, nu: (i, 0)),
        scratch_shapes=[pltpu.VMEM((d, f), BF16), pltpu.VMEM((d, f), BF16), pltpu.VMEM((f, d), BF16)],
    )
    depth = w_g.shape[0]
    return pl.pallas_call(
        _moe_gemm_body,
        grid_spec=grid_spec,
        out_shape=jax.ShapeDtypeStruct((rows, d), F32),
        compiler_params=_cparams(1),
        name="moe_gemm",
    )(tile_e, n_used, xs, w_g.reshape(depth * ne, d, f), w_u.reshape(depth * ne, d, f),
      w_d.reshape(depth * ne, f, d))


def _combine_body(pos_ref, x_ref, rw_ref, gate_ref, fg_ref, y_hbm, o_ref, ybuf, sem, *, tm, mod_bcast,
                  final_norm):
    base = pl.program_id(0) * tm

    def issue(r, c):
        for k in range(TOP_K):
            p = pos_ref[TOP_K * (base + r) + k]
            pltpu.make_async_copy(y_hbm.at[pl.ds(p, 1)], ybuf.at[k, pl.ds(r, 1)], sem).start(priority=k)
        return c

    lax.fori_loop(0, tm, issue, 0, unroll=8)
    for k in range(TOP_K):
        pltpu.make_async_copy(y_hbm.at[pl.ds(0, tm)], ybuf.at[k], sem).wait()
    gate = gate_ref[0] if mod_bcast else gate_ref[...]
    rw = rw_ref[...]
    moe = rw[:, 0:1] * ybuf[0] + rw[:, 1:2] * ybuf[1]
    xn = x_ref[...] + gate * moe
    o_ref[...] = _rms(xn, fg_ref[...]) if final_norm else xn


def _combine(pos, x, rw, gate2, final_g, y, *, tm, mod_bcast, rows_per_batch, final_norm):
    r, d = x.shape
    tpb = rows_per_batch // tm if mod_bcast else 1
    if mod_bcast:
        mod_spec = pl.BlockSpec((1, 1, d), lambda i, p: (i // tpb, 0, 0))
    else:
        mod_spec = pl.BlockSpec((tm, d), lambda i, p: (0, 0))
    grid_spec = pltpu.PrefetchScalarGridSpec(
        num_scalar_prefetch=1,
        grid=(r // tm,),
        in_specs=[pl.BlockSpec((tm, d), lambda i, p: (i, 0)),
                  pl.BlockSpec((tm, LANES), lambda i, p: (i, 0)),
                  mod_spec,
                  pl.BlockSpec((1, d), lambda i, p: (0, 0)),
                  pl.BlockSpec(memory_space=pl.ANY)],
        out_specs=pl.BlockSpec((tm, d), lambda i, p: (i, 0)),
        scratch_shapes=[pltpu.VMEM((TOP_K, tm, d), F32), pltpu.SemaphoreType.DMA(())],
    )
    return pl.pallas_call(
        functools.partial(_combine_body, tm=tm, mod_bcast=mod_bcast, final_norm=final_norm),
        grid_spec=grid_spec,
        out_shape=jax.ShapeDtypeStruct((r, d), F32),
        compiler_params=_cparams(1),
        name="moe_combine",
    )(pos, x, rw, gate2, final_g, y)


def _rope_tables(pos):
    half = QK_ROPE // 2
    inv = ROPE_THETA ** (-jnp.arange(half, dtype=F32) * (2.0 / QK_ROPE))
    ang = pos.astype(F32)[:, None] * inv[None, :]
    c, s = jnp.cos(ang), jnp.sin(ang)
    reps = LANES // half
    cos4 = jnp.tile(c, (1, reps))
    sin4 = jnp.tile(jnp.concatenate([-s, s], axis=-1), (1, reps // 2))
    return cos4, sin4


def _pick_tile(n, pref):
    t = min(pref, n)
    while n % t:
        t //= 2
    return t


def kernel(x_prompt, x_sample, cache_kv_latent, cache_k_rope, state_ssm_re, state_ssm_im, page_table,
           c_prompt, c_sample, norm_mix_g, norm_ffn_g, w_ada, b_ada, w_in, ssm_lam_re, ssm_lam_im,
           ssm_log_dt, ssm_b_re, ssm_b_im, ssm_c_re, ssm_c_im, ssm_d, w_glu, b_glu, kv_norm_g, w_uk,
           w_uv, g_ssm_out, g_attn_out, w_out, w_route_group, b_route_group, w_route_expert,
           b_route_expert, w_exp_gate, w_exp_up, w_exp_down, final_norm_g):
    bp, tp, d = x_prompt.shape
    bs, ts_, _ = x_sample.shape
    depth = w_in.shape[0]
    kv_rank = w_uk.shape[1]
    ssm_w = d // 2
    n_groups = ssm_w // SSM_CH
    nst = n_groups * SSM_STATE
    n_pages, page = page_table.shape[1], cache_kv_latent.shape[2]
    past_len = n_pages * page
    rp = bp * tp
    rs = bs * ts_

    w_in_p = w_in.astype(BF16)
    wuk_t = jnp.transpose(w_uk, (0, 2, 3, 1))
    odd_head = (jnp.arange(N_HEADS) % 2 == 1)[None, :, None, None]
    wuk_t = jnp.where(odd_head, jnp.roll(wuk_t, QK_NOPE // 2, axis=2), wuk_t).astype(BF16)
    wuv_tt = jnp.transpose(w_uv, (0, 2, 3, 1)).astype(BF16)
    wuv_flat = w_uv.reshape(depth, kv_rank, N_HEADS * V_HEAD_DIM).astype(BF16)
    w_glu_b = w_glu.astype(BF16)
    w_out_b = w_out.astype(BF16)
    n_route = MOE_GROUPS + N_EXPERTS
    w_route = jnp.concatenate([w_route_group, w_route_expert.reshape(depth, d, N_EXPERTS),
                               jnp.zeros((depth, d, LANES - n_route), F32)], axis=-1).astype(BF16)
    b_route = jnp.concatenate([b_route_group, b_route_expert.reshape(depth, N_EXPERTS),
                               jnp.zeros((depth, LANES - n_route), F32)], axis=-1).reshape(depth, 1, LANES)
    tabs_all = _ssm_tables(ssm_lam_re, ssm_lam_im, ssm_log_dt, ssm_b_re, ssm_b_im, ssm_c_re, ssm_c_im,
                           ssm_d, SSM_CHUNK)
    cos_p, sin_p = _rope_tables(jnp.arange(tp))
    cos_s, sin_s = _rope_tables(past_len + jnp.arange(ts_))
    cos_s = cos_s.reshape(ts_, 1, LANES)
    sin_s = sin_s.reshape(ts_, 1, LANES)

    pad = (-(bs + bp)) % 8
    c_all = jnp.concatenate([c_sample, c_prompt, jnp.zeros((pad, d), F32)], axis=0)
    mod = _ada(c_all, w_ada, b_ada)

    tm_p = _pick_tile(tp, 256)
    ts_ssm = _pick_tile(tp, 512)
    tq = _pick_tile(tp, 512)
    tk = _pick_tile(tp, 512)
    moe_tile = 256
    fg = final_norm_g.reshape(1, d)

    xp = x_prompt.reshape(rp, d)
    xs = jnp.transpose(x_sample, (1, 0, 2)).reshape(rs, d)
    cache_pe_t = jnp.transpose(cache_k_rope, (0, 1, 3, 2))
    h0re = state_ssm_re.reshape(depth, bs, nst)
    h0im = state_ssm_im.reshape(depth, bs, nst)

    lat_p, pe_p, sre_p, sim_p = [], [], [], []
    lat_s, pe_s, sre_s, sim_s = [], [], [], []
    y_prev = None
    for l in range(depth):
        mod_s = [mod[l, :bs, i * d:(i + 1) * d] for i in range(6)]
        mod_p = [mod[l, bs:bs + bp, i * d:(i + 1) * d].reshape(bp, 1, d) for i in range(6)]
        tabs = {k: tabs_all[k][l] for k in ("bbd", "cbd", "a_re", "a_im", "d")}
        g_mix = norm_mix_g[l].reshape(1, d)
        g_ffn = norm_ffn_g[l].reshape(1, d)
        kvg = kv_norm_g[l].reshape(1, kv_rank)
        bglu = b_glu[l].reshape(1, ssm_w)
        gso = g_ssm_out[l].reshape(1, ssm_w)
        gao = g_attn_out[l].reshape(1, N_HEADS * V_HEAD_DIM)

        u_p, q_t, kcat_p, ckv_p, kpe_p, v_t = _inproj(
            xp, g_mix, mod_p[0], mod_p[1], w_in_p, wuk_t[l], kvg, cos_p, sin_p,
            tm=tm_p, mod_bcast=True, rope_bcast=False, rows_per_batch=tp, transposed_q=True, layer=l)
        ssm_p, s_re, s_im = _ssm_prompt(u_p.reshape(bp, tp, ssm_w), tabs_all, w_glu_b, bglu, gso, ts=ts_ssm,
                                        layer=l)
        att_p = _attn_prompt(kcat_p, q_t, v_t, wuv_tt[l], gao, batch=bp, seq=tp, tq=tq, tk=tk,
                             heads_per_group=N_HEADS)
        xp, h2_p, ridx_p, rw_p = _outproj(
            xp, ssm_p.reshape(rp, ssm_w), att_p, w_out_b, mod_p[2], g_ffn, mod_p[3], mod_p[4],
            w_route[l], b_route[l], tm=tm_p, mod_bcast=True, rows_per_batch=tp, layer=l)
        lat_p.append(ckv_p.reshape(bp, tp, kv_rank))
        pe_p.append(kpe_p.reshape(bp, tp, QK_ROPE))
        sre_p.append(s_re.reshape(bp, n_groups, SSM_STATE))
        sim_p.append(s_im.reshape(bp, n_groups, SSM_STATE))

        u_s, q_s, kcat_s, ckv_s, kpe_s = _inproj(
            xs, g_mix, mod_s[0], mod_s[1], w_in_p, wuk_t[l], kvg, cos_s, sin_s,
            tm=bs, mod_bcast=False, rope_bcast=True, rows_per_batch=bs, transposed_q=False, layer=l)
        ssm_s, s_re, s_im = _ssm_sample(u_s, tabs, h0re[l], h0im[l], w_glu_b[l], bglu, gso)
        q_sb = jnp.transpose(q_s.reshape(N_HEADS, ts_, bs, kv_rank + QK_ROPE), (2, 0, 1, 3))
        q_sb = q_sb.reshape(bs, N_HEADS * ts_, kv_rank + QK_ROPE)
        k_new = jnp.transpose(kcat_s.reshape(ts_, bs, kv_rank + QK_ROPE), (1, 0, 2))
        att_s = _attn_sample(page_table, q_sb, k_new, wuv_flat[l], gao, cache_kv_latent, cache_pe_t,
                             layer=l, n_chunks=4)
        att_s = jnp.transpose(att_s, (1, 0, 2)).reshape(rs, N_HEADS * V_HEAD_DIM)
        xs, h2_s, ridx_s, rw_s = _outproj(
            xs, ssm_s, att_s, w_out_b, mod_s[2], g_ffn, mod_s[3], mod_s[4],
            w_route[l], b_route[l], tm=bs, mod_bcast=False, rows_per_batch=bs, layer=l)
        lat_s.append(jnp.transpose(ckv_s.reshape(ts_, bs, kv_rank), (1, 0, 2)))
        pe_s.append(jnp.transpose(kpe_s.reshape(ts_, bs, QK_ROPE), (1, 0, 2)))
        sre_s.append(s_re.reshape(bs, n_groups, SSM_STATE))
        sim_s.append(s_im.reshape(bs, n_groups, SSM_STATE))

        e_pairs = jnp.concatenate([ridx_p[:, :TOP_K].reshape(-1), ridx_s[:, :TOP_K].reshape(-1)])
        pos, tile_e, n_used, n_tiles = _route_meta(e_pairs, moe_tile)
        pos_p, pos_s = pos[:TOP_K * rp], pos[TOP_K * rp:]
        xsrt = jnp.zeros((n_tiles * moe_tile, d), F32) if y_prev is None else y_prev
        xsrt = _dispatch(pos_p, h2_p, xsrt, tm=tm_p)
        xsrt = _dispatch(pos_s, h2_s, xsrt, tm=bs)
        y = _moe_gemm(tile_e, n_used, xsrt, w_exp_gate, w_exp_up, w_exp_down, layer=l, tile=moe_tile)
        last = l == depth - 1
        xp = _combine(pos_p, xp, rw_p, mod_p[5], fg, y, tm=tm_p, mod_bcast=True, rows_per_batch=tp,
                      final_norm=last)
        xs = _combine(pos_s, xs, rw_s, mod_s[5], fg, y, tm=bs, mod_bcast=False, rows_per_batch=bs,
                      final_norm=last)
        y_prev = y

    y_prompt = xp.reshape(bp, tp, d)
    y_sample = jnp.transpose(xs.reshape(ts_, bs, d), (1, 0, 2))
    return (y_prompt, y_sample,
            jnp.stack(lat_p), jnp.stack(pe_p), jnp.stack(sre_p), jnp.stack(sim_p),
            jnp.stack(lat_s), jnp.stack(pe_s), jnp.stack(sre_s), jnp.stack(sim_s))
```

```python
import functools
import math

import numpy as np
import jax
import jax.numpy as jnp
from jax import lax
from jax.experimental import pallas as pl
from jax.experimental.pallas import tpu as pltpu

F32 = jnp.float32
BF16 = jnp.bfloat16
I32 = jnp.int32

SSM_CH = 16
SSM_STATE = 64
N_HEADS = 8
QK_NOPE = 128
QK_ROPE = 64
QK_DIM = QK_NOPE + QK_ROPE
V_HEAD_DIM = 128
ROPE_THETA = 10000.0
EPS = 1e-6
SM_SCALE = QK_DIM ** -0.5
MOE_GROUPS = 4
EXPERTS_PER_GROUP = 4
N_EXPERTS = MOE_GROUPS * EXPERTS_PER_GROUP
TOP_K = 2

LANES = 128
MXU_DIM = 256
VMEM_LIMIT = 56 * 1024 * 1024

GROUPS_PER_BLOCK = MXU_DIM // SSM_CH
SSM_CHUNK = 64
SAMPLE_PREFETCH = 3
NEG_BIG = -1e30


def _cparams(n_axes):
    return pltpu.CompilerParams(dimension_semantics=("arbitrary",) * n_axes,
                                vmem_limit_bytes=VMEM_LIMIT)


def _rms(x, g):
    return x * lax.rsqrt(jnp.mean(x * x, axis=-1, keepdims=True) + EPS) * g


def _ada_body(c_ref, w_ref, b_ref, o_ref):
    c = c_ref[...]
    s = (c * jax.nn.sigmoid(c)).astype(BF16)
    o_ref[0] = jnp.dot(s, w_ref[0].astype(BF16), preferred_element_type=F32) + b_ref[0]


def _ada(c_all, w_ada, b_ada):
    depth, d, n6 = w_ada.shape
    r = c_all.shape[0]
    tn = 1024
    return pl.pallas_call(
        _ada_body,
        grid=(depth, n6 // tn),
        in_specs=[pl.BlockSpec((r, d), lambda l, j: (0, 0)),
                  pl.BlockSpec((1, d, tn), lambda l, j: (l, 0, j)),
                  pl.BlockSpec((1, 1, tn), lambda l, j: (l, 0, j))],
        out_specs=pl.BlockSpec((1, r, tn), lambda l, j: (l, 0, j)),
        out_shape=jax.ShapeDtypeStruct((depth, r, n6), F32),
        compiler_params=_cparams(2),
        name="ada_ln",
    )(c_all, w_ada, b_ada.reshape(depth, 1, n6))


def _rope_pairs(p, cos4, sin4, first_half):
    swapped = jnp.where(first_half, pltpu.roll(p, 96, 1), pltpu.roll(p, 32, 1))
    return p * cos4 + swapped * sin4


def _inproj_body(x_ref, g_ref, sh_ref, sc_ref, w_ref, wuk_ref, kvg_ref, cos_ref, sin_ref,
                 u_ref, q_ref, kcat_ref, ckv_ref, kpe_ref, *maybe_vt_ref,
                 mod_bcast, rope_bcast, ssm_w, kv_rank, transposed_q):
    x = x_ref[...]
    tm = x.shape[0]
    sh = sh_ref[0] if mod_bcast else sh_ref[...]
    sc = sc_ref[0] if mod_bcast else sc_ref[...]
    h = _rms(x, g_ref[...]) * (1.0 + sc) + sh
    z = jnp.dot(h.astype(BF16), w_ref[...], preferred_element_type=F32)
    u_ref[...] = z[:, :ssm_w]
    cos4 = cos_ref[0] if rope_bcast else cos_ref[...]
    sin4 = sin_ref[0] if rope_bcast else sin_ref[...]
    lane = lax.broadcasted_iota(I32, (tm, LANES), 1)
    first_half = (lane % QK_ROPE) < (QK_ROPE // 2)
    off_kv = ssm_w + N_HEADS * QK_DIM
    off_kr = off_kv + kv_rank
    low_half = lane < QK_ROPE
    nopes = []
    for hp in range(N_HEADS // 2):
        base = ssm_w + hp * 2 * QK_DIM
        t0 = z[:, base:base + LANES]
        t1 = z[:, base + LANES:base + 2 * LANES]
        t2 = z[:, base + 2 * LANES:base + 3 * LANES]
        nopes.append(t0)
        nopes.append(jnp.where(low_half, t2, t1))
        pr = jnp.where(low_half, t1, t2)
        rp = _rope_pairs(pr, cos4, sin4, first_half) * SM_SCALE
        if transposed_q:
            rpt = rp.T.astype(BF16)
            q_ref[2 * hp, kv_rank:kv_rank + QK_ROPE, :] = rpt[:QK_ROPE]
            q_ref[2 * hp + 1, kv_rank:kv_rank + QK_ROPE, :] = rpt[QK_ROPE:]
        else:
            rp = rp.astype(BF16)
            q_ref[2 * hp, :, kv_rank:kv_rank + QK_ROPE] = rp[:, :QK_ROPE]
            q_ref[2 * hp + 1, :, kv_rank:kv_rank + QK_ROPE] = rp[:, QK_ROPE:]
    for hd in range(N_HEADS):
        ql = jnp.dot(nopes[hd].astype(BF16), wuk_ref[hd], preferred_element_type=F32) * SM_SCALE
        if transposed_q:
            q_ref[hd, :kv_rank, :] = ql.T.astype(BF16)
        else:
            q_ref[hd, :, :kv_rank] = ql.astype(BF16)
    ckv = _rms(z[:, off_kv:off_kr], kvg_ref[...])
    ckv_ref[...] = ckv
    kr = z[:, off_kr:off_kr + QK_ROPE]
    kk = _rope_pairs(jnp.concatenate([kr, kr], axis=1), cos4, sin4, first_half)
    kpe_ref[...] = kk[:, :QK_ROPE]
    kcat_ref[:, :kv_rank] = ckv.astype(BF16)
    kcat_ref[:, kv_rank:kv_rank + QK_ROPE] = kk[:, :QK_ROPE].astype(BF16)
    if transposed_q:
        maybe_vt_ref[0][...] = ckv.T.astype(BF16)


def _inproj(x, g, shift, scale, w, wuk_t, kvg, cos4, sin4, *, tm, mod_bcast, rope_bcast, rows_per_batch,
            transposed_q, layer):
    r, d = x.shape
    ncols = w.shape[2]
    kv_rank = wuk_t.shape[2]
    ssm_w = d // 2
    qd = kv_rank + QK_ROPE
    tpb = rows_per_batch // tm if mod_bcast else 1
    if mod_bcast:
        mod_spec = pl.BlockSpec((1, 1, d), lambda i: (i // tpb, 0, 0))
    else:
        mod_spec = pl.BlockSpec((tm, d), lambda i: (0, 0))
    if rope_bcast:
        rope_spec = pl.BlockSpec((1, 1, LANES), lambda i: (i, 0, 0))
    else:
        rope_spec = pl.BlockSpec((tm, LANES), lambda i: (i % tpb, 0))
    if transposed_q:
        q_spec = pl.BlockSpec((N_HEADS, qd, tm), lambda i: (0, 0, i))
        q_shape = jax.ShapeDtypeStruct((N_HEADS, qd, r), BF16)
    else:
        q_spec = pl.BlockSpec((N_HEADS, tm, qd), lambda i: (0, i, 0))
        q_shape = jax.ShapeDtypeStruct((N_HEADS, r, qd), BF16)
    out_specs = [pl.BlockSpec((tm, ssm_w), lambda i: (i, 0)),
                 q_spec,
                 pl.BlockSpec((tm, qd), lambda i: (i, 0)),
                 pl.BlockSpec((tm, kv_rank), lambda i: (i, 0)),
                 pl.BlockSpec((tm, QK_ROPE), lambda i: (i, 0))]
    out_shape = [jax.ShapeDtypeStruct((r, ssm_w), F32),
                 q_shape,
                 jax.ShapeDtypeStruct((r, qd), BF16),
                 jax.ShapeDtypeStruct((r, kv_rank), F32),
                 jax.ShapeDtypeStruct((r, QK_ROPE), F32)]
    if transposed_q:
        out_specs.append(pl.BlockSpec((kv_rank, tm), lambda i: (0, i)))
        out_shape.append(jax.ShapeDtypeStruct((kv_rank, r), BF16))
    body = functools.partial(_inproj_body, mod_bcast=mod_bcast, rope_bcast=rope_bcast,
                             ssm_w=ssm_w, kv_rank=kv_rank, transposed_q=transposed_q)
    return pl.pallas_call(
        body,
        grid=(r // tm,),
        in_specs=[pl.BlockSpec((tm, d), lambda i: (i, 0)),
                  pl.BlockSpec((1, d), lambda i: (0, 0)),
                  mod_spec, mod_spec,
                  pl.BlockSpec((None, d, ncols), lambda i: (layer, 0, 0)),
                  pl.BlockSpec(wuk_t.shape, lambda i: (0, 0, 0)),
                  pl.BlockSpec((1, kv_rank), lambda i: (0, 0)),
                  rope_spec, rope_spec],
        out_specs=out_specs,
        out_shape=out_shape,
        compiler_params=_cparams(1),
        name="in_proj_mla_prep",
    )(x, g, shift, scale, w, wuk_t, kvg, cos4, sin4)


def _glu_norm(y, wglu_ref, bglu_ref, gout_ref):
    g = jax.nn.gelu(y)
    gate = jnp.dot(g.astype(BF16), wglu_ref[...], preferred_element_type=F32) + bglu_ref[...]
    return _rms(g * jax.nn.sigmoid(gate), gout_ref[...])


def _ssm_prompt_body(u_ref, bbd_ref, cbd_ref, tri_ref, pre_re_ref, pre_im_ref, post_re_ref, post_im_ref,
                     a_re_ref, a_im_ref, d_ref, wglu_ref, bglu_ref, gout_ref,
                     y_ref, sre_ref, sim_ref, bu_scr, h_scr, hprev_scr, yacc_scr, *, n_blocks, chunk):
    tc = pl.program_id(1)
    ts = u_ref.shape[1]
    sl = GROUPS_PER_BLOCK * SSM_STATE

    @pl.when(tc == 0)
    def _():
        hprev_scr[...] = jnp.zeros_like(hprev_scr)

    u = u_ref[0]
    ub = u.astype(BF16)
    tri = tri_ref[...]
    for j in range(n_blocks):
        cols = slice(j * MXU_DIM, (j + 1) * MXU_DIM)
        lanes = slice(j * sl, (j + 1) * sl)
        bu_scr[...] = jnp.dot(ub[:, cols], bbd_ref[j], preferred_element_type=F32)
        a_re = a_re_ref[:, lanes]
        a_im = a_im_ref[:, lanes]

        def sub(s, carry):
            hr, hi = carry
            r0 = s * chunk
            b_re = bu_scr[pl.ds(r0, chunk), 0:sl]
            b_im = bu_scr[pl.ds(r0, chunk), sl:2 * sl]
            p_re = pre_re_ref[:, lanes]
            p_im = pre_im_ref[:, lanes]
            x_re = (p_re * b_re - p_im * b_im).astype(BF16)
            x_im = (p_re * b_im + p_im * b_re).astype(BF16)
            z_re = jnp.dot(tri, x_re, preferred_element_type=F32) + (a_re * hr - a_im * hi)
            z_im = jnp.dot(tri, x_im, preferred_element_type=F32) + (a_re * hi + a_im * hr)
            q_re = post_re_ref[:, lanes]
            q_im = post_im_ref[:, lanes]
            h_re = q_re * z_re - q_im * z_im
            h_im = q_re * z_im + q_im * z_re
            h_scr[pl.ds(r0, chunk), 0:sl] = h_re.astype(BF16)
            h_scr[pl.ds(r0, chunk), sl:2 * sl] = h_im.astype(BF16)
            return h_re[chunk - 1:chunk], h_im[chunk - 1:chunk]

        carry = (hprev_scr[2 * j:2 * j + 1], hprev_scr[2 * j + 1:2 * j + 2])
        for s in range(ts // chunk):
            carry = sub(s, carry)
        hr, hi = carry
        hprev_scr[2 * j:2 * j + 1] = hr
        hprev_scr[2 * j + 1:2 * j + 2] = hi
        sre_ref[0, :, lanes] = hr
        sim_ref[0, :, lanes] = hi
        yacc_scr[:, cols] = (jnp.dot(h_scr[...], cbd_ref[j], preferred_element_type=F32)
                             + d_ref[:, cols] * u[:, cols])
    y_ref[0] = _glu_norm(yacc_scr[...], wglu_ref, bglu_ref, gout_ref).astype(BF16)


def _ssm_prompt(u, tabs, wglu, bglu, gout, *, ts, layer):
    b, t, w = u.shape
    n_blocks = w // MXU_DIM
    sl2 = 2 * GROUPS_PER_BLOCK * SSM_STATE
    nst = (w // SSM_CH) * SSM_STATE
    chunk = SSM_CHUNK
    full2 = lambda shape: pl.BlockSpec(shape, lambda i, j: (0,) * len(shape))
    lsel = lambda shape: pl.BlockSpec((None,) + shape, lambda i, j: (layer,) + (0,) * len(shape))
    body = functools.partial(_ssm_prompt_body, n_blocks=n_blocks, chunk=chunk)
    return pl.pallas_call(
        body,
        grid=(b, t // ts),
        in_specs=[pl.BlockSpec((1, ts, w), lambda i, j: (i, j, 0)),
                  lsel((n_blocks, MXU_DIM, sl2)), lsel((n_blocks, sl2, MXU_DIM)),
                  full2((chunk, chunk)),
                  lsel((chunk, nst)), lsel((chunk, nst)), lsel((chunk, nst)), lsel((chunk, nst)),
                  lsel((1, nst)), lsel((1, nst)), lsel((1, w)),
                  lsel((w, w)), full2((1, w)), full2((1, w))],
        out_specs=[pl.BlockSpec((1, ts, w), lambda i, j: (i, j, 0)),
                   pl.BlockSpec((1, 1, nst), lambda i, j: (i, 0, 0)),
                   pl.BlockSpec((1, 1, nst), lambda i, j: (i, 0, 0))],
        out_shape=[jax.ShapeDtypeStruct((b, t, w), BF16),
                   jax.ShapeDtypeStruct((b, 1, nst), F32),
                   jax.ShapeDtypeStruct((b, 1, nst), F32)],
        scratch_shapes=[pltpu.VMEM((ts, sl2), F32), pltpu.VMEM((ts, sl2), BF16),
                        pltpu.VMEM((2 * n_blocks, sl2 // 2), F32), pltpu.VMEM((ts, w), F32)],
        compiler_params=_cparams(2),
        name="ssm_prompt",
    )(u, tabs["bbd"], tabs["cbd"], tabs["tri"], tabs["pre_re"], tabs["pre_im"], tabs["post_re"],
      tabs["post_im"], tabs["a_re"], tabs["a_im"], tabs["d"], wglu, bglu, gout)


def _ssm_sample_body(u_ref, bbd_ref, cbd_ref, a_re_ref, a_im_ref, d_ref, h0re_ref, h0im_ref,
                     wglu_ref, bglu_ref, gout_ref, y_ref, sre_ref, sim_ref,
                     bu_scr, h_scr, yacc_scr, *, n_blocks, n_steps):
    sl = GROUPS_PER_BLOCK * SSM_STATE
    bs = h0re_ref.shape[0]
    u = u_ref[...]
    ub = u.astype(BF16)
    for j in range(n_blocks):
        cols = slice(j * MXU_DIM, (j + 1) * MXU_DIM)
        lanes = slice(j * sl, (j + 1) * sl)
        bu_scr[...] = jnp.dot(ub[:, cols], bbd_ref[j], preferred_element_type=F32)
        a_re = a_re_ref[:, lanes]
        a_im = a_im_ref[:, lanes]
        hr = h0re_ref[:, lanes]
        hi = h0im_ref[:, lanes]
        for t in range(n_steps):
            rows = slice(t * bs, (t + 1) * bs)
            nr = (a_re * hr - a_im * hi) + bu_scr[rows, 0:sl]
            ni = (a_re * hi + a_im * hr) + bu_scr[rows, sl:2 * sl]
            hr, hi = nr, ni
            h_scr[rows, 0:sl] = hr.astype(BF16)
            h_scr[rows, sl:2 * sl] = hi.astype(BF16)
        sre_ref[:, lanes] = hr
        sim_ref[:, lanes] = hi
        yacc_scr[:, cols] = (jnp.dot(h_scr[...], cbd_ref[j], preferred_element_type=F32)
                             + d_ref[:, cols] * u[:, cols])
    y_ref[...] = _glu_norm(yacc_scr[...], wglu_ref, bglu_ref, gout_ref).astype(BF16)


def _ssm_sample(u, tabs, h0re, h0im, wglu, bglu, gout):
    r, w = u.shape
    bs, nst = h0re.shape
    n_blocks = w // MXU_DIM
    sl2 = 2 * GROUPS_PER_BLOCK * SSM_STATE
    body = functools.partial(_ssm_sample_body, n_blocks=n_blocks, n_steps=r // bs)
    return pl.pallas_call(
        body,
        out_shape=[jax.ShapeDtypeStruct((r, w), BF16),
                   jax.ShapeDtypeStruct((bs, nst), F32),
                   jax.ShapeDtypeStruct((bs, nst), F32)],
        scratch_shapes=[pltpu.VMEM((r, sl2), F32), pltpu.VMEM((r, sl2), BF16), pltpu.VMEM((r, w), F32)],
        compiler_params=pltpu.CompilerParams(vmem_limit_bytes=VMEM_LIMIT),
        name="ssm_sample",
    )(u, tabs["bbd"], tabs["cbd"], tabs["a_re"], tabs["a_im"], tabs["d"], h0re, h0im, wglu, bglu, gout)


def _ssm_tables(lam_re, lam_im, log_dt, b_re, b_im, c_re, c_im, d_skip, chunk):
    depth, g, n = lam_re.shape
    p = b_re.shape[-1]
    gb = GROUPS_PER_BLOCK
    nb = g // gb
    lr = lam_re.astype(F32)
    li = lam_im.astype(F32)
    dt = jnp.exp(log_dt.astype(F32))[..., None]
    mag = jnp.exp(lr * dt)
    ab_re = mag * jnp.cos(li * dt)
    ab_im = mag * jnp.sin(li * dt)
    den = lr * lr + li * li
    f_re = ((ab_re - 1.0) * lr + ab_im * li) / den
    f_im = (ab_im * lr - (ab_re - 1.0) * li) / den
    bb_re = f_re[..., None] * b_re - f_im[..., None] * b_im
    bb_im = f_re[..., None] * b_im + f_im[..., None] * b_re
    ks = np.arange(chunk)
    pw_re = jnp.ones((chunk, depth, g, n), F32)
    pw_im = jnp.zeros((chunk, depth, g, n), F32)
    sq_re, sq_im = ab_re, ab_im
    for bit in range(max(1, int(chunk - 1).bit_length())):
        sel = jnp.asarray(((ks >> bit) & 1).astype(np.float32))[:, None, None, None]
        m_re = sel * sq_re + (1.0 - sel)
        m_im = sel * sq_im
        pw_re, pw_im = pw_re * m_re - pw_im * m_im, pw_re * m_im + pw_im * m_re
        sq_re, sq_im = sq_re * sq_re - sq_im * sq_im, 2.0 * sq_re * sq_im
    nrm = pw_re * pw_re + pw_im * pw_im
    inv_re = pw_re / nrm
    inv_im = -pw_im / nrm
    flat = lambda a: jnp.moveaxis(a, 0, 1).reshape(depth, chunk, g * n)
    mask_in = jnp.asarray((np.arange(gb * p)[:, None] // p) == (np.arange(gb * n)[None, :] // n))
    mask_out = jnp.asarray((np.arange(gb * n)[:, None] // n) == (np.arange(gb * p)[None, :] // p))

    def bdiag_in(bb):
        small = jnp.swapaxes(bb.reshape(depth, nb, gb, n, p), -1, -2).reshape(depth, nb, gb * p, n)
        return jnp.where(mask_in, jnp.tile(small, (1, 1, 1, gb)), 0.0)

    def bdiag_out(cc):
        small = jnp.swapaxes(cc.reshape(depth, nb, gb, p, n), -1, -2).reshape(depth, nb, gb * n, p)
        return jnp.where(mask_out, jnp.tile(small, (1, 1, 1, gb)), 0.0)

    bbd = jnp.concatenate([bdiag_in(bb_re), bdiag_in(bb_im)], axis=-1).astype(BF16)
    cbd = jnp.concatenate([bdiag_out(c_re.astype(F32)), bdiag_out(-c_im.astype(F32))], axis=-2).astype(BF16)
    tri = jnp.asarray(np.tril(np.ones((chunk, chunk), np.float32))).astype(BF16)
    return {
        "bbd": bbd, "cbd": cbd, "tri": tri,
        "pre_re": flat(inv_re), "pre_im": flat(inv_im),
        "post_re": flat(pw_re), "post_im": flat(pw_im),
        "a_re": ab_re.reshape(depth, 1, g * n), "a_im": ab_im.reshape(depth, 1, g * n),
        "d": d_skip.astype(F32).reshape(depth, 1, g * p),
    }


def _attn_prompt_body(qi_ref, ki_ref, last_ref, k_ref, qt_ref, vt_ref, wuvt_ref, g_ref, o_ref,
                      m_scr, l_scr, acc_scr, *, tq, tk, hg):
    pidx = pl.program_id(1)
    qi = qi_ref[pidx]
    ki = ki_ref[pidx]

    @pl.when(ki == 0)
    def _():
        m_scr[...] = jnp.full_like(m_scr, NEG_BIG)
        l_scr[...] = jnp.zeros_like(l_scr)
        acc_scr[...] = jnp.zeros_like(acc_scr)

    k = k_ref[...]
    vt = vt_ref[...]
    kpos = ki * tk + lax.broadcasted_iota(I32, (tk, hg * tq), 0)
    qpos = qi * tq + (lax.broadcasted_iota(I32, (tk, hg * tq), 1) & (tq - 1))
    visible = kpos <= qpos
    for g0 in range(0, N_HEADS, hg):
        cols = slice(g0 * tq, (g0 + hg) * tq)
        qt = jnp.concatenate([qt_ref[hd] for hd in range(g0, g0 + hg)], axis=1)
        st = jnp.dot(k, qt, preferred_element_type=F32)
        st = jnp.where(visible, st, NEG_BIG)
        m_prev = m_scr[:, cols]
        m_new = jnp.maximum(m_prev, jnp.max(st, axis=0, keepdims=True))
        alpha = jnp.exp(m_prev - m_new)
        p = jnp.exp(st - m_new)
        l_scr[:, cols] = alpha * l_scr[:, cols] + jnp.sum(p, axis=0, keepdims=True)
        acc_scr[:, cols] = alpha * acc_scr[:, cols] + jnp.dot(vt, p.astype(BF16),
                                                              preferred_element_type=F32)
        m_scr[:, cols] = m_new

    @pl.when(ki == last_ref[pidx])
    def _():
        outs = []
        for hd in range(N_HEADS):
            cols = slice(hd * tq, (hd + 1) * tq)
            ot = (acc_scr[:, cols] / l_scr[:, cols]).astype(BF16)
            outs.append(jnp.dot(wuvt_ref[hd], ot, preferred_element_type=F32))
        att = jnp.concatenate(outs, axis=0).T
        o_ref[...] = _rms(att, g_ref[...]).astype(BF16)


def _attn_prompt(k_cat, q_t, v_t, wuv_tt, g_attn, *, batch, seq, tq, tk, heads_per_group):
    qd = k_cat.shape[1]
    kv_rank = v_t.shape[0]
    nq = seq // tq
    qi_l, ki_l, last_l = [], [], []
    for qi in range(nq):
        last = (qi * tq + tq - 1) // tk
        for ki in range(last + 1):
            qi_l.append(qi)
            ki_l.append(ki)
            last_l.append(last)
    npairs = len(qi_l)
    qi_a = jnp.asarray(np.array(qi_l, np.int32))
    ki_a = jnp.asarray(np.array(ki_l, np.int32))
    last_a = jnp.asarray(np.array(last_l, np.int32))
    nqb = seq // tq
    nkb = seq // tk
    d_out = N_HEADS * V_HEAD_DIM
    grid_spec = pltpu.PrefetchScalarGridSpec(
        num_scalar_prefetch=3,
        grid=(batch, npairs),
        in_specs=[pl.BlockSpec((tk, qd), lambda b, p, qi, ki, la: (b * nkb + ki[p], 0)),
                  pl.BlockSpec((N_HEADS, qd, tq), lambda b, p, qi, ki, la: (0, 0, b * nqb + qi[p])),
                  pl.BlockSpec((kv_rank, tk), lambda b, p, qi, ki, la: (0, b * nkb + ki[p])),
                  pl.BlockSpec(wuv_tt.shape, lambda b, p, qi, ki, la: (0, 0, 0)),
                  pl.BlockSpec((1, d_out), lambda b, p, qi, ki, la: (0, 0))],
        out_specs=pl.BlockSpec((tq, d_out), lambda b, p, qi, ki, la: (b * nqb + qi[p], 0)),
        scratch_shapes=[pltpu.VMEM((1, N_HEADS * tq), F32), pltpu.VMEM((1, N_HEADS * tq), F32),
                        pltpu.VMEM((kv_rank, N_HEADS * tq), F32)],
    )
    body = functools.partial(_attn_prompt_body, tq=tq, tk=tk, hg=heads_per_group)
    return pl.pallas_call(
        body,
        grid_spec=grid_spec,
        out_shape=jax.ShapeDtypeStruct((batch * seq, d_out), BF16),
        compiler_params=_cparams(2),
        name="attn_prompt",
    )(qi_a, ki_a, last_a, k_cat, q_t, v_t, wuv_tt, g_attn)


def _attn_sample_body(pt_ref, q_ref, knew_ref, wuv_ref, g_ref, lat_hbm, pet_hbm, o_ref,
                      lat_buf, pet_buf, sems, *, layer, n_pages, page, kv_rank, n_new, n_chunks):
    b = pl.program_id(0)
    nb = pl.num_programs(0)

    def page_copies(bb, slot, j):
        pg = pt_ref[bb * n_pages + j]
        return (pltpu.make_async_copy(lat_hbm.at[layer, pg], lat_buf.at[slot, pl.ds(j * page, page)],
                                      sems.at[slot, 0]),
                pltpu.make_async_copy(pet_hbm.at[layer, pg], pet_buf.at[slot, :, pl.ds(j * page, page)],
                                      sems.at[slot, 1]))

    def start_all(bb, slot):
        for j in range(n_pages):
            c0, c1 = page_copies(bb, slot, j)
            c0.start()
            c1.start()

    @pl.when(b == 0)
    def _():
        for ahead in range(SAMPLE_PREFETCH):
            start_all(ahead, ahead)

    slot = b % (SAMPLE_PREFETCH + 1)

    @pl.when(b + SAMPLE_PREFETCH < nb)
    def _():
        start_all(b + SAMPLE_PREFETCH, (b + SAMPLE_PREFETCH) % (SAMPLE_PREFETCH + 1))

    for j in range(n_pages):
        c0, c1 = page_copies(b, slot, j)
        c0.wait()
        c1.wait()

    q = q_ref[0]
    rows = q.shape[0]
    ql = q[:, :kv_rank]
    qp = q[:, kv_rank:]
    contract_last = (((1,), (1,)), ((), ()))
    ck = (n_pages * page) // n_chunks
    ms, ls, os_ = [], [], []
    lats, scores = [], []
    for c in range(n_chunks):
        lat = lat_buf[slot, pl.ds(c * ck, ck), :].astype(BF16)
        pet = pet_buf[slot, :, pl.ds(c * ck, ck)].astype(BF16)
        lats.append(lat)
        scores.append(lax.dot_general(ql, lat, contract_last, preferred_element_type=F32)
                      + jnp.dot(qp, pet, preferred_element_type=F32))
    for lat, s in zip(lats, scores):
        m = jnp.max(s, axis=-1, keepdims=True)
        p = jnp.exp(s - m)
        ms.append(m)
        ls.append(jnp.sum(p, axis=-1, keepdims=True))
        os_.append(jnp.dot(p.astype(BF16), lat, preferred_element_type=F32))
    kn = knew_ref[0]
    sn = lax.dot_general(q, kn, contract_last, preferred_element_type=F32)
    t_row = lax.broadcasted_iota(I32, (rows, n_new), 0) % n_new
    t_col = lax.broadcasted_iota(I32, (rows, n_new), 1)
    sn = jnp.where(t_col <= t_row, sn, NEG_BIG)
    mn = jnp.max(sn, axis=-1, keepdims=True)
    pn = jnp.exp(sn - mn)
    ms.append(mn)
    ls.append(jnp.sum(pn, axis=-1, keepdims=True))
    os_.append(jnp.dot(pn.astype(BF16), kn[:, :kv_rank], preferred_element_type=F32))
    m_all = functools.reduce(jnp.maximum, ms)
    l = jnp.zeros_like(m_all)
    o = jnp.zeros_like(os_[0])
    for mi, li, oi in zip(ms, ls, os_):
        w = jnp.exp(mi - m_all)
        l = l + w * li
        o = o + w * oi
    o = o / l
    res = jnp.dot(o.astype(BF16), wuv_ref[...], preferred_element_type=F32)
    col_head = lax.broadcasted_iota(I32, (n_new, N_HEADS * V_HEAD_DIM), 1) // V_HEAD_DIM
    out = jnp.zeros((n_new, N_HEADS * V_HEAD_DIM), F32)
    for hd in range(N_HEADS):
        out = jnp.where(col_head == hd, res[hd * n_new:(hd + 1) * n_new], out)
    o_ref[0] = _rms(out, g_ref[...]).astype(BF16)


def _attn_sample(page_table, q_s, k_new, wuv_flat, g_attn, cache_lat, cache_pe_t, *, layer, n_chunks):
    bs, rows, qd = q_s.shape
    n_new = k_new.shape[1]
    n_pages = page_table.shape[1]
    page = cache_lat.shape[2]
    kv_rank = cache_lat.shape[3]
    past = n_pages * page
    d_out = N_HEADS * V_HEAD_DIM
    assert bs >= SAMPLE_PREFETCH
    grid_spec = pltpu.PrefetchScalarGridSpec(
        num_scalar_prefetch=1,
        grid=(bs,),
        in_specs=[pl.BlockSpec((1, rows, qd), lambda b, pt: (b, 0, 0)),
                  pl.BlockSpec((1, n_new, qd), lambda b, pt: (b, 0, 0)),
                  pl.BlockSpec(wuv_flat.shape, lambda b, pt: (0, 0)),
                  pl.BlockSpec((1, d_out), lambda b, pt: (0, 0)),
                  pl.BlockSpec(memory_space=pl.ANY),
                  pl.BlockSpec(memory_space=pl.ANY)],
        out_specs=pl.BlockSpec((1, n_new, d_out), lambda b, pt: (b, 0, 0)),
        scratch_shapes=[pltpu.VMEM((SAMPLE_PREFETCH + 1, past, kv_rank), F32),
                        pltpu.VMEM((SAMPLE_PREFETCH + 1, QK_ROPE, past), F32),
                        pltpu.SemaphoreType.DMA((SAMPLE_PREFETCH + 1, 2))],
    )
    body = functools.partial(_attn_sample_body, layer=layer, n_pages=n_pages, page=page,
                             kv_rank=kv_rank, n_new=n_new, n_chunks=n_chunks)
    return pl.pallas_call(
        body,
        grid_spec=grid_spec,
        out_shape=jax.ShapeDtypeStruct((bs, n_new, d_out), BF16),
        compiler_params=_cparams(1),
        name="attn_sample",
    )(page_table.reshape(-1), q_s, k_new, wuv_flat, g_attn, cache_lat, cache_pe_t)


def _outproj_body(x_ref, ssm_ref, att_ref, w_ref, gate_ref, g2_ref, sh_ref, sc_ref, wr_ref, br_ref,
                  xo_ref, h2_ref, ridx_ref, rw_ref, *, mod_bcast):
    half = ssm_ref.shape[1]
    merged = (jnp.dot(ssm_ref[...], w_ref[:half], preferred_element_type=F32)
              + jnp.dot(att_ref[...], w_ref[half:], preferred_element_type=F32))
    gate = gate_ref[0] if mod_bcast else gate_ref[...]
    sh = sh_ref[0] if mod_bcast else sh_ref[...]
    sc = sc_ref[0] if mod_bcast else sc_ref[...]
    xn = x_ref[...] + gate * merged
    xo_ref[...] = xn
    h2 = _rms(xn, g2_ref[...]) * (1.0 + sc) + sh
    h2_ref[...] = h2
    logits = jnp.dot(h2.astype(BF16), wr_ref[...], preferred_element_type=F32) + br_ref[...]
    tm = logits.shape[0]
    lane = lax.broadcasted_iota(I32, (tm, LANES), 1)
    lanef = lane.astype(F32)
    lg = jnp.where(lane < MOE_GROUPS, logits, NEG_BIG)
    mg = jnp.max(lg, axis=-1, keepdims=True)
    gsel = jnp.min(jnp.where(lg == mg, lanef, float(LANES)), axis=-1, keepdims=True)
    wg = 1.0 / jnp.sum(jnp.exp(lg - mg), axis=-1, keepdims=True)
    lo = float(MOE_GROUPS) + float(EXPERTS_PER_GROUP) * gsel
    le = jnp.where(lanef >= lo, jnp.where(lanef < lo + float(EXPERTS_PER_GROUP), logits, NEG_BIG), NEG_BIG)
    v1 = jnp.max(le, axis=-1, keepdims=True)
    i1 = jnp.min(jnp.where(le == v1, lanef, float(LANES)), axis=-1, keepdims=True)
    le2 = jnp.where(lanef == i1, NEG_BIG, le)
    v2 = jnp.max(le2, axis=-1, keepdims=True)
    i2 = jnp.min(jnp.where(le2 == v2, lanef, float(LANES)), axis=-1, keepdims=True)
    e2 = jnp.exp(v2 - v1)
    w1 = wg / (1.0 + e2)
    w2 = wg * e2 / (1.0 + e2)
    ex1 = (i1 - float(MOE_GROUPS)).astype(I32)
    ex2 = (i2 - float(MOE_GROUPS)).astype(I32)
    ridx_ref[...] = jnp.where(lane == 0, ex1, jnp.where(lane == 1, ex2, 0))
    rw_ref[...] = jnp.where(lane == 0, w1, jnp.where(lane == 1, w2, 0.0))


def _outproj(x, ssm_n, att_n, w_out, gate1, g2, shift2, scale2, w_route, b_route, *, tm, mod_bcast,
             rows_per_batch, layer):
    r, d = x.shape
    half = ssm_n.shape[1]
    tpb = rows_per_batch // tm if mod_bcast else 1
    if mod_bcast:
        mod_spec = pl.BlockSpec((1, 1, d), lambda i: (i // tpb, 0, 0))
    else:
        mod_spec = pl.BlockSpec((tm, d), lambda i: (0, 0))
    body = functools.partial(_outproj_body, mod_bcast=mod_bcast)
    return pl.pallas_call(
        body,
        grid=(r // tm,),
        in_specs=[pl.BlockSpec((tm, d), lambda i: (i, 0)),
                  pl.BlockSpec((tm, half), lambda i: (i, 0)),
                  pl.BlockSpec((tm, half), lambda i: (i, 0)),
                  pl.BlockSpec((None,) + w_out.shape[1:], lambda i: (layer, 0, 0)),
                  mod_spec,
                  pl.BlockSpec((1, d), lambda i: (0, 0)),
                  mod_spec, mod_spec,
                  pl.BlockSpec((d, LANES), lambda i: (0, 0)),
                  pl.BlockSpec((1, LANES), lambda i: (0, 0))],
        out_specs=[pl.BlockSpec((tm, d), lambda i: (i, 0)),
                   pl.BlockSpec((tm, d), lambda i: (i, 0)),
                   pl.BlockSpec((tm, LANES), lambda i: (i, 0)),
                   pl.BlockSpec((tm, LANES), lambda i: (i, 0))],
        out_shape=[jax.ShapeDtypeStruct((r, d), F32),
                   jax.ShapeDtypeStruct((r, d), F32),
                   jax.ShapeDtypeStruct((r, LANES), I32),
                   jax.ShapeDtypeStruct((r, LANES), F32)],
        compiler_params=_cparams(1),
        name="out_proj_router",
    )(x, ssm_n, att_n, w_out, gate1, g2, shift2, scale2, w_route, b_route)


def _route_meta(e_pairs, tile):
    npairs = e_pairs.shape[0]
    n_tiles = -(-npairs // tile) + N_EXPERTS
    oh = (e_pairs[:, None] == jnp.arange(N_EXPERTS, dtype=I32)[None, :]).astype(I32)
    cs = jnp.cumsum(oh, axis=0)
    rank = jnp.sum(cs * oh, axis=1) - 1
    counts = cs[-1]
    padded = ((counts + tile - 1) // tile) * tile
    ends = jnp.cumsum(padded)
    starts = ends - padded
    pos = jnp.sum(oh * starts[None, :], axis=1) + rank
    n_used = ends[-1] // tile
    tile_start = jnp.arange(n_tiles, dtype=I32) * tile
    tile_e = jnp.sum((tile_start[:, None] >= ends[None, :]).astype(I32), axis=1)
    last_e = jnp.sum((jnp.maximum(n_used - 1, 0) * tile >= ends).astype(I32))
    tile_e = jnp.minimum(tile_e, last_e).astype(I32)
    return pos.astype(I32), tile_e, n_used.astype(I32).reshape(1), n_tiles


def _dispatch_body(pos_ref, h_ref, xs_in_ref, xs_ref, sem, *, tm):
    del xs_in_ref
    base = pl.program_id(0) * tm

    def issue(r, c):
        for k in range(TOP_K):
            p = pos_ref[TOP_K * (base + r) + k]
            pltpu.make_async_copy(h_ref.at[pl.ds(r, 1)], xs_ref.at[pl.ds(p, 1)], sem).start(priority=k)
        return c

    lax.fori_loop(0, tm, issue, 0, unroll=8)
    for k in range(TOP_K):
        pltpu.make_async_copy(h_ref, xs_ref.at[pl.ds(0, tm)], sem).wait()


def _dispatch(pos, h2, xs, *, tm):
    r, d = h2.shape
    grid_spec = pltpu.PrefetchScalarGridSpec(
        num_scalar_prefetch=1,
        grid=(r // tm,),
        in_specs=[pl.BlockSpec((tm, d), lambda i, p: (i, 0)),
                  pl.BlockSpec(memory_space=pl.ANY)],
        out_specs=pl.BlockSpec(memory_space=pl.ANY),
        scratch_shapes=[pltpu.SemaphoreType.DMA(())],
    )
    return pl.pallas_call(
        functools.partial(_dispatch_body, tm=tm),
        grid_spec=grid_spec,
        out_shape=jax.ShapeDtypeStruct(xs.shape, xs.dtype),
        input_output_aliases={2: 0},
        compiler_params=_cparams(1),
        name="moe_dispatch",
    )(pos, h2, xs)


def _moe_gemm_body(te_ref, nu_ref, x_ref, wg_ref, wu_ref, wd_ref, y_ref, wg_s, wu_s, wd_s):
    i = pl.program_id(0)
    e = te_ref[i]
    prev = te_ref[jnp.maximum(i - 1, 0)]

    @pl.when(jnp.logical_or(i == 0, e != prev))
    def _():
        wg_s[...] = wg_ref[0].astype(BF16)
        wu_s[...] = wu_ref[0].astype(BF16)
        wd_s[...] = wd_ref[0].astype(BF16)

    @pl.when(i < nu_ref[0])
    def _():
        xb = x_ref[...].astype(BF16)
        a = jnp.dot(xb, wg_s[...], preferred_element_type=F32)
        b = jnp.dot(xb, wu_s[...], preferred_element_type=F32)
        act = (a * jax.nn.sigmoid(a) * b).astype(BF16)
        y_ref[...] = jnp.dot(act, wd_s[...], preferred_element_type=F32)

    @pl.when(i >= nu_ref[0])
    def _():
        y_ref[...] = jnp.zeros_like(y_ref)


def _moe_gemm(tile_e, n_used, xs, w_g, w_u, w_d, *, layer, tile):
    rows, d = xs.shape
    f = w_g.shape[-1]
    ne = w_g.shape[1]
    n_tiles = rows // tile
    grid_spec = pltpu.PrefetchScalarGridSpec(
        num_scalar_prefetch=2,
        grid=(n_tiles,),
        in_specs=[pl.BlockSpec((tile, d), lambda i, te, nu: (jnp.minimum(i, nu[0] - 1), 0)),
                  pl.BlockSpec((1, d, f), lambda i, te, nu: (layer * ne + te[i], 0, 0)),
                  pl.BlockSpec((1, d, f), lambda i, te, nu: (layer * ne + te[i], 0, 0)),
                  pl.BlockSpec((1, f, d), lambda i, te, nu: (layer * ne + te[i], 0, 0))],
        out_specs=pl.BlockSpec((tile, d), lambda i, te, nu: (i, 0)),
        scratch_shapes=[pltpu.VMEM((d, f), BF16), pltpu.VMEM((d, f), BF16), pltpu.VMEM((f, d), BF16)],
    )
    depth = w_g.shape[0]
    return pl.pallas_call(
        _moe_gemm_body,
        grid_spec=grid_spec,
        out_shape=jax.ShapeDtypeStruct((rows, d), F32),
        compiler_params=_cparams(1),
        name="moe_gemm",
    )(tile_e, n_used, xs, w_g.reshape(depth * ne, d, f), w_u.reshape(depth * ne, d, f),
      w_d.reshape(depth * ne, f, d))


def _combine_body(pos_ref, x_ref, rw_ref, gate_ref, fg_ref, y_hbm, o_ref, ybuf, sem, *, tm, mod_bcast,
                  final_norm):
    base = pl.program_id(0) * tm

    def issue(r, c):
        for k in range(TOP_K):
            p = pos_ref[TOP_K * (base + r) + k]
            pltpu.make_async_copy(y_hbm.at[pl.ds(p, 1)], ybuf.at[k, pl.ds(r, 1)], sem).start(priority=k)
        return c

    lax.fori_loop(0, tm, issue, 0, unroll=8)
    for k in range(TOP_K):
        pltpu.make_async_copy(y_hbm.at[pl.ds(0, tm)], ybuf.at[k], sem).wait()
    gate = gate_ref[0] if mod_bcast else gate_ref[...]
    rw = rw_ref[...]
    moe = rw[:, 0:1] * ybuf[0] + rw[:, 1:2] * ybuf[1]
    xn = x_ref[...] + gate * moe
    o_ref[...] = _rms(xn, fg_ref[...]) if final_norm else xn


def _combine(pos, x, rw, gate2, final_g, y, *, tm, mod_bcast, rows_per_batch, final_norm):
    r, d = x.shape
    tpb = rows_per_batch // tm if mod_bcast else 1
    if mod_bcast:
        mod_spec = pl.BlockSpec((1, 1, d), lambda i, p: (i // tpb, 0, 0))
    else:
        mod_spec = pl.BlockSpec((tm, d), lambda i, p: (0, 0))
    grid_spec = pltpu.PrefetchScalarGridSpec(
        num_scalar_prefetch=1,
        grid=(r // tm,),
        in_specs=[pl.BlockSpec((tm, d), lambda i, p: (i, 0)),
                  pl.BlockSpec((tm, LANES), lambda i, p: (i, 0)),
                  mod_spec,
                  pl.BlockSpec((1, d), lambda i, p: (0, 0)),
                  pl.BlockSpec(memory_space=pl.ANY)],
        out_specs=pl.BlockSpec((tm, d), lambda i, p: (i, 0)),
        scratch_shapes=[pltpu.VMEM((TOP_K, tm, d), F32), pltpu.SemaphoreType.DMA(())],
    )
    return pl.pallas_call(
        functools.partial(_combine_body, tm=tm, mod_bcast=mod_bcast, final_norm=final_norm),
        grid_spec=grid_spec,
        out_shape=jax.ShapeDtypeStruct((r, d), F32),
        compiler_params=_cparams(1),
        name="moe_combine",
    )(pos, x, rw, gate2, final_g, y)


def _rope_tables(pos):
    half = QK_ROPE // 2
    inv = ROPE_THETA ** (-jnp.arange(half, dtype=F32) * (2.0 / QK_ROPE))
    ang = pos.astype(F32)[:, None] * inv[None, :]
    c, s = jnp.cos(ang), jnp.sin(ang)
    reps = LANES // half
    cos4 = jnp.tile(c, (1, reps))
    sin4 = jnp.tile(jnp.concatenate([-s, s], axis=-1), (1, reps // 2))
    return cos4, sin4


def _pick_tile(n, pref):
    t = min(pref, n)
    while n % t:
        t //= 2
    return t


def kernel(x_prompt, x_sample, cache_kv_latent, cache_k_rope, state_ssm_re, state_ssm_im, page_table,
           c_prompt, c_sample, norm_mix_g, norm_ffn_g, w_ada, b_ada, w_in, ssm_lam_re, ssm_lam_im,
           ssm_log_dt, ssm_b_re, ssm_b_im, ssm_c_re, ssm_c_im, ssm_d, w_glu, b_glu, kv_norm_g, w_uk,
           w_uv, g_ssm_out, g_attn_out, w_out, w_route_group, b_route_group, w_route_expert,
           b_route_expert, w_exp_gate, w_exp_up, w_exp_down, final_norm_g):
    bp, tp, d = x_prompt.shape
    bs, ts_, _ = x_sample.shape
    depth = w_in.shape[0]
    kv_rank = w_uk.shape[1]
    ssm_w = d // 2
    n_groups = ssm_w // SSM_CH
    nst = n_groups * SSM_STATE
    n_pages, page = page_table.shape[1], cache_kv_latent.shape[2]
    past_len = n_pages * page
    rp = bp * tp
    rs = bs * ts_

    w_in_p = w_in.astype(BF16)
    wuk_t = jnp.transpose(w_uk, (0, 2, 3, 1))
    odd_head = (jnp.arange(N_HEADS) % 2 == 1)[None, :, None, None]
    wuk_t = jnp.where(odd_head, jnp.roll(wuk_t, QK_NOPE // 2, axis=2), wuk_t).astype(BF16)
    wuv_tt = jnp.transpose(w_uv, (0, 2, 3, 1)).astype(BF16)
    wuv_flat = w_uv.reshape(depth, kv_rank, N_HEADS * V_HEAD_DIM).astype(BF16)
    w_glu_b = w_glu.astype(BF16)
    w_out_b = w_out.astype(BF16)
    n_route = MOE_GROUPS + N_EXPERTS
    w_route = jnp.concatenate([w_route_group, w_route_expert.reshape(depth, d, N_EXPERTS),
                               jnp.zeros((depth, d, LANES - n_route), F32)], axis=-1).astype(BF16)
    b_route = jnp.concatenate([b_route_group, b_route_expert.reshape(depth, N_EXPERTS),
                               jnp.zeros((depth, LANES - n_route), F32)], axis=-1).reshape(depth, 1, LANES)
    tabs_all = _ssm_tables(ssm_lam_re, ssm_lam_im, ssm_log_dt, ssm_b_re, ssm_b_im, ssm_c_re, ssm_c_im,
                           ssm_d, SSM_CHUNK)
    cos_p, sin_p = _rope_tables(jnp.arange(tp))
    cos_s, sin_s = _rope_tables(past_len + jnp.arange(ts_))
    cos_s = cos_s.reshape(ts_, 1, LANES)
    sin_s = sin_s.reshape(ts_, 1, LANES)

    pad = (-(bs + bp)) % 8
    c_all = jnp.concatenate([c_sample, c_prompt, jnp.zeros((pad, d), F32)], axis=0)
    mod = _ada(c_all, w_ada, b_ada)

    tm_p = _pick_tile(tp, 256)
    ts_ssm = _pick_tile(tp, 512)
    tq = _pick_tile(tp, 512)
    tk = _pick_tile(tp, 512)
    moe_tile = 512
    fg = final_norm_g.reshape(1, d)

    xp = x_prompt.reshape(rp, d)
    xs = jnp.transpose(x_sample, (1, 0, 2)).reshape(rs, d)
    cache_pe_t = jnp.transpose(cache_k_rope, (0, 1, 3, 2))
    h0re = state_ssm_re.reshape(depth, bs, nst)
    h0im = state_ssm_im.reshape(depth, bs, nst)

    lat_p, pe_p, sre_p, sim_p = [], [], [], []
    lat_s, pe_s, sre_s, sim_s = [], [], [], []
    y_prev = None
    for l in range(depth):
        mod_s = [mod[l, :bs, i * d:(i + 1) * d] for i in range(6)]
        mod_p = [mod[l, bs:bs + bp, i * d:(i + 1) * d].reshape(bp, 1, d) for i in range(6)]
        tabs = {k: tabs_all[k][l] for k in ("bbd", "cbd", "a_re", "a_im", "d")}
        g_mix = norm_mix_g[l].reshape(1, d)
        g_ffn = norm_ffn_g[l].reshape(1, d)
        kvg = kv_norm_g[l].reshape(1, kv_rank)
        bglu = b_glu[l].reshape(1, ssm_w)
        gso = g_ssm_out[l].reshape(1, ssm_w)
        gao = g_attn_out[l].reshape(1, N_HEADS * V_HEAD_DIM)

        u_p, q_t, kcat_p, ckv_p, kpe_p, v_t = _inproj(
            xp, g_mix, mod_p[0], mod_p[1], w_in_p, wuk_t[l], kvg, cos_p, sin_p,
            tm=tm_p, mod_bcast=True, rope_bcast=False, rows_per_batch=tp, transposed_q=True, layer=l)
        ssm_p, s_re, s_im = _ssm_prompt(u_p.reshape(bp, tp, ssm_w), tabs_all, w_glu_b, bglu, gso, ts=ts_ssm,
                                        layer=l)
        att_p = _attn_prompt(kcat_p, q_t, v_t, wuv_tt[l], gao, batch=bp, seq=tp, tq=tq, tk=tk,
                             heads_per_group=N_HEADS)
        xp, h2_p, ridx_p, rw_p = _outproj(
            xp, ssm_p.reshape(rp, ssm_w), att_p, w_out_b, mod_p[2], g_ffn, mod_p[3], mod_p[4],
            w_route[l], b_route[l], tm=tm_p, mod_bcast=True, rows_per_batch=tp, layer=l)
        lat_p.append(ckv_p.reshape(bp, tp, kv_rank))
        pe_p.append(kpe_p.reshape(bp, tp, QK_ROPE))
        sre_p.append(s_re.reshape(bp, n_groups, SSM_STATE))
        sim_p.append(s_im.reshape(bp, n_groups, SSM_STATE))

        u_s, q_s, kcat_s, ckv_s, kpe_s = _inproj(
            xs, g_mix, mod_s[0], mod_s[1], w_in_p, wuk_t[l], kvg, cos_s, sin_s,
            tm=bs, mod_bcast=False, rope_bcast=True, rows_per_batch=bs, transposed_q=False, layer=l)
        ssm_s, s_re, s_im = _ssm_sample(u_s, tabs, h0re[l], h0im[l], w_glu_b[l], bglu, gso)
        q_sb = jnp.transpose(q_s.reshape(N_HEADS, ts_, bs, kv_rank + QK_ROPE), (2, 0, 1, 3))
        q_sb = q_sb.reshape(bs, N_HEADS * ts_, kv_rank + QK_ROPE)
        k_new = jnp.transpose(kcat_s.reshape(ts_, bs, kv_rank + QK_ROPE), (1, 0, 2))
        att_s = _attn_sample(page_table, q_sb, k_new, wuv_flat[l], gao, cache_kv_latent, cache_pe_t,
                             layer=l, n_chunks=4)
        att_s = jnp.transpose(att_s, (1, 0, 2)).reshape(rs, N_HEADS * V_HEAD_DIM)
        xs, h2_s, ridx_s, rw_s = _outproj(
            xs, ssm_s, att_s, w_out_b, mod_s[2], g_ffn, mod_s[3], mod_s[4],
            w_route[l], b_route[l], tm=bs, mod_bcast=False, rows_per_batch=bs, layer=l)
        lat_s.append(jnp.transpose(ckv_s.reshape(ts_, bs, kv_rank), (1, 0, 2)))
        pe_s.append(jnp.transpose(kpe_s.reshape(ts_, bs, QK_ROPE), (1, 0, 2)))
        sre_s.append(s_re.reshape(bs, n_groups, SSM_STATE))
        sim_s.append(s_im.reshape(bs, n_groups, SSM_STATE))

        e_pairs = jnp.concatenate([ridx_p[:, :TOP_K].reshape(-1), ridx_s[:, :TOP_K].reshape(-1)])
        pos, tile_e, n_used, n_tiles = _route_meta(e_pairs, moe_tile)
        pos_p, pos_s = pos[:TOP_K * rp], pos[TOP_K * rp:]
        xsrt = jnp.zeros((n_tiles * moe_tile, d), F32) if y_prev is None else y_prev
        xsrt = _dispatch(pos_p, h2_p, xsrt, tm=tm_p)
        xsrt = _dispatch(pos_s, h2_s, xsrt, tm=bs)
        y = _moe_gemm(tile_e, n_used, xsrt, w_exp_gate, w_exp_up, w_exp_down, layer=l, tile=moe_tile)
        last = l == depth - 1
        xp = _combine(pos_p, xp, rw_p, mod_p[5], fg, y, tm=tm_p, mod_bcast=True, rows_per_batch=tp,
                      final_norm=last)
        xs = _combine(pos_s, xs, rw_s, mod_s[5], fg, y, tm=bs, mod_bcast=False, rows_per_batch=bs,
                      final_norm=last)
        y_prev = y

    y_prompt = xp.reshape(bp, tp, d)
    y_sample = jnp.transpose(xs.reshape(ts_, bs, d), (1, 0, 2))
    return (y_prompt, y_sample,
            jnp.stack(lat_p), jnp.stack(pe_p), jnp.stack(sre_p), jnp.stack(sim_p),
            jnp.stack(lat_s), jnp.stack(pe_s), jnp.stack(sre_s), jnp.stack(sim_s))
```
